```python
import math
import jax, jax.numpy as jnp
from jax import lax
import numpy as np

D_MODEL = 2048
BATCH = 4
SEQ = 2048
DEPTH = 1

D_MIX = D_MODEL
HEAD_DIM = 128
ATTN_WIDTH = D_MIX // 2
N_Q_HEADS = ATTN_WIDTH // HEAD_DIM
N_KV_HEADS = 2
KV_GROUP = N_Q_HEADS // N_KV_HEADS
HGRN_WIDTH = D_MIX - ATTN_WIDTH
HGRN_EXPAND = 128
N_HGRN_HEADS = HGRN_WIDTH // HGRN_EXPAND
HGRN_CHUNK = 64
Q_BLOCK = 128
GRID_W = 64
ROPE_THETA = 10000.0
ROPE_AXIS_DIM = HEAD_DIM // 2
N_EXPERTS = 32
TOP_K = 4
D_EXPERT = D_MODEL
SWIGLU_LIMIT = 7.0
SWIGLU_ALPHA = 1.702
EXPERT_BLOCK = 128
NORM_EPS = 1e-6
DEEPNORM_ALPHA = (2 * DEPTH) ** 0.25
DEEPNORM_BETA = (8 * DEPTH) ** -0.25
PROJ_SIZES = (ATTN_WIDTH, N_KV_HEADS * HEAD_DIM, N_KV_HEADS * HEAD_DIM,
              HGRN_WIDTH, HGRN_WIDTH, HGRN_WIDTH, HGRN_WIDTH, HGRN_WIDTH)
PROJ_WIDTH = sum(PROJ_SIZES)
PROJ_OFFSETS = tuple(int(o) for o in np.cumsum(PROJ_SIZES)[:-1])

kernel_name = "hymba_attn_hgrn2_gptoss_moe_deepnorm_adaln"


def layer_norm(x, g=None, b=None):
    xf = x.astype(jnp.float32)
    mu = jnp.mean(xf, axis=-1, keepdims=True)
    var = jnp.mean(jnp.square(xf - mu), axis=-1, keepdims=True)
    y = (xf - mu) * lax.rsqrt(var + NORM_EPS)
    if g is not None:
        y = y * g.astype(jnp.float32) + b.astype(jnp.float32)
    return y.astype(x.dtype)


def rms_norm(x, w):
    xf = x.astype(jnp.float32)
    y = xf * lax.rsqrt(jnp.mean(jnp.square(xf), axis=-1, keepdims=True) + NORM_EPS)
    return (y * w.astype(jnp.float32)).astype(x.dtype)


def modulate(h, shift, scale):
    return h * (1.0 + scale[:, None, :]) + shift[:, None, :]


def axial_angles(seq_len):
    rows = seq_len // GRID_W
    t = jnp.arange(seq_len, dtype=jnp.int32)
    row = (t // GRID_W - rows // 2).astype(jnp.float32)
    col = (t % GRID_W - GRID_W // 2).astype(jnp.float32)
    inv_freq = ROPE_THETA ** (-jnp.arange(0, ROPE_AXIS_DIM, 2, dtype=jnp.float32) / ROPE_AXIS_DIM)
    return row[:, None] * inv_freq[None, :], col[:, None] * inv_freq[None, :]


def rotate(xh, ang):
    c = jnp.cos(ang)[:, None, :]
    s = jnp.sin(ang)[:, None, :]
    x1, x2 = jnp.split(xh, 2, axis=-1)
    return jnp.concatenate([x1 * c - x2 * s, x2 * c + x1 * s], axis=-1)


def apply_axial_rope(x, ang_row, ang_col):
    xf = x.astype(jnp.float32)
    out = jnp.concatenate([rotate(xf[..., :ROPE_AXIS_DIM], ang_row),
                           rotate(xf[..., ROPE_AXIS_DIM:], ang_col)], axis=-1)
    return out.astype(x.dtype)


def block_attention(q, k, v):
    B, HKV, G, S, D = q.shape
    n_blocks = S // Q_BLOCK
    scale = 1.0 / math.sqrt(D)
    qb = q.reshape(B, HKV, G, n_blocks, Q_BLOCK, D).transpose(3, 0, 1, 2, 4, 5)

    def one_block(q_blk):
        s = jnp.einsum('bkgqd,bksd->bkgqs', q_blk, k, preferred_element_type=jnp.float32) * scale
        p = jax.nn.softmax(s, axis=-1)
        return jnp.einsum('bkgqs,bksd->bkgqd', p.astype(v.dtype), v)

    ob = lax.map(one_block, qb)
    return ob.transpose(1, 2, 3, 0, 4, 5).reshape(B, HKV * G, S, D)


def gla_chunked(q, k, v, log_f):
    B, H, S, DK = q.shape
    DV = v.shape[-1]
    nc = S // HGRN_CHUNK
    q, k, v, log_f = (t.reshape(B, H, nc, HGRN_CHUNK, t.shape[-1]) for t in (q, k, v, log_f))
    b = jnp.cumsum(log_f, axis=3)
    b_last = b[:, :, :, -1:, :]
    q_dec = q * jnp.exp(b)
    scores = jnp.einsum('bhnid,bhnjd->bhnij', q_dec, k * jnp.exp(-b))
    mask = jnp.tril(jnp.ones((HGRN_CHUNK, HGRN_CHUNK), dtype=bool))
    intra = jnp.einsum('bhnij,bhnje->bhnie', jnp.where(mask, scores, 0.0), v)
    u = jnp.einsum('bhncd,bhnce->bhnde', k * jnp.exp(b_last - b), v)
    decay = jnp.exp(b_last[:, :, :, 0, :])

    def step(state, inp):
        dec, uu = inp
        return dec[..., None] * state + uu, state

    _, s_prev = lax.scan(step, jnp.zeros((B, H, DK, DV), jnp.float32),
                         (jnp.moveaxis(decay, 2, 0), jnp.moveaxis(u, 2, 0)))
    s_prev = jnp.moveaxis(s_prev, 0, 2)
    inter = jnp.einsum('bhncd,bhnde->bhnce', q_dec, s_prev)
    return (intra + inter).reshape(B, H, S, DV)


def token_mixer(h, w_in, q_norm_w, k_norm_w, attn_norm_w, hgrn_lb, hgrn_norm_w, w_out, layer_idx):
    B, S, _ = h.shape
    proj = jnp.einsum('bsd,dp->bsp', h, w_in)
    q, k, v, q_r, f_fw, f_bw, i_in, g_out = jnp.split(proj, PROJ_OFFSETS, axis=-1)

    q = rms_norm(q.reshape(B, S, N_Q_HEADS, HEAD_DIM), q_norm_w)
    k = rms_norm(k.reshape(B, S, N_KV_HEADS, HEAD_DIM), k_norm_w)
    v = v.reshape(B, S, N_KV_HEADS, HEAD_DIM)
    ang_row, ang_col = axial_angles(S)
    q = apply_axial_rope(q, ang_row, ang_col)
    k = apply_axial_rope(k, ang_row, ang_col)
    q = q.transpose(0, 2, 1, 3).reshape(B, N_KV_HEADS, KV_GROUP, S, HEAD_DIM)
    o_attn = block_attention(q, k.transpose(0, 2, 1, 3), v.transpose(0, 2, 1, 3))
    o_attn = rms_norm(o_attn, attn_norm_w.reshape(N_Q_HEADS, 1, HEAD_DIM))
    o_attn = o_attn.transpose(0, 2, 1, 3).reshape(B, S, ATTN_WIDTH)

    def to_heads(t):
        return t.reshape(B, S, N_HGRN_HEADS, HGRN_EXPAND).transpose(0, 2, 1, 3).astype(jnp.float32)

    q_r = to_heads(jax.nn.silu(q_r))
    v_r = to_heads(i_in)
    lb = jnp.cumsum(jax.nn.softmax(hgrn_lb.astype(jnp.float32), axis=1), axis=1)[:, layer_idx]
    lb = lb.reshape(2, N_HGRN_HEADS, 1, HGRN_EXPAND)

    def gates(f_logit, lb_d):
        fg = lb_d + (1.0 - lb_d) * jax.nn.sigmoid(to_heads(f_logit))
        return 1.0 - fg, jnp.log(fg)

    k_fw, lf_fw = gates(f_fw, lb[0])
    k_bw, lf_bw = gates(f_bw, lb[1])
    flip = lambda t: jnp.flip(t, axis=2)
    o_r = gla_chunked(q_r, k_fw, v_r, lf_fw) + flip(
        gla_chunked(flip(q_r), flip(k_bw), flip(v_r), flip(lf_bw)))
    o_r = rms_norm(o_r, hgrn_norm_w.reshape(N_HGRN_HEADS, 1, HGRN_EXPAND))
    o_r = o_r.transpose(0, 2, 1, 3).reshape(B, S, HGRN_WIDTH) * jax.nn.silu(g_out.astype(jnp.float32))

    mixed = jnp.concatenate([o_attn.astype(h.dtype), o_r.astype(h.dtype)], axis=-1)
    return jnp.einsum('bsm,md->bsd', mixed, w_out)


def clamped_swiglu(hid):
    x_glu, x_lin = jnp.split(hid, 2, axis=-1)
    x_glu = jnp.minimum(x_glu, SWIGLU_LIMIT)
    x_lin = jnp.clip(x_lin, -SWIGLU_LIMIT, SWIGLU_LIMIT)
    return x_glu * jax.nn.sigmoid(SWIGLU_ALPHA * x_glu) * (x_lin + 1.0)


def moe_ffn(h, w_router, b_router, w1, b1, w2, b2):
    B, S, D = h.shape
    N = B * S
    A = N * TOP_K
    xf = h.reshape(N, D)
    logits = (xf @ w_router + b_router).astype(jnp.float32)
    top_v, top_i = lax.top_k(logits, TOP_K)
    gates = jax.nn.softmax(top_v, axis=-1)

    eid = top_i.reshape(A)
    tok = jnp.arange(A, dtype=jnp.int32) // TOP_K
    order = jnp.argsort(eid)
    e_s, tok_s, g_s = eid[order], tok[order], gates.reshape(A)[order]
    counts = jnp.zeros((N_EXPERTS,), jnp.int32).at[eid].add(1)
    padded = (counts + EXPERT_BLOCK - 1) // EXPERT_BLOCK * EXPERT_BLOCK
    start = jnp.cumsum(counts) - counts
    p_end = jnp.cumsum(padded)
    p_start = p_end - padded
    dest = p_start[e_s] + (jnp.arange(A, dtype=jnp.int32) - start[e_s])
    P = A + N_EXPERTS * EXPERT_BLOCK
    n_blocks = P // EXPERT_BLOCK
    buf_tok = jnp.full((P,), N, jnp.int32).at[dest].set(tok_s)
    buf_gate = jnp.zeros((P,), jnp.float32).at[dest].set(g_s)
    blk_start = jnp.arange(n_blocks, dtype=jnp.int32) * EXPERT_BLOCK
    blk_exp = jnp.minimum(jnp.sum(blk_start[:, None] >= p_end[None, :], axis=1), N_EXPERTS - 1)

    x_pad = jnp.concatenate([xf, jnp.zeros((1, D), xf.dtype)], axis=0)
    xb = x_pad[buf_tok].reshape(n_blocks, EXPERT_BLOCK, D)

    def expert_block(args):
        x_blk, e = args
        hid = x_blk @ w1[e] + b1[e]
        return clamped_swiglu(hid) @ w2[e] + b2[e]

    yb = lax.map(expert_block, (xb, blk_exp)).reshape(P, D)
    yb = yb * buf_gate[:, None].astype(yb.dtype)
    out = jnp.zeros((N + 1, D), yb.dtype).at[buf_tok].add(yb)[:N]
    return out.reshape(B, S, D)


def setup_inputs(seed: int = 0) -> dict:
    key = jax.random.key(seed)
    ks = jax.random.split(key, 21)
    f32 = jnp.float32

    def nrm(k, shape, s):
        return s * jax.random.normal(k, shape, f32)

    return {
        "x": nrm(ks[0], (BATCH, SEQ, D_MODEL), 1.0),
        "c": nrm(ks[1], (BATCH, D_MODEL), 1.0),
        "w_ada": nrm(ks[2], (DEPTH, D_MODEL, 6 * D_MODEL), D_MODEL ** -0.5),
        "b_ada": nrm(ks[3], (DEPTH, 6 * D_MODEL), 0.02),
        "w_in": nrm(ks[4], (DEPTH, D_MODEL, PROJ_WIDTH), D_MODEL ** -0.5),
        "q_norm_w": 1.0 + nrm(ks[5], (DEPTH, HEAD_DIM), 0.02),
        "k_norm_w": 1.0 + nrm(ks[6], (DEPTH, HEAD_DIM), 0.02),
        "attn_norm_w": 1.0 + nrm(ks[7], (DEPTH, ATTN_WIDTH), 0.02),
        "hgrn_lb": 1.0 + nrm(ks[8], (2, DEPTH + 1, HGRN_WIDTH), 0.1),
        "hgrn_norm_w": 1.0 + nrm(ks[9], (DEPTH, HGRN_WIDTH), 0.02),
        "w_out": nrm(ks[10], (DEPTH, D_MIX, D_MODEL), DEEPNORM_BETA * D_MIX ** -0.5),
        "ln1_g": 1.0 + nrm(ks[11], (DEPTH, D_MODEL), 0.02),
        "ln1_b": nrm(ks[12], (DEPTH, D_MODEL), 0.02),
        "w_router": nrm(ks[13], (DEPTH, D_MODEL, N_EXPERTS), D_MODEL ** -0.5),
        "b_router": nrm(ks[14], (DEPTH, N_EXPERTS), 0.01),
        "w_exp_in": nrm(ks[15], (DEPTH, N_EXPERTS, D_MODEL, 2 * D_EXPERT), D_MODEL ** -0.5),
        "b_exp_in": nrm(ks[16], (DEPTH, N_EXPERTS, 2 * D_EXPERT), 0.02),
        "w_exp_out": nrm(ks[17], (DEPTH, N_EXPERTS, D_EXPERT, D_MODEL), DEEPNORM_BETA * D_EXPERT ** -0.5),
        "b_exp_out": nrm(ks[18], (DEPTH, N_EXPERTS, D_MODEL), 0.02),
        "ln2_g": 1.0 + nrm(ks[19], (DEPTH, D_MODEL), 0.02),
        "ln2_b": nrm(ks[20], (DEPTH, D_MODEL), 0.02),
    }


def reference(x, c, w_ada, b_ada, w_in, q_norm_w, k_norm_w, attn_norm_w, hgrn_lb, hgrn_norm_w,
              w_out, ln1_g, ln1_b, w_router, b_router, w_exp_in, b_exp_in, w_exp_out, b_exp_out,
              ln2_g, ln2_b):
    c_act = jax.nn.silu(c)
    for l in range(DEPTH):
        mod = c_act @ w_ada[l] + b_ada[l]
        sh1, sc1, g1, sh2, sc2, g2 = jnp.split(mod, 6, axis=-1)
        h = modulate(layer_norm(x), sh1, sc1)
        y = token_mixer(h, w_in[l], q_norm_w[l], k_norm_w[l], attn_norm_w[l], hgrn_lb,
                        hgrn_norm_w[l], w_out[l], l)
        x = layer_norm(DEEPNORM_ALPHA * x + g1[:, None, :] * y, ln1_g[l], ln1_b[l])
        h = modulate(layer_norm(x), sh2, sc2)
        y = moe_ffn(h, w_router[l], b_router[l], w_exp_in[l], b_exp_in[l], w_exp_out[l], b_exp_out[l])
        x = layer_norm(DEEPNORM_ALPHA * x + g2[:, None, :] * y, ln2_g[l], ln2_b[l])
    return x
```

```python
import functools
import math

import numpy as np
import jax
import jax.numpy as jnp
from jax import lax
from jax.experimental import pallas as pl
from jax.experimental.pallas import tpu as pltpu

F32 = jnp.float32
BF16 = jnp.bfloat16

D_MODEL = 2048
BATCH = 4
SEQ = 2048
DEPTH = 1
N_TOK = BATCH * SEQ
HEAD_DIM = 128
ATTN_WIDTH = 1024
N_Q_HEADS = 8
N_KV_HEADS = 2
KV_GROUP = 4
HGRN_WIDTH = 1024
N_HGRN_HEADS = 8
HGRN_CHUNK = 64
GRID_W = 64
ROPE_THETA = 10000.0
ROPE_AXIS_DIM = 64
N_EXPERTS = 32
TOP_K = 4
D_EXPERT = 2048
SWIGLU_LIMIT = 7.0
SWIGLU_ALPHA = 1.702
NORM_EPS = 1e-6
DEEPNORM_ALPHA = (2 * DEPTH) ** 0.25
PROJ_WIDTH = 6656
LANES = 128

COL_Q = 0
COL_K = 8
COL_V = 10
COL_QR = 12
COL_FF = 20
COL_FB = 28
COL_IN = 36
COL_GO = 44

VMEM_LIMIT = 56 * 1024 * 1024

ADA_TN = 1024
PROJ_TM = 512
PROJ_TN = 1664
ATTN_TQ = 256
MIX_TM = 256
MOE_SB = 1024
MOE_TMI = 256
MOE_TH = 256
MOE_G = N_TOK * TOP_K // MOE_SB + N_EXPERTS
FIN_TM = 256


def _cparams(sem):
    return pltpu.CompilerParams(dimension_semantics=sem, vmem_limit_bytes=VMEM_LIMIT)


def _sigmoid(x):
    return 1.0 / (1.0 + jnp.exp(-x))


def _layer_norm(x):
    mu = jnp.mean(x, axis=-1, keepdims=True)
    xc = x - mu
    var = jnp.mean(xc * xc, axis=-1, keepdims=True)
    return xc * lax.rsqrt(var + NORM_EPS)


def _rms(x):
    return x * lax.rsqrt(jnp.mean(x * x, axis=-1, keepdims=True) + NORM_EPS)


def _ada_kernel(c_ref, w_ref, b_ref, o_ref):
    c = c_ref[...]
    ca = c * _sigmoid(c)
    o_ref[...] = jnp.dot(ca.astype(BF16), w_ref[...].astype(BF16),
                         preferred_element_type=F32) + b_ref[...]


def _ada(c_pad, w, b):
    n = w.shape[1]
    return pl.pallas_call(
        _ada_kernel,
        grid=(n // ADA_TN,),
        in_specs=[pl.BlockSpec((8, D_MODEL), lambda j: (0, 0)),
                  pl.BlockSpec((D_MODEL, ADA_TN), lambda j: (0, j)),
                  pl.BlockSpec((1, ADA_TN), lambda j: (0, j))],
        out_specs=pl.BlockSpec((8, ADA_TN), lambda j: (0, j)),
        out_shape=jax.ShapeDtypeStruct((8, n), F32),
        compiler_params=_cparams(("arbitrary",)),
        name="ada",
    )(c_pad, w, b)


def _proj_kernel(x_ref, sc_ref, sh_ref, w_ref, o_ref, h_ref):
    @pl.when(pl.program_id(1) == 0)
    def _():
        h = _layer_norm(x_ref[...]) * (1.0 + sc_ref[...]) + sh_ref[...]
        h_ref[...] = h.astype(BF16)

    o_ref[...] = jnp.dot(h_ref[...], w_ref[...], preferred_element_type=F32)


def _proj(x2, sc, sh, w_bf):
    tiles_per_batch = SEQ // PROJ_TM
    return pl.pallas_call(
        _proj_kernel,
        grid=(N_TOK // PROJ_TM, PROJ_WIDTH // PROJ_TN),
        in_specs=[pl.BlockSpec((PROJ_TM, D_MODEL), lambda i, j: (i, 0)),
                  pl.BlockSpec((None, 1, D_MODEL), lambda i, j: (i // tiles_per_batch, 0, 0)),
                  pl.BlockSpec((None, 1, D_MODEL), lambda i, j: (i // tiles_per_batch, 0, 0)),
                  pl.BlockSpec((D_MODEL, PROJ_TN), lambda i, j: (0, j))],
        out_specs=pl.BlockSpec((PROJ_TM, PROJ_TN), lambda i, j: (i, j)),
        out_shape=jax.ShapeDtypeStruct((N_TOK, PROJ_WIDTH), F32),
        scratch_shapes=[pltpu.VMEM((PROJ_TM, D_MODEL), BF16)],
        compiler_params=_cparams(("arbitrary", "arbitrary")),
        name="proj",
    )(x2, sc, sh, w_bf)


def _rope(x, cos, sin_lo, sin_hi):
    return (x * cos + pltpu.roll(x, 96, axis=1) * sin_lo + pltpu.roll(x, 32, axis=1) * sin_hi)


def _attn_kernel(q_ref, k_ref, v_ref, cq_ref, slq_ref, shq_ref, ck_ref, slk_ref, shk_ref,
                 qw_ref, kw_ref, aw_ref, o_ref, kr_ref, vb_ref):
    @pl.when(pl.program_id(2) == 0)
    def _():
        k = _rms(k_ref[...]) * kw_ref[...]
        kr_ref[...] = _rope(k, ck_ref[...], slk_ref[...], shk_ref[...]).astype(BF16)
        vb_ref[...] = v_ref[...].astype(BF16)

    scale = 1.0 / math.sqrt(HEAD_DIM)
    cq = cq_ref[...]
    slq = slq_ref[...]
    shq = shq_ref[...]
    for h in range(KV_GROUP):
        cols = slice(h * HEAD_DIM, (h + 1) * HEAD_DIM)
        q = _rms(q_ref[:, cols]) * qw_ref[...]
        q = _rope(q, cq, slq, shq) * scale
        s = lax.dot_general(q.astype(BF16), kr_ref[...], (((1,), (1,)), ((), ())),
                            preferred_element_type=F32)
        m = jnp.max(s, axis=-1, keepdims=True)
        p = jnp.exp(s - m)
        l = jnp.sum(p, axis=-1, keepdims=True)
        o = jnp.dot(p.astype(BF16), vb_ref[...], preferred_element_type=F32) / l
        o = _rms(o) * aw_ref[:, cols]
        o_ref[:, cols] = o.astype(BF16)


def _attention(proj, cos, sin_lo, sin_hi, qw, kw, aw):
    nq = SEQ // ATTN_TQ
    gw = KV_GROUP * HEAD_DIM
    tab_q = pl.BlockSpec((ATTN_TQ, HEAD_DIM), lambda b, g, i: (i, 0))
    tab_k = pl.BlockSpec((SEQ, HEAD_DIM), lambda b, g, i: (0, 0))
    return pl.pallas_call(
        _attn_kernel,
        grid=(BATCH, N_KV_HEADS, nq),
        in_specs=[pl.BlockSpec((ATTN_TQ, gw), lambda b, g, i: (b * nq + i, g)),
                  pl.BlockSpec((SEQ, HEAD_DIM), lambda b, g, i: (b, COL_K + g)),
                  pl.BlockSpec((SEQ, HEAD_DIM), lambda b, g, i: (b, COL_V + g)),
                  tab_q, tab_q, tab_q, tab_k, tab_k, tab_k,
                  pl.BlockSpec((1, HEAD_DIM), lambda b, g, i: (0, 0)),
                  pl.BlockSpec((1, HEAD_DIM), lambda b, g, i: (0, 0)),
                  pl.BlockSpec((1, gw), lambda b, g, i: (0, g))],
        out_specs=pl.BlockSpec((ATTN_TQ, gw), lambda b, g, i: (b * nq + i, g)),
        out_shape=jax.ShapeDtypeStruct((N_TOK, ATTN_WIDTH), BF16),
        scratch_shapes=[pltpu.VMEM((SEQ, HEAD_DIM), BF16), pltpu.VMEM((SEQ, HEAD_DIM), BF16)],
        compiler_params=_cparams(("arbitrary", "arbitrary", "arbitrary")),
        name="attn",
    )(proj, proj, proj, cos, sin_lo, sin_hi, cos, sin_lo, sin_hi, qw, kw, aw)


def _hgrn_kernel(qr_ref, ff_ref, fb_ref, iv_ref, go_ref, lb_ref, nw_ref, o_ref, acc_ref):
    C = HGRN_CHUNK
    nc = SEQ // C
    row = lax.broadcasted_iota(jnp.int32, (C, C), 0)
    col = lax.broadcasted_iota(jnp.int32, (C, C), 1)

    def chunk(rows, f_ref, lb, keep, last, state):
        fg = lb + (1.0 - lb) * _sigmoid(f_ref[rows, :])
        kk = 1.0 - fg
        b = jnp.dot(keep.astype(F32), jnp.log(fg), precision=lax.Precision.HIGHEST,
                    preferred_element_type=F32)
        bl = b[last:last + 1, :]
        qx = qr_ref[rows, :]
        qd = (qx * _sigmoid(qx) * jnp.exp(b)).astype(BF16)
        kd = (kk * jnp.exp(-b)).astype(BF16)
        ku = (kk * jnp.exp(bl - b)).astype(BF16)
        v = iv_ref[rows, :].astype(BF16)
        sc = lax.dot_general(qd, kd, (((1,), (1,)), ((), ())), preferred_element_type=F32)
        sc = jnp.where(keep, sc, 0.0).astype(BF16)
        intra = jnp.dot(sc, v, preferred_element_type=F32)
        inter = lax.dot_general(qd, state.astype(BF16), (((1,), (1,)), ((), ())),
                                preferred_element_type=F32)
        u_t = lax.dot_general(v, ku, (((0,), (0,)), ((), ())), preferred_element_type=F32)
        return intra + inter, state * jnp.exp(bl) + u_t

    state0 = jnp.zeros((HEAD_DIM, HEAD_DIM), F32)

    def fw_body(n, state):
        rows = pl.ds(pl.multiple_of(n * C, C), C)
        o, state = chunk(rows, ff_ref, lb_ref[0], row >= col, C - 1, state)
        acc_ref[rows, :] = o
        return state

    lax.fori_loop(0, nc, fw_body, state0)

    def bw_body(n, state):
        rows = pl.ds(pl.multiple_of((nc - 1 - n) * C, C), C)
        o, state = chunk(rows, fb_ref, lb_ref[1], row <= col, 0, state)
        o = _rms(acc_ref[rows, :] + o) * nw_ref[...]
        g = go_ref[rows, :]
        o_ref[rows, :] = (o * (g * _sigmoid(g))).astype(BF16)
        return state

    lax.fori_loop(0, nc, bw_body, state0)


def _hgrn(proj, lb, nw):
    def col(c0):
        return pl.BlockSpec((SEQ, HEAD_DIM), lambda b, h: (b, c0 + h))

    return pl.pallas_call(
        _hgrn_kernel,
        grid=(BATCH, N_HGRN_HEADS),
        in_specs=[col(COL_QR), col(COL_FF), col(COL_FB), col(COL_IN), col(COL_GO),
                  pl.BlockSpec((2, 1, HEAD_DIM), lambda b, h: (0, 0, h)),
                  pl.BlockSpec((1, HEAD_DIM), lambda b, h: (0, h))],
        out_specs=pl.BlockSpec((SEQ, HEAD_DIM), lambda b, h: (b, h)),
        out_shape=jax.ShapeDtypeStruct((N_TOK, HGRN_WIDTH), BF16),
        scratch_shapes=[pltpu.VMEM((SEQ, HEAD_DIM), F32)],
        compiler_params=_cparams(("arbitrary", "arbitrary")),
        name="hgrn",
    )(proj, proj, proj, proj, proj, lb, nw)


def _mix_kernel(oa_ref, or_ref, wa_ref, wr_ref, x_ref, g1_ref, sc_ref, sh_ref, lg_ref, lbias_ref,
                wrt_ref, brt_ref, x1_ref, h2_ref, idx_ref, gate_ref, rank_ref, cnt_ref, carry_ref):
    i = pl.program_id(0)

    @pl.when(i == 0)
    def _():
        carry_ref[...] = jnp.zeros_like(carry_ref)

    y = jnp.dot(oa_ref[...], wa_ref[...], preferred_element_type=F32)
    y = y + jnp.dot(or_ref[...], wr_ref[...], preferred_element_type=F32)
    x1 = _layer_norm(DEEPNORM_ALPHA * x_ref[...] + g1_ref[...] * y) * lg_ref[...] + lbias_ref[...]
    x1_ref[...] = x1
    h2 = (_layer_norm(x1) * (1.0 + sc_ref[...]) + sh_ref[...]).astype(BF16)
    h2_ref[...] = h2
    logits = jnp.dot(h2, wrt_ref[...], preferred_element_type=F32) + brt_ref[...]

    tm = logits.shape[0]
    lane = lax.broadcasted_iota(jnp.int32, (tm, LANES), 1)
    neg = jnp.float32(-jnp.inf)
    work = logits
    vals, sels = [], []
    for _ in range(TOP_K):
        m = jnp.max(work, axis=-1, keepdims=True)
        sel = jnp.min(jnp.where(work == m, lane, LANES), axis=-1, keepdims=True)
        vals.append(m)
        sels.append(sel)
        work = jnp.where(lane == sel, neg, work)
    es = [jnp.exp(v - vals[0]) for v in vals]
    denom = es[0] + es[1] + es[2] + es[3]

    multi = jnp.zeros((tm, LANES), F32)
    for sel in sels:
        multi = multi + jnp.where(lane == sel, 1.0, 0.0)
    r = lax.broadcasted_iota(jnp.int32, (tm, tm), 0)
    c = lax.broadcasted_iota(jnp.int32, (tm, tm), 1)
    strict = jnp.where(r > c, 1.0, 0.0).astype(BF16)
    before = jnp.dot(strict, multi.astype(BF16), preferred_element_type=F32) + carry_ref[...]
    carry_ref[...] = carry_ref[...] + jnp.sum(multi, axis=0, keepdims=True)
    cnt_ref[...] = carry_ref[...]

    idx_out = jnp.zeros((tm, LANES), jnp.int32)
    gate_out = jnp.zeros((tm, LANES), F32)
    rank_out = jnp.zeros((tm, LANES), F32)
    for k in range(TOP_K):
        rk = jnp.sum(jnp.where(lane == sels[k], before, 0.0), axis=-1, keepdims=True)
        idx_out = jnp.where(lane == k, sels[k], idx_out)
        gate_out = jnp.where(lane == k, es[k] / denom, gate_out)
        rank_out = jnp.where(lane == k, rk, rank_out)
    idx_ref[...] = idx_out
    gate_ref[...] = gate_out
    rank_ref[...] = rank_out.astype(jnp.int32)


def _mix(o_attn, o_r, wa, wr, x2, g1, sc2, sh2, ln_g, ln_b, w_rt, b_rt):
    tiles_per_batch = SEQ // MIX_TM
    rows = lambda w: pl.BlockSpec((MIX_TM, w), lambda i: (i, 0))
    full = lambda a, b: pl.BlockSpec((a, b), lambda i: (0, 0))
    per_batch = pl.BlockSpec((None, 1, D_MODEL), lambda i: (i // tiles_per_batch, 0, 0))
    return pl.pallas_call(
        _mix_kernel,
        grid=(N_TOK // MIX_TM,),
        in_specs=[rows(ATTN_WIDTH), rows(HGRN_WIDTH), full(ATTN_WIDTH, D_MODEL), full(HGRN_WIDTH, D_MODEL),
                  rows(D_MODEL), per_batch, per_batch, per_batch, full(1, D_MODEL), full(1, D_MODEL),
                  full(D_MODEL, LANES), full(1, LANES)],
        out_specs=[rows(D_MODEL), rows(D_MODEL), rows(LANES), rows(LANES), rows(LANES), full(1, LANES)],
        out_shape=[jax.ShapeDtypeStruct((N_TOK, D_MODEL), F32),
                   jax.ShapeDtypeStruct((N_TOK, D_MODEL), BF16),
                   jax.ShapeDtypeStruct((N_TOK, LANES), jnp.int32),
                   jax.ShapeDtypeStruct((N_TOK, LANES), F32),
                   jax.ShapeDtypeStruct((N_TOK, LANES), jnp.int32),
                   jax.ShapeDtypeStruct((1, LANES), F32)],
        scratch_shapes=[pltpu.VMEM((1, LANES), F32)],
        compiler_params=_cparams(("arbitrary",)),
        name="mix",
    )(o_attn, o_r, wa, wr, x2, g1, sc2, sh2, ln_g, ln_b, w_rt, b_rt)


def _moe_kernel(se_ref, sr_ref, nt_ref, x_ref, w1g_ref, w1l_ref, b1g_ref, b1l_ref, w2_ref, b2_ref, o_ref):
    g = pl.program_id(0)
    j = pl.program_id(1)
    nt = nt_ref[g]

    @pl.when(nt > 0)
    def _():
        w1g = w1g_ref[...].astype(BF16)
        w1l = w1l_ref[...].astype(BF16)
        w2 = w2_ref[...].astype(BF16)

        def body(i, carry):
            rows = pl.ds(pl.multiple_of(i * MOE_TMI, MOE_TMI), MOE_TMI)
            xs = x_ref[rows, :]
            hg = jnp.dot(xs, w1g, preferred_element_type=F32) + b1g_ref[...]
            hl = jnp.dot(xs, w1l, preferred_element_type=F32) + b1l_ref[...]
            hg = jnp.minimum(hg, SWIGLU_LIMIT)
            hl = jnp.clip(hl, -SWIGLU_LIMIT, SWIGLU_LIMIT)
            act = hg * _sigmoid(SWIGLU_ALPHA * hg) * (hl + 1.0)
            part = jnp.dot(act.astype(BF16), w2, preferred_element_type=F32)

            @pl.when(j == 0)
            def _():
                o_ref[rows, :] = part + b2_ref[...]

            @pl.when(j > 0)
            def _():
                o_ref[rows, :] += part

            return carry

        lax.fori_loop(0, nt, body, 0)

    @pl.when(j == 0)
    def _():
        def zero(i, carry):
            rows = pl.ds(pl.multiple_of(i * MOE_TMI, MOE_TMI), MOE_TMI)
            o_ref[rows, :] = jnp.zeros((MOE_TMI, D_MODEL), F32)
            return carry

        lax.fori_loop(nt, MOE_SB // MOE_TMI, zero, 0)


def _moe(sb_expert, sb_rows, sb_tiles, x_packed, w1, b1, w2, b2):
    nj = D_EXPERT // MOE_TH
    grid_spec = pltpu.PrefetchScalarGridSpec(
        num_scalar_prefetch=3,
        grid=(MOE_G, nj),
        in_specs=[
            pl.BlockSpec((MOE_SB, D_MODEL), lambda g, j, se, sr, nt: (sr[g], 0)),
            pl.BlockSpec((None, D_MODEL, MOE_TH), lambda g, j, se, sr, nt: (se[g], 0, jnp.where(nt[g] > 0, j, nj - 1))),
            pl.BlockSpec((None, D_MODEL, MOE_TH), lambda g, j, se, sr, nt: (se[g], 0, nj + jnp.where(nt[g] > 0, j, nj - 1))),
            pl.BlockSpec((None, 1, MOE_TH), lambda g, j, se, sr, nt: (se[g], 0, jnp.where(nt[g] > 0, j, nj - 1))),
            pl.BlockSpec((None, 1, MOE_TH), lambda g, j, se, sr, nt: (se[g], 0, nj + jnp.where(nt[g] > 0, j, nj - 1))),
            pl.BlockSpec((None, MOE_TH, D_MODEL), lambda g, j, se, sr, nt: (se[g], jnp.where(nt[g] > 0, j, nj - 1), 0)),
            pl.BlockSpec((None, 1, D_MODEL), lambda g, j, se, sr, nt: (se[g], 0, 0)),
        ],
        out_specs=pl.BlockSpec((MOE_SB, D_MODEL), lambda g, j, se, sr, nt: (g, 0)),
    )
    return pl.pallas_call(
        _moe_kernel,
        grid_spec=grid_spec,
        out_shape=jax.ShapeDtypeStruct((MOE_G * MOE_SB, D_MODEL), F32),
        compiler_params=_cparams(("arbitrary", "arbitrary")),
        name="moe",
    )(sb_expert, sb_rows, sb_tiles, x_packed, w1, w1, b1, b1, w2, b2)


def _final_kernel(x1_ref, y_ref, g2_ref, lg_ref, lb_ref, o_ref):
    z = DEEPNORM_ALPHA * x1_ref[...] + g2_ref[...] * y_ref[...]
    o_ref[...] = _layer_norm(z) * lg_ref[...] + lb_ref[...]


def _final(x1, y, g2, ln_g, ln_b):
    tiles_per_batch = SEQ // FIN_TM
    rows = pl.BlockSpec((FIN_TM, D_MODEL), lambda i: (i, 0))
    vec = pl.BlockSpec((1, D_MODEL), lambda i: (0, 0))
    return pl.pallas_call(
        _final_kernel,
        grid=(N_TOK // FIN_TM,),
        in_specs=[rows, rows, pl.BlockSpec((None, 1, D_MODEL), lambda i: (i // tiles_per_batch, 0, 0)), vec, vec],
        out_specs=rows,
        out_shape=jax.ShapeDtypeStruct((N_TOK, D_MODEL), F32),
        compiler_params=_cparams(("arbitrary",)),
        name="final",
    )(x1, y, g2, ln_g, ln_b)


def _rope_tables():
    rows = SEQ // GRID_W
    t = np.arange(SEQ)
    row = (t // GRID_W - rows // 2).astype(np.float32)
    col = (t % GRID_W - GRID_W // 2).astype(np.float32)
    inv_freq = jnp.asarray(ROPE_THETA, F32) ** (-jnp.arange(0, ROPE_AXIS_DIM, 2, dtype=F32) / ROPE_AXIS_DIM)
    ang_row = jnp.asarray(row)[:, None] * inv_freq[None, :]
    ang_col = jnp.asarray(col)[:, None] * inv_freq[None, :]
    zeros = jnp.zeros_like(ang_row)
    cos = jnp.concatenate([jnp.cos(ang_row)] * 2 + [jnp.cos(ang_col)] * 2, axis=-1)
    sin_lo = jnp.concatenate([-jnp.sin(ang_row), zeros, -jnp.sin(ang_col), zeros], axis=-1)
    sin_hi = jnp.concatenate([zeros, jnp.sin(ang_row), zeros, jnp.sin(ang_col)], axis=-1)
    return cos, sin_lo, sin_hi


def _routing(top_i, rank, counts):
    counts = counts.astype(jnp.int32)
    nsb = (counts + MOE_SB - 1) // MOE_SB
    sb_end = jnp.cumsum(nsb)
    sb_start = sb_end - nsb
    dest = sb_start[top_i] * MOE_SB + rank
    g = jnp.arange(MOE_G, dtype=jnp.int32)
    active = g < sb_end[-1]
    e_of_g = jnp.minimum(jnp.sum(g[:, None] >= sb_end[None, :], axis=1), N_EXPERTS - 1).astype(jnp.int32)
    last_e = e_of_g[jnp.maximum(sb_end[-1] - 1, 0)]
    rows_left = counts[e_of_g] - (g - sb_start[e_of_g]) * MOE_SB
    n_rows = jnp.clip(rows_left, 0, MOE_SB)
    tiles = jnp.where(active, (n_rows + MOE_TMI - 1) // MOE_TMI, 0).astype(jnp.int32)
    sb_expert = jnp.where(active, e_of_g, last_e).astype(jnp.int32)
    sb_rows = jnp.where(active, g, jnp.maximum(sb_end[-1] - 1, 0)).astype(jnp.int32)
    return dest, sb_expert, sb_rows, tiles


def kernel(x, c, w_ada, b_ada, w_in, q_norm_w, k_norm_w, attn_norm_w, hgrn_lb, hgrn_norm_w, w_out, ln1_g, ln1_b, w_router, b_router, w_exp_in, b_exp_in, w_exp_out, b_exp_out, ln2_g, ln2_b):
    c_pad = jnp.zeros((8, D_MODEL), F32).at[:BATCH].set(c)
    cos, sin_lo, sin_hi = _rope_tables()
    x2 = x.reshape(N_TOK, D_MODEL)
    for l in range(DEPTH):
        mod = _ada(c_pad, w_ada[l], b_ada[l][None, :])[:BATCH]
        sh1, sc1, g1, sh2, sc2, g2 = [m.reshape(BATCH, 1, D_MODEL) for m in jnp.split(mod, 6, axis=-1)]

        proj = _proj(x2, sc1, sh1, w_in[l].astype(BF16))
        o_attn = _attention(proj, cos, sin_lo, sin_hi, q_norm_w[l][None, :], k_norm_w[l][None, :],
                            attn_norm_w[l][None, :])
        lb = jnp.cumsum(jax.nn.softmax(hgrn_lb.astype(F32), axis=1), axis=1)[:, l]
        o_r = _hgrn(proj, lb.reshape(2, 1, HGRN_WIDTH), hgrn_norm_w[l][None, :])

        w_o = w_out[l].astype(BF16)
        w_rt = jnp.zeros((D_MODEL, LANES), BF16).at[:, :N_EXPERTS].set(w_router[l].astype(BF16))
        b_rt = jnp.full((1, LANES), -1e30, F32).at[0, :N_EXPERTS].set(b_router[l])
        x1, h2, idx, gates, rank, counts = _mix(
            o_attn, o_r, w_o[:ATTN_WIDTH], w_o[ATTN_WIDTH:], x2, g1, sc2, sh2,
            ln1_g[l][None, :], ln1_b[l][None, :], w_rt, b_rt)
        top_i = idx[:, :TOP_K]
        gates = gates[:, :TOP_K]
        rank = rank[:, :TOP_K]

        dest, sb_expert, sb_rows, sb_tiles = _routing(top_i, rank, counts[0, :N_EXPERTS])
        tok = jnp.broadcast_to(jnp.arange(N_TOK, dtype=jnp.int32)[:, None], (N_TOK, TOP_K))
        row_tok = jnp.zeros((MOE_G * MOE_SB,), jnp.int32).at[dest.reshape(-1)].set(tok.reshape(-1))
        x_packed = h2[row_tok]
        y_packed = _moe(sb_expert, sb_rows, sb_tiles, x_packed, w_exp_in[l], b_exp_in[l][:, None, :],
                        w_exp_out[l], b_exp_out[l][:, None, :])
        y = jnp.sum(y_packed[dest] * gates[:, :, None], axis=1)

        x2 = _final(x1, y, g2, ln2_g[l][None, :], ln2_b[l][None, :])
    return x2.reshape(BATCH, SEQ, D_MODEL)
```

```python
import functools
import math

import numpy as np
import jax
import jax.numpy as jnp
from jax import lax
from jax.experimental import pallas as pl
from jax.experimental.pallas import tpu as pltpu

F32 = jnp.float32
BF16 = jnp.bfloat16

D_MODEL = 2048
BATCH = 4
SEQ = 2048
DEPTH = 1
N_TOK = BATCH * SEQ
HEAD_DIM = 128
ATTN_WIDTH = 1024
N_Q_HEADS = 8
N_KV_HEADS = 2
KV_GROUP = 4
HGRN_WIDTH = 1024
N_HGRN_HEADS = 8
HGRN_CHUNK = 64
GRID_W = 64
ROPE_THETA = 10000.0
ROPE_AXIS_DIM = 64
N_EXPERTS = 32
TOP_K = 4
D_EXPERT = 2048
SWIGLU_LIMIT = 7.0
SWIGLU_ALPHA = 1.702
NORM_EPS = 1e-6
DEEPNORM_ALPHA = (2 * DEPTH) ** 0.25
PROJ_WIDTH = 6656
LANES = 128
HI_MASK = -65536
LO_MASK = 65535

COL_Q = 0
COL_K = 8
COL_V = 10
COL_QR = 12
COL_FF = 20
COL_FB = 28
COL_IN = 36
COL_GO = 44

VMEM_LIMIT = 56 * 1024 * 1024

ADA_TN = 1024
PROJ_TM = 512
PROJ_TN = 1664
ATTN_TQ = 256
MIX_TM = 256
MOE_SB = 1024
MOE_TMI = 256
MOE_TH = 256
MOE_G = N_TOK * TOP_K // MOE_SB + N_EXPERTS + 1
FIN_TM = 256


def _cparams(sem):
    return pltpu.CompilerParams(dimension_semantics=sem, vmem_limit_bytes=VMEM_LIMIT)


def _sigmoid(x):
    return 1.0 / (1.0 + jnp.exp(-x))


def _layer_norm(x):
    mu = jnp.mean(x, axis=-1, keepdims=True)
    xc = x - mu
    var = jnp.mean(xc * xc, axis=-1, keepdims=True)
    return xc * lax.rsqrt(var + NORM_EPS)


def _rms(x):
    return x * lax.rsqrt(jnp.mean(x * x, axis=-1, keepdims=True) + NORM_EPS)


def _ada_kernel(c_ref, w_ref, b_ref, o_ref):
    c = c_ref[...]
    ca = c * _sigmoid(c)
    o_ref[...] = jnp.dot(ca.astype(BF16), w_ref[...].astype(BF16),
                         preferred_element_type=F32) + b_ref[...]


def _ada(c_pad, w, b):
    n = w.shape[1]
    return pl.pallas_call(
        _ada_kernel,
        grid=(n // ADA_TN,),
        in_specs=[pl.BlockSpec((8, D_MODEL), lambda j: (0, 0)),
                  pl.BlockSpec((D_MODEL, ADA_TN), lambda j: (0, j)),
                  pl.BlockSpec((1, ADA_TN), lambda j: (0, j))],
        out_specs=pl.BlockSpec((8, ADA_TN), lambda j: (0, j)),
        out_shape=jax.ShapeDtypeStruct((8, n), F32),
        compiler_params=_cparams(("arbitrary",)),
        name="ada",
    )(c_pad, w, b)


def _proj_kernel(x_ref, sc_ref, sh_ref, w_ref, o_ref, h_ref):
    @pl.when(pl.program_id(1) == 0)
    def _():
        h = _layer_norm(x_ref[...]) * (1.0 + sc_ref[...]) + sh_ref[...]
        h_ref[...] = h.astype(BF16)

    o_ref[...] = jnp.dot(h_ref[...], w_ref[...], preferred_element_type=F32)


def _proj(x2, sc, sh, w_bf):
    tiles_per_batch = SEQ // PROJ_TM
    return pl.pallas_call(
        _proj_kernel,
        grid=(N_TOK // PROJ_TM, PROJ_WIDTH // PROJ_TN),
        in_specs=[pl.BlockSpec((PROJ_TM, D_MODEL), lambda i, j: (i, 0)),
                  pl.BlockSpec((None, 1, D_MODEL), lambda i, j: (i // tiles_per_batch, 0, 0)),
                  pl.BlockSpec((None, 1, D_MODEL), lambda i, j: (i // tiles_per_batch, 0, 0)),
                  pl.BlockSpec((D_MODEL, PROJ_TN), lambda i, j: (0, j))],
        out_specs=pl.BlockSpec((PROJ_TM, PROJ_TN), lambda i, j: (i, j)),
        out_shape=jax.ShapeDtypeStruct((N_TOK, PROJ_WIDTH), F32),
        scratch_shapes=[pltpu.VMEM((PROJ_TM, D_MODEL), BF16)],
        compiler_params=_cparams(("arbitrary", "arbitrary")),
        name="proj",
    )(x2, sc, sh, w_bf)


def _rope(x, cos, sin_lo, sin_hi):
    return (x * cos + pltpu.roll(x, 96, axis=1) * sin_lo + pltpu.roll(x, 32, axis=1) * sin_hi)


def _attn_kernel(q_ref, k_ref, v_ref, cq_ref, slq_ref, shq_ref, ck_ref, slk_ref, shk_ref,
                 qw_ref, kw_ref, aw_ref, o_ref, kr_ref, vb_ref):
    @pl.when(pl.program_id(2) == 0)
    def _():
        k = _rms(k_ref[...]) * kw_ref[...]
        kr_ref[...] = _rope(k, ck_ref[...], slk_ref[...], shk_ref[...]).astype(BF16)
        vb_ref[...] = v_ref[...].astype(BF16)

    scale = 1.0 / math.sqrt(HEAD_DIM)
    cq = cq_ref[...]
    slq = slq_ref[...]
    shq = shq_ref[...]
    for h in range(KV_GROUP):
        cols = slice(h * HEAD_DIM, (h + 1) * HEAD_DIM)
        q = _rms(q_ref[:, cols]) * qw_ref[...]
        q = _rope(q, cq, slq, shq) * scale
        s = lax.dot_general(q.astype(BF16), kr_ref[...], (((1,), (1,)), ((), ())),
                            preferred_element_type=F32)
        m = jnp.max(s, axis=-1, keepdims=True)
        p = jnp.exp(s - m)
        l = jnp.sum(p, axis=-1, keepdims=True)
        o = jnp.dot(p.astype(BF16), vb_ref[...], preferred_element_type=F32) / l
        o = _rms(o) * aw_ref[:, cols]
        o_ref[:, cols] = o.astype(BF16)


def _attention(proj, cos, sin_lo, sin_hi, qw, kw, aw):
    nq = SEQ // ATTN_TQ
    gw = KV_GROUP * HEAD_DIM
    tab_q = pl.BlockSpec((ATTN_TQ, HEAD_DIM), lambda b, g, i: (i, 0))
    tab_k = pl.BlockSpec((SEQ, HEAD_DIM), lambda b, g, i: (0, 0))
    return pl.pallas_call(
        _attn_kernel,
        grid=(BATCH, N_KV_HEADS, nq),
        in_specs=[pl.BlockSpec((ATTN_TQ, gw), lambda b, g, i: (b * nq + i, g)),
                  pl.BlockSpec((SEQ, HEAD_DIM), lambda b, g, i: (b, COL_K + g)),
                  pl.BlockSpec((SEQ, HEAD_DIM), lambda b, g, i: (b, COL_V + g)),
                  tab_q, tab_q, tab_q, tab_k, tab_k, tab_k,
                  pl.BlockSpec((1, HEAD_DIM), lambda b, g, i: (0, 0)),
                  pl.BlockSpec((1, HEAD_DIM), lambda b, g, i: (0, 0)),
                  pl.BlockSpec((1, gw), lambda b, g, i: (0, g))],
        out_specs=pl.BlockSpec((ATTN_TQ, gw), lambda b, g, i: (b * nq + i, g)),
        out_shape=jax.ShapeDtypeStruct((N_TOK, ATTN_WIDTH), BF16),
        scratch_shapes=[pltpu.VMEM((SEQ, HEAD_DIM), BF16), pltpu.VMEM((SEQ, HEAD_DIM), BF16)],
        compiler_params=_cparams(("arbitrary", "arbitrary", "arbitrary")),
        name="attn",
    )(proj, proj, proj, cos, sin_lo, sin_hi, cos, sin_lo, sin_hi, qw, kw, aw)


def _hgrn_kernel(qr_ref, ff_ref, fb_ref, iv_ref, go_ref, lb_ref, nw_ref, o_ref, acc_ref):
    C = HGRN_CHUNK
    nc = SEQ // C
    row = lax.broadcasted_iota(jnp.int32, (C, C), 0)
    col = lax.broadcasted_iota(jnp.int32, (C, C), 1)

    def chunk(rows, f_ref, lb, keep, last, state):
        fg = lb + (1.0 - lb) * _sigmoid(f_ref[rows, :])
        kk = 1.0 - fg
        b = jnp.dot(keep.astype(F32), jnp.log(fg), precision=lax.Precision.HIGHEST,
                    preferred_element_type=F32)
        bl = b[last:last + 1, :]
        qx = qr_ref[rows, :]
        qd = (qx * _sigmoid(qx) * jnp.exp(b)).astype(BF16)
        kd = (kk * jnp.exp(-b)).astype(BF16)
        ku = (kk * jnp.exp(bl - b)).astype(BF16)
        v = iv_ref[rows, :].astype(BF16)
        sc = lax.dot_general(qd, kd, (((1,), (1,)), ((), ())), preferred_element_type=F32)
        sc = jnp.where(keep, sc, 0.0).astype(BF16)
        intra = jnp.dot(sc, v, preferred_element_type=F32)
        inter = lax.dot_general(qd, state.astype(BF16), (((1,), (1,)), ((), ())),
                                preferred_element_type=F32)
        u_t = lax.dot_general(v, ku, (((0,), (0,)), ((), ())), preferred_element_type=F32)
        return intra + inter, state * jnp.exp(bl) + u_t

    state0 = jnp.zeros((HEAD_DIM, HEAD_DIM), F32)

    def fw_body(n, state):
        rows = pl.ds(pl.multiple_of(n * C, C), C)
        o, state = chunk(rows, ff_ref, lb_ref[0], row >= col, C - 1, state)
        acc_ref[rows, :] = o
        return state

    lax.fori_loop(0, nc, fw_body, state0)

    def bw_body(n, state):
        rows = pl.ds(pl.multiple_of((nc - 1 - n) * C, C), C)
        o, state = chunk(rows, fb_ref, lb_ref[1], row <= col, 0, state)
        o = _rms(acc_ref[rows, :] + o) * nw_ref[...]
        g = go_ref[rows, :]
        o_ref[rows, :] = (o * (g * _sigmoid(g))).astype(BF16)
        return state

    lax.fori_loop(0, nc, bw_body, state0)


def _hgrn(proj, lb, nw):
    def col(c0):
        return pl.BlockSpec((SEQ, HEAD_DIM), lambda b, h: (b, c0 + h))

    return pl.pallas_call(
        _hgrn_kernel,
        grid=(BATCH, N_HGRN_HEADS),
        in_specs=[col(COL_QR), col(COL_FF), col(COL_FB), col(COL_IN), col(COL_GO),
                  pl.BlockSpec((2, 1, HEAD_DIM), lambda b, h: (0, 0, h)),
                  pl.BlockSpec((1, HEAD_DIM), lambda b, h: (0, h))],
        out_specs=pl.BlockSpec((SEQ, HEAD_DIM), lambda b, h: (b, h)),
        out_shape=jax.ShapeDtypeStruct((N_TOK, HGRN_WIDTH), BF16),
        scratch_shapes=[pltpu.VMEM((SEQ, HEAD_DIM), F32)],
        compiler_params=_cparams(("arbitrary", "arbitrary")),
        name="hgrn",
    )(proj, proj, proj, proj, proj, lb, nw)


def _mix_kernel(oa_ref, or_ref, wa_ref, wr_ref, x_ref, g1_ref, sc_ref, sh_ref, lg_ref, lbias_ref,
                wrt_ref, brt_ref, x1_ref, h2_ref, idx_ref, gate_ref, rank_ref, cnt_ref, carry_ref):
    i = pl.program_id(0)

    @pl.when(i == 0)
    def _():
        carry_ref[...] = jnp.zeros_like(carry_ref)

    y = jnp.dot(oa_ref[...], wa_ref[...], preferred_element_type=F32)
    y = y + jnp.dot(or_ref[...], wr_ref[...], preferred_element_type=F32)
    x1 = _layer_norm(DEEPNORM_ALPHA * x_ref[...] + g1_ref[...] * y) * lg_ref[...] + lbias_ref[...]
    x1_ref[...] = x1
    h2 = (_layer_norm(x1) * (1.0 + sc_ref[...]) + sh_ref[...]).astype(BF16)
    half = D_MODEL // 2
    lo_bits = lax.bitcast_convert_type(h2[:, :half].astype(F32), jnp.int32)
    hi_bits = lax.bitcast_convert_type(h2[:, half:].astype(F32), jnp.int32)
    h2_ref[...] = ((lo_bits >> 16) & LO_MASK) | (hi_bits & HI_MASK)
    logits = jnp.dot(h2, wrt_ref[...], preferred_element_type=F32) + brt_ref[...]

    tm = logits.shape[0]
    lane = lax.broadcasted_iota(jnp.int32, (tm, LANES), 1)
    neg = jnp.float32(-jnp.inf)
    work = logits
    vals, sels = [], []
    for _ in range(TOP_K):
        m = jnp.max(work, axis=-1, keepdims=True)
        sel = jnp.min(jnp.where(work == m, lane, LANES), axis=-1, keepdims=True)
        vals.append(m)
        sels.append(sel)
        work = jnp.where(lane == sel, neg, work)
    es = [jnp.exp(v - vals[0]) for v in vals]
    denom = es[0] + es[1] + es[2] + es[3]

    multi = jnp.zeros((tm, LANES), F32)
    for sel in sels:
        multi = multi + jnp.where(lane == sel, 1.0, 0.0)
    r = lax.broadcasted_iota(jnp.int32, (tm, tm), 0)
    c = lax.broadcasted_iota(jnp.int32, (tm, tm), 1)
    strict = jnp.where(r > c, 1.0, 0.0).astype(BF16)
    before = jnp.dot(strict, multi.astype(BF16), preferred_element_type=F32) + carry_ref[...]
    carry_ref[...] = carry_ref[...] + jnp.sum(multi, axis=0, keepdims=True)
    cnt_ref[...] = carry_ref[...]

    idx_out = jnp.zeros((tm, LANES), jnp.int32)
    gate_out = jnp.zeros((tm, LANES), F32)
    rank_out = jnp.zeros((tm, LANES), F32)
    for k in range(TOP_K):
        rk = jnp.sum(jnp.where(lane == sels[k], before, 0.0), axis=-1, keepdims=True)
        idx_out = jnp.where(lane == k, sels[k], idx_out)
        gate_out = jnp.where(lane == k, es[k] / denom, gate_out)
        rank_out = jnp.where(lane == k, rk, rank_out)
    idx_ref[...] = idx_out
    gate_ref[...] = gate_out
    rank_ref[...] = rank_out.astype(jnp.int32)


def _mix(o_attn, o_r, wa, wr, x2, g1, sc2, sh2, ln_g, ln_b, w_rt, b_rt):
    tiles_per_batch = SEQ // MIX_TM
    rows = lambda w: pl.BlockSpec((MIX_TM, w), lambda i: (i, 0))
    full = lambda a, b: pl.BlockSpec((a, b), lambda i: (0, 0))
    per_batch = pl.BlockSpec((None, 1, D_MODEL), lambda i: (i // tiles_per_batch, 0, 0))
    return pl.pallas_call(
        _mix_kernel,
        grid=(N_TOK // MIX_TM,),
        in_specs=[rows(ATTN_WIDTH), rows(HGRN_WIDTH), full(ATTN_WIDTH, D_MODEL), full(HGRN_WIDTH, D_MODEL),
                  rows(D_MODEL), per_batch, per_batch, per_batch, full(1, D_MODEL), full(1, D_MODEL),
                  full(D_MODEL, LANES), full(1, LANES)],
        out_specs=[rows(D_MODEL), rows(D_MODEL // 2), rows(LANES), rows(LANES), rows(LANES), full(1, LANES)],
        out_shape=[jax.ShapeDtypeStruct((N_TOK, D_MODEL), F32),
                   jax.ShapeDtypeStruct((N_TOK, D_MODEL // 2), jnp.int32),
                   jax.ShapeDtypeStruct((N_TOK, LANES), jnp.int32),
                   jax.ShapeDtypeStruct((N_TOK, LANES), F32),
                   jax.ShapeDtypeStruct((N_TOK, LANES), jnp.int32),
                   jax.ShapeDtypeStruct((1, LANES), F32)],
        scratch_shapes=[pltpu.VMEM((1, LANES), F32)],
        compiler_params=_cparams(("arbitrary",)),
        name="mix",
    )(o_attn, o_r, wa, wr, x2, g1, sc2, sh2, ln_g, ln_b, w_rt, b_rt)


def _wait_rows(n, make_copy):
    for bit in range(MOE_SB.bit_length()):
        @pl.when(((n >> bit) & 1) == 1)
        def _(bit=bit):
            make_copy(1 << bit).wait()


def _moe_kernel(se_ref, so_ref, sn_ref, sa_ref, h2_hbm, w1g_ref, w1l_ref, b1g_ref, b1l_ref, w2_ref, b2_ref,
                y_hbm, xg_ref, xb_ref, acc_ref, gsem, ssem):
    g = pl.program_id(0)
    j = pl.program_id(1)
    nj = pl.num_programs(1)
    n = sn_ref[g]
    nt = (n + MOE_TMI - 1) // MOE_TMI
    slot = g % 2
    half = D_MODEL // 2
    chunk = MOE_SB // (D_EXPERT // MOE_TH)

    def gather_copy(tok, r):
        return pltpu.make_async_copy(h2_hbm.at[pl.ds(tok, 1)], xg_ref.at[pl.ds(r, 1)], gsem)

    def scatter_copy(s, r, a):
        return pltpu.make_async_copy(acc_ref.at[s, pl.ds(r, 1)], y_hbm.at[pl.ds(a, 1)], ssem.at[s])

    def issue_gather(off, lo, hi):
        def body(r, carry):
            gather_copy(sa_ref[off + r] // TOP_K, r).start()
            return carry

        lax.fori_loop(lo, hi, body, 0)

    def tile_rows(i):
        return pl.ds(pl.multiple_of(i * MOE_TMI, MOE_TMI), MOE_TMI)

    @pl.when(j == 0)
    def _():
        @pl.when(g == 0)
        def _():
            xg_ref[...] = jnp.zeros_like(xg_ref)
            issue_gather(so_ref[0], 0, sn_ref[0])

        _wait_rows(n, lambda k: pltpu.make_async_copy(h2_hbm.at[pl.ds(0, k)], xg_ref.at[pl.ds(0, k)], gsem))

        @pl.when(g >= 2)
        def _():
            _wait_rows(sn_ref[jnp.maximum(g - 2, 0)],
                       lambda k: pltpu.make_async_copy(acc_ref.at[slot, pl.ds(0, k)], y_hbm.at[pl.ds(0, k)],
                                                       ssem.at[slot]))

        def prep(i, carry):
            rows = tile_rows(i)
            u = xg_ref[rows, :]
            xb_ref[rows, :half] = lax.bitcast_convert_type(u << 16, F32).astype(BF16)
            xb_ref[rows, half:] = lax.bitcast_convert_type(u & HI_MASK, F32).astype(BF16)
            acc_ref[slot, rows, :] = jnp.broadcast_to(b2_ref[...], (MOE_TMI, D_MODEL))
            return carry

        lax.fori_loop(0, nt, prep, 0)

    @pl.when(nt > 0)
    def _():
        w1g = w1g_ref[...].astype(BF16)
        w1l = w1l_ref[...].astype(BF16)
        w2 = w2_ref[...].astype(BF16)

        def body(i, carry):
            rows = tile_rows(i)
            xs = xb_ref[rows, :]
            hg = jnp.dot(xs, w1g, preferred_element_type=F32) + b1g_ref[...]
            hl = jnp.dot(xs, w1l, preferred_element_type=F32) + b1l_ref[...]
            hg = jnp.minimum(hg, SWIGLU_LIMIT)
            hl = jnp.clip(hl, -SWIGLU_LIMIT, SWIGLU_LIMIT)
            act = hg * _sigmoid(SWIGLU_ALPHA * hg) * (hl + 1.0)
            acc_ref[slot, rows, :] += jnp.dot(act.astype(BF16), w2, preferred_element_type=F32)
            return carry

        lax.fori_loop(0, nt, body, 0)

    g_next = jnp.minimum(g + 1, MOE_G - 1)
    n_next = jnp.where(g + 1 < MOE_G, sn_ref[g_next], 0)
    issue_gather(so_ref[g_next], jnp.minimum(j * chunk, n_next), jnp.minimum((j + 1) * chunk, n_next))

    g_prev = jnp.maximum(g - 1, 0)
    n_prev = jnp.where(g >= 1, sn_ref[g_prev], 0)
    off_prev = so_ref[g_prev]

    def scatter_body(r, carry):
        scatter_copy(1 - slot, r, sa_ref[off_prev + r]).start()
        return carry

    lax.fori_loop(jnp.minimum(j * chunk, n_prev), jnp.minimum((j + 1) * chunk, n_prev), scatter_body, 0)

    @pl.when((g == MOE_G - 1) & (j == nj - 1))
    def _():
        _wait_rows(n_prev, lambda k: pltpu.make_async_copy(acc_ref.at[1 - slot, pl.ds(0, k)],
                                                            y_hbm.at[pl.ds(0, k)], ssem.at[1 - slot]))


def _moe(sb_expert, sb_off, sb_n, sorted_assign, h2_packed, w1, b1, w2, b2):
    nj = D_EXPERT // MOE_TH

    def hidden(j, sn, g):
        return jnp.where(sn[g] > 0, j, nj - 1)

    grid_spec = pltpu.PrefetchScalarGridSpec(
        num_scalar_prefetch=4,
        grid=(MOE_G, nj),
        in_specs=[
            pl.BlockSpec(memory_space=pl.ANY),
            pl.BlockSpec((None, D_MODEL, MOE_TH), lambda g, j, se, so, sn, sa: (se[g], 0, hidden(j, sn, g))),
            pl.BlockSpec((None, D_MODEL, MOE_TH), lambda g, j, se, so, sn, sa: (se[g], 0, nj + hidden(j, sn, g))),
            pl.BlockSpec((None, 1, MOE_TH), lambda g, j, se, so, sn, sa: (se[g], 0, hidden(j, sn, g))),
            pl.BlockSpec((None, 1, MOE_TH), lambda g, j, se, so, sn, sa: (se[g], 0, nj + hidden(j, sn, g))),
            pl.BlockSpec((None, MOE_TH, D_MODEL), lambda g, j, se, so, sn, sa: (se[g], hidden(j, sn, g), 0)),
            pl.BlockSpec((None, 1, D_MODEL), lambda g, j, se, so, sn, sa: (se[g], 0, 0)),
        ],
        out_specs=pl.BlockSpec(memory_space=pl.ANY),
        scratch_shapes=[pltpu.VMEM((MOE_SB, D_MODEL // 2), jnp.int32),
                        pltpu.VMEM((MOE_SB, D_MODEL), BF16),
                        pltpu.VMEM((2, MOE_SB, D_MODEL), F32),
                        pltpu.SemaphoreType.DMA(()),
                        pltpu.SemaphoreType.DMA((2,))],
    )
    return pl.pallas_call(
        _moe_kernel,
        grid_spec=grid_spec,
        out_shape=jax.ShapeDtypeStruct((N_TOK * TOP_K, D_MODEL), F32),
        compiler_params=_cparams(("arbitrary", "arbitrary")),
        name="moe",
    )(sb_expert, sb_off, sb_n, sorted_assign, h2_packed, w1, w1, b1, b1, w2, b2)


def _final_kernel(x1_ref, y_ref, gate_ref, g2_ref, lg_ref, lb_ref, o_ref):
    y = gate_ref[:, 0:1] * y_ref[:, 0:D_MODEL]
    for k in range(1, TOP_K):
        y = y + gate_ref[:, k:k + 1] * y_ref[:, k * D_MODEL:(k + 1) * D_MODEL]
    z = DEEPNORM_ALPHA * x1_ref[...] + g2_ref[...] * y
    o_ref[...] = _layer_norm(z) * lg_ref[...] + lb_ref[...]


def _final(x1, y4, gates, g2, ln_g, ln_b):
    tiles_per_batch = SEQ // FIN_TM
    rows = lambda w: pl.BlockSpec((FIN_TM, w), lambda i: (i, 0))
    vec = pl.BlockSpec((1, D_MODEL), lambda i: (0, 0))
    return pl.pallas_call(
        _final_kernel,
        grid=(N_TOK // FIN_TM,),
        in_specs=[rows(D_MODEL), rows(TOP_K * D_MODEL), rows(LANES),
                  pl.BlockSpec((None, 1, D_MODEL), lambda i: (i // tiles_per_batch, 0, 0)), vec, vec],
        out_specs=rows(D_MODEL),
        out_shape=jax.ShapeDtypeStruct((N_TOK, D_MODEL), F32),
        compiler_params=_cparams(("arbitrary",)),
        name="final",
    )(x1, y4, gates, g2, ln_g, ln_b)


def _rope_tables():
    rows = SEQ // GRID_W
    t = np.arange(SEQ)
    row = (t // GRID_W - rows // 2).astype(np.float32)
    col = (t % GRID_W - GRID_W // 2).astype(np.float32)
    inv_freq = jnp.asarray(ROPE_THETA, F32) ** (-jnp.arange(0, ROPE_AXIS_DIM, 2, dtype=F32) / ROPE_AXIS_DIM)
    ang_row = jnp.asarray(row)[:, None] * inv_freq[None, :]
    ang_col = jnp.asarray(col)[:, None] * inv_freq[None, :]
    zeros = jnp.zeros_like(ang_row)
    cos = jnp.concatenate([jnp.cos(ang_row)] * 2 + [jnp.cos(ang_col)] * 2, axis=-1)
    sin_lo = jnp.concatenate([-jnp.sin(ang_row), zeros, -jnp.sin(ang_col), zeros], axis=-1)
    sin_hi = jnp.concatenate([zeros, jnp.sin(ang_row), zeros, jnp.sin(ang_col)], axis=-1)
    return cos, sin_lo, sin_hi


def _routing(top_i, rank, counts):
    counts = counts.astype(jnp.int32)
    start = jnp.cumsum(counts) - counts
    assign = jnp.arange(N_TOK * TOP_K, dtype=jnp.int32)
    dest = (start[top_i] + rank).reshape(-1)
    sorted_assign = jnp.zeros((N_TOK * TOP_K,), jnp.int32).at[dest].set(assign)
    nsb = (counts + MOE_SB - 1) // MOE_SB
    sb_end = jnp.cumsum(nsb)
    sb_start = sb_end - nsb
    g = jnp.arange(MOE_G, dtype=jnp.int32)
    active = g < sb_end[-1]
    e_of_g = jnp.minimum(jnp.sum(g[:, None] >= sb_end[None, :], axis=1), N_EXPERTS - 1).astype(jnp.int32)
    last_e = e_of_g[jnp.maximum(sb_end[-1] - 1, 0)]
    first_row = (g - sb_start[e_of_g]) * MOE_SB
    sb_n = jnp.where(active, jnp.clip(counts[e_of_g] - first_row, 0, MOE_SB), 0).astype(jnp.int32)
    sb_off = jnp.where(active, start[e_of_g] + first_row, 0).astype(jnp.int32)
    sb_expert = jnp.where(active, e_of_g, last_e).astype(jnp.int32)
    return sb_expert, sb_off, sb_n, sorted_assign


def kernel(x, c, w_ada, b_ada, w_in, q_norm_w, k_norm_w, attn_norm_w, hgrn_lb, hgrn_norm_w, w_out, ln1_g, ln1_b, w_router, b_router, w_exp_in, b_exp_in, w_exp_out, b_exp_out, ln2_g, ln2_b):
    c_pad = jnp.zeros((8, D_MODEL), F32).at[:BATCH].set(c)
    cos, sin_lo, sin_hi = _rope_tables()
    x2 = x.reshape(N_TOK, D_MODEL)
    for l in range(DEPTH):
        mod = _ada(c_pad, w_ada[l], b_ada[l][None, :])[:BATCH]
        sh1, sc1, g1, sh2, sc2, g2 = [m.reshape(BATCH, 1, D_MODEL) for m in jnp.split(mod, 6, axis=-1)]

        proj = _proj(x2, sc1, sh1, w_in[l].astype(BF16))
        o_attn = _attention(proj, cos, sin_lo, sin_hi, q_norm_w[l][None, :], k_norm_w[l][None, :],
                            attn_norm_w[l][None, :])
        lb = jnp.cumsum(jax.nn.softmax(hgrn_lb.astype(F32), axis=1), axis=1)[:, l]
        o_r = _hgrn(proj, lb.reshape(2, 1, HGRN_WIDTH), hgrn_norm_w[l][None, :])

        w_o = w_out[l].astype(BF16)
        w_rt = jnp.zeros((D_MODEL, LANES), BF16).at[:, :N_EXPERTS].set(w_router[l].astype(BF16))
        b_rt = jnp.full((1, LANES), -1e30, F32).at[0, :N_EXPERTS].set(b_router[l])
        x1, h2, idx, gates, rank, counts = _mix(
            o_attn, o_r, w_o[:ATTN_WIDTH], w_o[ATTN_WIDTH:], x2, g1, sc2, sh2,
            ln1_g[l][None, :], ln1_b[l][None, :], w_rt, b_rt)
        sb_expert, sb_off, sb_n, sorted_assign = _routing(idx[:, :TOP_K], rank[:, :TOP_K],
                                                          counts[0, :N_EXPERTS])
        y4 = _moe(sb_expert, sb_off, sb_n, sorted_assign, h2, w_exp_in[l], b_exp_in[l][:, None, :],
                  w_exp_out[l], b_exp_out[l][:, None, :])
        x2 = _final(x1, y4.reshape(N_TOK, TOP_K * D_MODEL), gates, g2, ln2_g[l][None, :], ln2_b[l][None, :])
    return x2.reshape(BATCH, SEQ, D_MODEL)
```

```python
import functools
import math

import numpy as np
import jax
import jax.numpy as jnp
from jax import lax
from jax.experimental import pallas as pl
from jax.experimental.pallas import tpu as pltpu

F32 = jnp.float32
BF16 = jnp.bfloat16

D_MODEL = 2048
BATCH = 4
SEQ = 2048
DEPTH = 1
N_TOK = BATCH * SEQ
HEAD_DIM = 128
ATTN_WIDTH = 1024
N_Q_HEADS = 8
N_KV_HEADS = 2
KV_GROUP = 4
HGRN_WIDTH = 1024
N_HGRN_HEADS = 8
HGRN_CHUNK = 64
GRID_W = 64
ROPE_THETA = 10000.0
ROPE_AXIS_DIM = 64
N_EXPERTS = 32
TOP_K = 4
D_EXPERT = 2048
SWIGLU_LIMIT = 7.0
SWIGLU_ALPHA = 1.702
NORM_EPS = 1e-6
DEEPNORM_ALPHA = (2 * DEPTH) ** 0.25
PROJ_WIDTH = 6656
LANES = 128

COL_Q = 0
COL_K = 8
COL_V = 10
COL_QR = 12
COL_FF = 20
COL_FB = 28
COL_IN = 36
COL_GO = 44

VMEM_LIMIT = 56 * 1024 * 1024

ADA_TN = 1024
PROJ_TM = 512
PROJ_TN = 1664
ATTN_TQ = 256
HGRN_HB = 2
HGRN_UN = 2
MIX_TM = 256
MOE_ISSUE_UNROLL = 8
MOE_SB = 1024
MOE_TMI = 256
MOE_TH = 256
MOE_G = N_TOK * TOP_K // MOE_SB + N_EXPERTS + 1
FIN_TM = 256


def _cparams(sem):
    return pltpu.CompilerParams(dimension_semantics=sem, vmem_limit_bytes=VMEM_LIMIT)


def _sigmoid(x):
    return 1.0 / (1.0 + jnp.exp(-x))


def _layer_norm(x):
    mu = jnp.mean(x, axis=-1, keepdims=True)
    xc = x - mu
    var = jnp.mean(xc * xc, axis=-1, keepdims=True)
    return xc * lax.rsqrt(var + NORM_EPS)


def _rms(x):
    return x * lax.rsqrt(jnp.mean(x * x, axis=-1, keepdims=True) + NORM_EPS)


def _ada_kernel(c_ref, w_ref, b_ref, o_ref):
    c = c_ref[...]
    ca = c * _sigmoid(c)
    o_ref[...] = jnp.dot(ca.astype(BF16), w_ref[...].astype(BF16),
                         preferred_element_type=F32) + b_ref[...]


def _ada(c_pad, w, b):
    n = w.shape[1]
    return pl.pallas_call(
        _ada_kernel,
        grid=(n // ADA_TN,),
        in_specs=[pl.BlockSpec((8, D_MODEL), lambda j: (0, 0)),
                  pl.BlockSpec((D_MODEL, ADA_TN), lambda j: (0, j)),
                  pl.BlockSpec((1, ADA_TN), lambda j: (0, j))],
        out_specs=pl.BlockSpec((8, ADA_TN), lambda j: (0, j)),
        out_shape=jax.ShapeDtypeStruct((8, n), F32),
        compiler_params=_cparams(("arbitrary",)),
        name="ada",
    )(c_pad, w, b)


def _proj_kernel(x_ref, sc_ref, sh_ref, w_ref, o_ref, h_ref):
    @pl.when(pl.program_id(1) == 0)
    def _():
        h = _layer_norm(x_ref[...]) * (1.0 + sc_ref[...]) + sh_ref[...]
        h_ref[...] = h.astype(BF16)

    o_ref[...] = jnp.dot(h_ref[...], w_ref[...], preferred_element_type=F32)


def _proj(x2, sc, sh, w_bf):
    tiles_per_batch = SEQ // PROJ_TM
    return pl.pallas_call(
        _proj_kernel,
        grid=(N_TOK // PROJ_TM, PROJ_WIDTH // PROJ_TN),
        in_specs=[pl.BlockSpec((PROJ_TM, D_MODEL), lambda i, j: (i, 0)),
                  pl.BlockSpec((None, 1, D_MODEL), lambda i, j: (i // tiles_per_batch, 0, 0)),
                  pl.BlockSpec((None, 1, D_MODEL), lambda i, j: (i // tiles_per_batch, 0, 0)),
                  pl.BlockSpec((D_MODEL, PROJ_TN), lambda i, j: (0, j))],
        out_specs=pl.BlockSpec((PROJ_TM, PROJ_TN), lambda i, j: (i, j)),
        out_shape=jax.ShapeDtypeStruct((N_TOK, PROJ_WIDTH), F32),
        scratch_shapes=[pltpu.VMEM((PROJ_TM, D_MODEL), BF16)],
        compiler_params=_cparams(("arbitrary", "arbitrary")),
        name="proj",
    )(x2, sc, sh, w_bf)


def _rope(x, cos, sin_lo, sin_hi):
    return (x * cos + pltpu.roll(x, 96, axis=1) * sin_lo + pltpu.roll(x, 32, axis=1) * sin_hi)


def _attn_kernel(q_ref, k_ref, v_ref, cq_ref, slq_ref, shq_ref, ck_ref, slk_ref, shk_ref,
                 qw_ref, kw_ref, aw_ref, o_ref, kr_ref, vb_ref):
    @pl.when(pl.program_id(2) == 0)
    def _():
        k = _rms(k_ref[...]) * kw_ref[...]
        kr_ref[...] = _rope(k, ck_ref[...], slk_ref[...], shk_ref[...]).astype(BF16)
        vb_ref[...] = v_ref[...].astype(BF16)

    scale = 1.0 / math.sqrt(HEAD_DIM)
    cq = cq_ref[...]
    slq = slq_ref[...]
    shq = shq_ref[...]
    for h in range(KV_GROUP):
        cols = slice(h * HEAD_DIM, (h + 1) * HEAD_DIM)
        q = _rms(q_ref[:, cols]) * qw_ref[...]
        q = _rope(q, cq, slq, shq) * scale
        s = lax.dot_general(q.astype(BF16), kr_ref[...], (((1,), (1,)), ((), ())),
                            preferred_element_type=F32)
        m = jnp.max(s, axis=-1, keepdims=True)
        p = jnp.exp(s - m)
        l = jnp.sum(p, axis=-1, keepdims=True)
        o = jnp.dot(p.astype(BF16), vb_ref[...], preferred_element_type=F32) / l
        o = _rms(o) * aw_ref[:, cols]
        o_ref[:, cols] = o.astype(BF16)


def _attention(proj, cos, sin_lo, sin_hi, qw, kw, aw):
    nq = SEQ // ATTN_TQ
    gw = KV_GROUP * HEAD_DIM
    tab_q = pl.BlockSpec((ATTN_TQ, HEAD_DIM), lambda b, g, i: (i, 0))
    tab_k = pl.BlockSpec((SEQ, HEAD_DIM), lambda b, g, i: (0, 0))
    return pl.pallas_call(
        _attn_kernel,
        grid=(BATCH, N_KV_HEADS, nq),
        in_specs=[pl.BlockSpec((ATTN_TQ, gw), lambda b, g, i: (b * nq + i, g)),
                  pl.BlockSpec((SEQ, HEAD_DIM), lambda b, g, i: (b, COL_K + g)),
                  pl.BlockSpec((SEQ, HEAD_DIM), lambda b, g, i: (b, COL_V + g)),
                  tab_q, tab_q, tab_q, tab_k, tab_k, tab_k,
                  pl.BlockSpec((1, HEAD_DIM), lambda b, g, i: (0, 0)),
                  pl.BlockSpec((1, HEAD_DIM), lambda b, g, i: (0, 0)),
                  pl.BlockSpec((1, gw), lambda b, g, i: (0, g))],
        out_specs=pl.BlockSpec((ATTN_TQ, gw), lambda b, g, i: (b * nq + i, g)),
        out_shape=jax.ShapeDtypeStruct((N_TOK, ATTN_WIDTH), BF16),
        scratch_shapes=[pltpu.VMEM((SEQ, HEAD_DIM), BF16), pltpu.VMEM((SEQ, HEAD_DIM), BF16)],
        compiler_params=_cparams(("arbitrary", "arbitrary", "arbitrary")),
        name="attn",
    )(proj, proj, proj, cos, sin_lo, sin_hi, cos, sin_lo, sin_hi, qw, kw, aw)


def _hgrn_kernel(qr_ref, ff_ref, fb_ref, iv_ref, go_ref, lb_ref, nw_ref, o_ref, acc_ref, st_ref):
    C = HGRN_CHUNK
    nc = SEQ // C
    trips = nc // HGRN_UN
    row = lax.broadcasted_iota(jnp.int32, (C, C), 0)
    col = lax.broadcasted_iota(jnp.int32, (C, C), 1)
    keeps = (row >= col, row <= col)
    lasts = (C - 1, 0)
    f_refs = (ff_ref, fb_ref)

    nt_dims = (((1,), (1,)), ((), ()))
    tn_dims = (((0,), (0,)), ((), ()))
    st_ref[...] = jnp.zeros_like(st_ref)

    def trip(it, finish):
        chains = []
        for h in range(HGRN_HB):
            cols = slice(h * HEAD_DIM, (h + 1) * HEAD_DIM)
            for d in range(2):
                for u in range(HGRN_UN):
                    n = it * HGRN_UN + u
                    cidx = n if d == 0 else nc - 1 - n
                    chains.append(dict(h=h, d=d, cols=cols, rows=pl.ds(pl.multiple_of(cidx * C, C), C)))

        for ch in chains:
            d = ch["d"]
            lb = lb_ref[d, :, ch["cols"]]
            fg = lb + (1.0 - lb) * _sigmoid(f_refs[d][ch["rows"], ch["cols"]])
            ch["kk"] = 1.0 - fg
            lf = jnp.log(fg)
            lf_hi = lf.astype(BF16)
            lf_lo = (lf - lf_hi.astype(F32)).astype(BF16)
            tri = jnp.where(keeps[d], 1.0, 0.0).astype(BF16)
            ch["b"] = (jnp.dot(tri, lf_hi, preferred_element_type=F32)
                       + jnp.dot(tri, lf_lo, preferred_element_type=F32))
        for ch in chains:
            b = ch["b"]
            bl = b[lasts[ch["d"]]:lasts[ch["d"]] + 1, :]
            qx = qr_ref[ch["rows"], ch["cols"]]
            ch["qd"] = (qx * _sigmoid(qx) * jnp.exp(b)).astype(BF16)
            kd = (ch["kk"] * jnp.exp(-b)).astype(BF16)
            ku = (ch["kk"] * jnp.exp(bl - b)).astype(BF16)
            ch["v"] = iv_ref[ch["rows"], ch["cols"]].astype(BF16)
            ch["decay"] = jnp.exp(bl)
            ch["sc"] = lax.dot_general(ch["qd"], kd, nt_dims, preferred_element_type=F32)
            ch["u_t"] = lax.dot_general(ch["v"], ku, tn_dims, preferred_element_type=F32)
        for h in range(HGRN_HB):
            for d in range(2):
                state = st_ref[2 * h + d]
                for ch in chains:
                    if ch["h"] == h and ch["d"] == d:
                        ch["state"] = state.astype(BF16)
                        state = state * ch["decay"] + ch["u_t"]
                st_ref[2 * h + d] = state
        for ch in chains:
            sc = jnp.where(keeps[ch["d"]], ch["sc"], 0.0).astype(BF16)
            ch["o"] = (jnp.dot(sc, ch["v"], preferred_element_type=F32)
                       + lax.dot_general(ch["qd"], ch["state"], nt_dims, preferred_element_type=F32))
        for ch in chains:
            rows, cols = ch["rows"], ch["cols"]
            if finish:
                o = _rms(acc_ref[rows, cols] + ch["o"]) * nw_ref[:, cols]
                g = go_ref[rows, cols]
                o_ref[rows, cols] = (o * (g * _sigmoid(g))).astype(BF16)
            else:
                acc_ref[rows, cols] = ch["o"]

    def first_half(it, carry):
        trip(it, False)
        return carry

    def second_half(it, carry):
        trip(it, True)
        return carry

    lax.fori_loop(0, trips // 2, first_half, 0)
    lax.fori_loop(trips // 2, trips, second_half, 0)


def _hgrn(proj, lb, nw):
    width = HGRN_HB * HEAD_DIM

    def col(c0):
        return pl.BlockSpec((SEQ, width), lambda b, h: (b, c0 // HGRN_HB + h))

    return pl.pallas_call(
        _hgrn_kernel,
        grid=(BATCH, N_HGRN_HEADS // HGRN_HB),
        in_specs=[col(COL_QR), col(COL_FF), col(COL_FB), col(COL_IN), col(COL_GO),
                  pl.BlockSpec((2, 1, width), lambda b, h: (0, 0, h)),
                  pl.BlockSpec((1, width), lambda b, h: (0, h))],
        out_specs=pl.BlockSpec((SEQ, width), lambda b, h: (b, h)),
        out_shape=jax.ShapeDtypeStruct((N_TOK, HGRN_WIDTH), BF16),
        scratch_shapes=[pltpu.VMEM((SEQ, width), F32),
                        pltpu.VMEM((2 * HGRN_HB, HEAD_DIM, HEAD_DIM), F32)],
        compiler_params=_cparams(("arbitrary", "arbitrary")),
        name="hgrn",
    )(proj, proj, proj, proj, proj, lb, nw)


def _mix_kernel(oa_ref, or_ref, wa_ref, wr_ref, x_ref, g1_ref, sc_ref, sh_ref, lg_ref, lbias_ref,
                wrt_ref, brt_ref, x1_ref, h2_ref, idx_ref, gate_ref, rank_ref, cnt_ref, carry_ref):
    i = pl.program_id(0)

    @pl.when(i == 0)
    def _():
        carry_ref[...] = jnp.zeros_like(carry_ref)

    y = jnp.dot(oa_ref[...], wa_ref[...], preferred_element_type=F32)
    y = y + jnp.dot(or_ref[...], wr_ref[...], preferred_element_type=F32)
    x1 = _layer_norm(DEEPNORM_ALPHA * x_ref[...] + g1_ref[...] * y) * lg_ref[...] + lbias_ref[...]
    x1_ref[...] = x1
    h2 = _layer_norm(x1) * (1.0 + sc_ref[...]) + sh_ref[...]
    h2_ref[...] = h2
    logits = jnp.dot(h2.astype(BF16), wrt_ref[...], preferred_element_type=F32) + brt_ref[...]

    tm = logits.shape[0]
    lane = lax.broadcasted_iota(jnp.int32, (tm, LANES), 1)
    neg = jnp.float32(-jnp.inf)
    work = logits
    vals, sels = [], []
    for _ in range(TOP_K):
        m = jnp.max(work, axis=-1, keepdims=True)
        sel = jnp.min(jnp.where(work == m, lane, LANES), axis=-1, keepdims=True)
        vals.append(m)
        sels.append(sel)
        work = jnp.where(lane == sel, neg, work)
    es = [jnp.exp(v - vals[0]) for v in vals]
    denom = es[0] + es[1] + es[2] + es[3]

    multi = jnp.zeros((tm, LANES), F32)
    for sel in sels:
        multi = multi + jnp.where(lane == sel, 1.0, 0.0)
    r = lax.broadcasted_iota(jnp.int32, (tm, tm), 0)
    c = lax.broadcasted_iota(jnp.int32, (tm, tm), 1)
    strict = jnp.where(r > c, 1.0, 0.0).astype(BF16)
    before = jnp.dot(strict, multi.astype(BF16), preferred_element_type=F32) + carry_ref[...]
    carry_ref[...] = carry_ref[...] + jnp.sum(multi, axis=0, keepdims=True)
    cnt_ref[...] = carry_ref[...]

    idx_out = jnp.zeros((tm, LANES), jnp.int32)
    gate_out = jnp.zeros((tm, LANES), F32)
    rank_out = jnp.zeros((tm, LANES), F32)
    for k in range(TOP_K):
        rk = jnp.sum(jnp.where(lane == sels[k], before, 0.0), axis=-1, keepdims=True)
        idx_out = jnp.where(lane == k, sels[k], idx_out)
        gate_out = jnp.where(lane == k, es[k] / denom, gate_out)
        rank_out = jnp.where(lane == k, rk, rank_out)
    idx_ref[...] = idx_out
    gate_ref[...] = gate_out
    rank_ref[...] = rank_out.astype(jnp.int32)


def _mix(o_attn, o_r, wa, wr, x2, g1, sc2, sh2, ln_g, ln_b, w_rt, b_rt):
    tiles_per_batch = SEQ // MIX_TM
    rows = lambda w: pl.BlockSpec((MIX_TM, w), lambda i: (i, 0))
    full = lambda a, b: pl.BlockSpec((a, b), lambda i: (0, 0))
    per_batch = pl.BlockSpec((None, 1, D_MODEL), lambda i: (i // tiles_per_batch, 0, 0))
    return pl.pallas_call(
        _mix_kernel,
        grid=(N_TOK // MIX_TM,),
        in_specs=[rows(ATTN_WIDTH), rows(HGRN_WIDTH), full(ATTN_WIDTH, D_MODEL), full(HGRN_WIDTH, D_MODEL),
                  rows(D_MODEL), per_batch, per_batch, per_batch, full(1, D_MODEL), full(1, D_MODEL),
                  full(D_MODEL, LANES), full(1, LANES)],
        out_specs=[rows(D_MODEL), rows(D_MODEL), rows(LANES), rows(LANES), rows(LANES), full(1, LANES)],
        out_shape=[jax.ShapeDtypeStruct((N_TOK, D_MODEL), F32),
                   jax.ShapeDtypeStruct((N_TOK, D_MODEL), F32),
                   jax.ShapeDtypeStruct((N_TOK, LANES), jnp.int32),
                   jax.ShapeDtypeStruct((N_TOK, LANES), F32),
                   jax.ShapeDtypeStruct((N_TOK, LANES), jnp.int32),
                   jax.ShapeDtypeStruct((1, LANES), F32)],
        scratch_shapes=[pltpu.VMEM((1, LANES), F32)],
        compiler_params=_cparams(("arbitrary",)),
        name="mix",
    )(o_attn, o_r, wa, wr, x2, g1, sc2, sh2, ln_g, ln_b, w_rt, b_rt)


def _wait_rows(n, make_copy):
    for bit in range(MOE_SB.bit_length()):
        @pl.when(((n >> bit) & 1) == 1)
        def _(bit=bit):
            make_copy(1 << bit).wait()


def _for_rows(lo, hi, fn):
    groups = (hi - lo) // MOE_ISSUE_UNROLL

    def group(q, carry):
        base = lo + q * MOE_ISSUE_UNROLL
        for u in range(MOE_ISSUE_UNROLL):
            fn(base + u)
        return carry

    def single(r, carry):
        fn(r)
        return carry

    lax.fori_loop(0, groups, group, 0)
    lax.fori_loop(lo + groups * MOE_ISSUE_UNROLL, hi, single, 0)


def _moe_kernel(se_ref, so_ref, sn_ref, src_ref, dst_ref, h2_hbm, w1g_ref, w1l_ref, b1g_ref, b1l_ref, w2_ref,
                b2_ref, y_hbm, xg_ref, xb_ref, acc_ref, gsem, ssem):
    g = pl.program_id(0)
    j = pl.program_id(1)
    nj = pl.num_programs(1)
    n = sn_ref[g]
    nt = (n + MOE_TMI - 1) // MOE_TMI
    slot = g % 2
    chunk = MOE_SB // (D_EXPERT // MOE_TH)

    def gather_copy(tok, r):
        return pltpu.make_async_copy(h2_hbm.at[pl.ds(tok, 1)], xg_ref.at[pl.ds(r, 1)], gsem)

    def scatter_copy(s, r, a):
        return pltpu.make_async_copy(acc_ref.at[s, pl.ds(r, 1)], y_hbm.at[pl.ds(a, 1)], ssem.at[s])

    def issue_gather(off, lo, hi):
        _for_rows(lo, hi, lambda r: gather_copy(src_ref[off + r], r).start())

    def tile_rows(i):
        return pl.ds(pl.multiple_of(i * MOE_TMI, MOE_TMI), MOE_TMI)

    @pl.when(j == 0)
    def _():
        @pl.when(g == 0)
        def _():
            xg_ref[...] = jnp.zeros_like(xg_ref)
            issue_gather(so_ref[0], 0, sn_ref[0])

        _wait_rows(n, lambda k: pltpu.make_async_copy(h2_hbm.at[pl.ds(0, k)], xg_ref.at[pl.ds(0, k)], gsem))

        @pl.when(g >= 2)
        def _():
            _wait_rows(sn_ref[jnp.maximum(g - 2, 0)],
                       lambda k: pltpu.make_async_copy(acc_ref.at[slot, pl.ds(0, k)], y_hbm.at[pl.ds(0, k)],
                                                       ssem.at[slot]))

        def prep(i, carry):
            rows = tile_rows(i)
            xb_ref[rows, :] = xg_ref[rows, :].astype(BF16)
            acc_ref[slot, rows, :] = jnp.broadcast_to(b2_ref[...], (MOE_TMI, D_MODEL))
            return carry

        lax.fori_loop(0, nt, prep, 0)

    @pl.when(nt > 0)
    def _():
        w1g = w1g_ref[...].astype(BF16)
        w1l = w1l_ref[...].astype(BF16)
        w2 = w2_ref[...].astype(BF16)

        def body(i, carry):
            rows = tile_rows(i)
            xs = xb_ref[rows, :]
            hg = jnp.dot(xs, w1g, preferred_element_type=F32) + b1g_ref[...]
            hl = jnp.dot(xs, w1l, preferred_element_type=F32) + b1l_ref[...]
            hg = jnp.minimum(hg, SWIGLU_LIMIT)
            hl = jnp.clip(hl, -SWIGLU_LIMIT, SWIGLU_LIMIT)
            act = hg * _sigmoid(SWIGLU_ALPHA * hg) * (hl + 1.0)
            acc_ref[slot, rows, :] += jnp.dot(act.astype(BF16), w2, preferred_element_type=F32)
            return carry

        lax.fori_loop(0, nt, body, 0)

    g_next = jnp.minimum(g + 1, MOE_G - 1)
    n_next = jnp.where(g + 1 < MOE_G, sn_ref[g_next], 0)
    issue_gather(so_ref[g_next], jnp.minimum(j * chunk, n_next), jnp.minimum((j + 1) * chunk, n_next))

    g_prev = jnp.maximum(g - 1, 0)
    n_prev = jnp.where(g >= 1, sn_ref[g_prev], 0)
    off_prev = so_ref[g_prev]

    _for_rows(jnp.minimum(j * chunk, n_prev), jnp.minimum((j + 1) * chunk, n_prev),
              lambda r: scatter_copy(1 - slot, r, dst_ref[off_prev + r]).start())

    @pl.when((g == MOE_G - 1) & (j == nj - 1))
    def _():
        _wait_rows(n_prev, lambda k: pltpu.make_async_copy(acc_ref.at[1 - slot, pl.ds(0, k)],
                                                            y_hbm.at[pl.ds(0, k)], ssem.at[1 - slot]))


def _moe(sb_expert, sb_off, sb_n, src_tok, dst_row, h2, w1, b1, w2, b2):
    nj = D_EXPERT // MOE_TH

    def hidden(j, sn, g):
        return jnp.where(sn[g] > 0, j, nj - 1)

    grid_spec = pltpu.PrefetchScalarGridSpec(
        num_scalar_prefetch=5,
        grid=(MOE_G, nj),
        in_specs=[
            pl.BlockSpec(memory_space=pl.ANY),
            pl.BlockSpec((None, D_MODEL, MOE_TH), lambda g, j, se, so, sn, *_: (se[g], 0, hidden(j, sn, g))),
            pl.BlockSpec((None, D_MODEL, MOE_TH), lambda g, j, se, so, sn, *_: (se[g], 0, nj + hidden(j, sn, g))),
            pl.BlockSpec((None, 1, MOE_TH), lambda g, j, se, so, sn, *_: (se[g], 0, hidden(j, sn, g))),
            pl.BlockSpec((None, 1, MOE_TH), lambda g, j, se, so, sn, *_: (se[g], 0, nj + hidden(j, sn, g))),
            pl.BlockSpec((None, MOE_TH, D_MODEL), lambda g, j, se, so, sn, *_: (se[g], hidden(j, sn, g), 0)),
            pl.BlockSpec((None, 1, D_MODEL), lambda g, j, se, so, sn, *_: (se[g], 0, 0)),
        ],
        out_specs=pl.BlockSpec(memory_space=pl.ANY),
        scratch_shapes=[pltpu.VMEM((MOE_SB, D_MODEL), F32),
                        pltpu.VMEM((MOE_SB, D_MODEL), BF16),
                        pltpu.VMEM((2, MOE_SB, D_MODEL), F32),
                        pltpu.SemaphoreType.DMA(()),
                        pltpu.SemaphoreType.DMA((2,))],
    )
    return pl.pallas_call(
        _moe_kernel,
        grid_spec=grid_spec,
        out_shape=jax.ShapeDtypeStruct((N_TOK * TOP_K, D_MODEL), F32),
        compiler_params=_cparams(("arbitrary", "arbitrary")),
        name="moe",
    )(sb_expert, sb_off, sb_n, src_tok, dst_row, h2, w1, w1, b1, b1, w2, b2)


def _final_kernel(x1_ref, y0_ref, y1_ref, y2_ref, y3_ref, gate_ref, g2_ref, lg_ref, lb_ref, o_ref):
    y = gate_ref[:, 0:1] * y0_ref[...]
    for k, y_ref in enumerate((y1_ref, y2_ref, y3_ref), start=1):
        y = y + gate_ref[:, k:k + 1] * y_ref[...]
    z = DEEPNORM_ALPHA * x1_ref[...] + g2_ref[...] * y
    o_ref[...] = _layer_norm(z) * lg_ref[...] + lb_ref[...]


def _final(x1, y4, gates, g2, ln_g, ln_b):
    tiles_per_batch = SEQ // FIN_TM
    tiles = N_TOK // FIN_TM
    rows = lambda w: pl.BlockSpec((FIN_TM, w), lambda i: (i, 0))
    plane = lambda k: pl.BlockSpec((FIN_TM, D_MODEL), lambda i: (k * tiles + i, 0))
    vec = pl.BlockSpec((1, D_MODEL), lambda i: (0, 0))
    return pl.pallas_call(
        _final_kernel,
        grid=(tiles,),
        in_specs=[rows(D_MODEL), plane(0), plane(1), plane(2), plane(3), rows(LANES),
                  pl.BlockSpec((None, 1, D_MODEL), lambda i: (i // tiles_per_batch, 0, 0)), vec, vec],
        out_specs=rows(D_MODEL),
        out_shape=jax.ShapeDtypeStruct((N_TOK, D_MODEL), F32),
        compiler_params=_cparams(("arbitrary",)),
        name="final",
    )(x1, y4, y4, y4, y4, gates, g2, ln_g, ln_b)


def _rope_tables():
    rows = SEQ // GRID_W
    t = np.arange(SEQ)
    row = (t // GRID_W - rows // 2).astype(np.float32)
    col = (t % GRID_W - GRID_W // 2).astype(np.float32)
    inv_freq = jnp.asarray(ROPE_THETA, F32) ** (-jnp.arange(0, ROPE_AXIS_DIM, 2, dtype=F32) / ROPE_AXIS_DIM)
    ang_row = jnp.asarray(row)[:, None] * inv_freq[None, :]
    ang_col = jnp.asarray(col)[:, None] * inv_freq[None, :]
    zeros = jnp.zeros_like(ang_row)
    cos = jnp.concatenate([jnp.cos(ang_row)] * 2 + [jnp.cos(ang_col)] * 2, axis=-1)
    sin_lo = jnp.concatenate([-jnp.sin(ang_row), zeros, -jnp.sin(ang_col), zeros], axis=-1)
    sin_hi = jnp.concatenate([zeros, jnp.sin(ang_row), zeros, jnp.sin(ang_col)], axis=-1)
    return cos, sin_lo, sin_hi


def _routing(top_i, rank, counts):
    counts = counts.astype(jnp.int32)
    start = jnp.cumsum(counts) - counts
    assign = jnp.arange(N_TOK * TOP_K, dtype=jnp.int32)
    dest = (start[top_i] + rank).reshape(-1)
    sorted_assign = jnp.zeros((N_TOK * TOP_K,), jnp.int32).at[dest].set(assign)
    nsb = (counts + MOE_SB - 1) // MOE_SB
    sb_end = jnp.cumsum(nsb)
    sb_start = sb_end - nsb
    g = jnp.arange(MOE_G, dtype=jnp.int32)
    active = g < sb_end[-1]
    e_of_g = jnp.minimum(jnp.sum(g[:, None] >= sb_end[None, :], axis=1), N_EXPERTS - 1).astype(jnp.int32)
    last_e = e_of_g[jnp.maximum(sb_end[-1] - 1, 0)]
    first_row = (g - sb_start[e_of_g]) * MOE_SB
    sb_n = jnp.where(active, jnp.clip(counts[e_of_g] - first_row, 0, MOE_SB), 0).astype(jnp.int32)
    sb_off = jnp.where(active, start[e_of_g] + first_row, 0).astype(jnp.int32)
    sb_expert = jnp.where(active, e_of_g, last_e).astype(jnp.int32)
    src_tok = sorted_assign // TOP_K
    dst_row = (sorted_assign % TOP_K) * N_TOK + src_tok
    return sb_expert, sb_off, sb_n, src_tok, dst_row


def kernel(x, c, w_ada, b_ada, w_in, q_norm_w, k_norm_w, attn_norm_w, hgrn_lb, hgrn_norm_w, w_out, ln1_g, ln1_b, w_router, b_router, w_exp_in, b_exp_in, w_exp_out, b_exp_out, ln2_g, ln2_b):
    c_pad = jnp.zeros((8, D_MODEL), F32).at[:BATCH].set(c)
    cos, sin_lo, sin_hi = _rope_tables()
    x2 = x.reshape(N_TOK, D_MODEL)
    for l in range(DEPTH):
        mod = _ada(c_pad, w_ada[l], b_ada[l][None, :])[:BATCH]
        sh1, sc1, g1, sh2, sc2, g2 = [m.reshape(BATCH, 1, D_MODEL) for m in jnp.split(mod, 6, axis=-1)]

        proj = _proj(x2, sc1, sh1, w_in[l].astype(BF16))
        o_attn = _attention(proj, cos, sin_lo, sin_hi, q_norm_w[l][None, :], k_norm_w[l][None, :],
                            attn_norm_w[l][None, :])
        lb = jnp.cumsum(jax.nn.softmax(hgrn_lb.astype(F32), axis=1), axis=1)[:, l]
        o_r = _hgrn(proj, lb.reshape(2, 1, HGRN_WIDTH), hgrn_norm_w[l][None, :])

        w_o = w_out[l].astype(BF16)
        w_rt = jnp.zeros((D_MODEL, LANES), BF16).at[:, :N_EXPERTS].set(w_router[l].astype(BF16))
        b_rt = jnp.full((1, LANES), -1e30, F32).at[0, :N_EXPERTS].set(b_router[l])
        x1, h2, idx, gates, rank, counts = _mix(
            o_attn, o_r, w_o[:ATTN_WIDTH], w_o[ATTN_WIDTH:], x2, g1, sc2, sh2,
            ln1_g[l][None, :], ln1_b[l][None, :], w_rt, b_rt)
        sb_expert, sb_off, sb_n, src_tok, dst_row = _routing(idx[:, :TOP_K], rank[:, :TOP_K],
                                                             counts[0, :N_EXPERTS])
        y4 = _moe(sb_expert, sb_off, sb_n, src_tok, dst_row, h2, w_exp_in[l], b_exp_in[l][:, None, :],
                  w_exp_out[l], b_exp_out[l][:, None, :])
        x2 = _final(x1, y4, gates, g2, ln2_g[l][None, :], ln2_b[l][None, :])
    return x2.reshape(BATCH, SEQ, D_MODEL)
```

```python
import functools
import math

import numpy as np
import jax
import jax.numpy as jnp
from jax import lax
from jax.experimental import pallas as pl
from jax.experimental.pallas import tpu as pltpu

F32 = jnp.float32
BF16 = jnp.bfloat16

D_MODEL = 2048
BATCH = 4
SEQ = 2048
DEPTH = 1
N_TOK = BATCH * SEQ
HEAD_DIM = 128
ATTN_WIDTH = 1024
N_Q_HEADS = 8
N_KV_HEADS = 2
KV_GROUP = 4
HGRN_WIDTH = 1024
N_HGRN_HEADS = 8
HGRN_CHUNK = 64
GRID_W = 64
ROPE_THETA = 10000.0
ROPE_AXIS_DIM = 64
N_EXPERTS = 32
TOP_K = 4
D_EXPERT = 2048
SWIGLU_LIMIT = 7.0
SWIGLU_ALPHA = 1.702
NORM_EPS = 1e-6
DEEPNORM_ALPHA = (2 * DEPTH) ** 0.25
PROJ_WIDTH = 6656
LANES = 128

COL_Q = 0
COL_K = 8
COL_V = 10
COL_QR = 12
COL_FF = 20
COL_FB = 28
COL_IN = 36
COL_GO = 44

VMEM_LIMIT = 56 * 1024 * 1024

ADA_TN = 1024
PROJ_TM = 512
PROJ_TN = 1664
ATTN_TQ = 256
HGRN_HB = 2
HGRN_UN = 2
MIX_TM = 256
MOE_ISSUE_UNROLL = 8
MOE_SB = 1024
MOE_TMI = 256
MOE_TH = 256
MOE_G = N_TOK * TOP_K // MOE_SB + N_EXPERTS + 1
MOE_Q = MOE_SB // ((MOE_SB // MOE_TMI) * (D_EXPERT // MOE_TH))
FIN_TM = 256


def _cparams(sem):
    return pltpu.CompilerParams(dimension_semantics=sem, vmem_limit_bytes=VMEM_LIMIT)


def _sigmoid(x):
    return 1.0 / (1.0 + jnp.exp(-x))


def _layer_norm(x):
    mu = jnp.mean(x, axis=-1, keepdims=True)
    xc = x - mu
    var = jnp.mean(xc * xc, axis=-1, keepdims=True)
    return xc * lax.rsqrt(var + NORM_EPS)


def _rms(x):
    return x * lax.rsqrt(jnp.mean(x * x, axis=-1, keepdims=True) + NORM_EPS)


def _ada_kernel(c_ref, w_ref, b_ref, o_ref):
    c = c_ref[...]
    ca = c * _sigmoid(c)
    o_ref[...] = jnp.dot(ca.astype(BF16), w_ref[...].astype(BF16),
                         preferred_element_type=F32) + b_ref[...]


def _ada(c_pad, w, b):
    n = w.shape[1]
    return pl.pallas_call(
        _ada_kernel,
        grid=(n // ADA_TN,),
        in_specs=[pl.BlockSpec((8, D_MODEL), lambda j: (0, 0)),
                  pl.BlockSpec((D_MODEL, ADA_TN), lambda j: (0, j)),
                  pl.BlockSpec((1, ADA_TN), lambda j: (0, j))],
        out_specs=pl.BlockSpec((8, ADA_TN), lambda j: (0, j)),
        out_shape=jax.ShapeDtypeStruct((8, n), F32),
        compiler_params=_cparams(("arbitrary",)),
        name="ada",
    )(c_pad, w, b)


def _proj_kernel(x_ref, sc_ref, sh_ref, w_ref, o_ref, h_ref):
    @pl.when(pl.program_id(1) == 0)
    def _():
        h = _layer_norm(x_ref[...]) * (1.0 + sc_ref[...]) + sh_ref[...]
        h_ref[...] = h.astype(BF16)

    o_ref[...] = jnp.dot(h_ref[...], w_ref[...], preferred_element_type=F32)


def _proj(x2, sc, sh, w_bf):
    tiles_per_batch = SEQ // PROJ_TM
    return pl.pallas_call(
        _proj_kernel,
        grid=(N_TOK // PROJ_TM, PROJ_WIDTH // PROJ_TN),
        in_specs=[pl.BlockSpec((PROJ_TM, D_MODEL), lambda i, j: (i, 0)),
                  pl.BlockSpec((None, 1, D_MODEL), lambda i, j: (i // tiles_per_batch, 0, 0)),
                  pl.BlockSpec((None, 1, D_MODEL), lambda i, j: (i // tiles_per_batch, 0, 0)),
                  pl.BlockSpec((D_MODEL, PROJ_TN), lambda i, j: (0, j))],
        out_specs=pl.BlockSpec((PROJ_TM, PROJ_TN), lambda i, j: (i, j)),
        out_shape=jax.ShapeDtypeStruct((N_TOK, PROJ_WIDTH), F32),
        scratch_shapes=[pltpu.VMEM((PROJ_TM, D_MODEL), BF16)],
        compiler_params=_cparams(("arbitrary", "arbitrary")),
        name="proj",
    )(x2, sc, sh, w_bf)


def _rope(x, cos, sin_lo, sin_hi):
    return (x * cos + pltpu.roll(x, 96, axis=1) * sin_lo + pltpu.roll(x, 32, axis=1) * sin_hi)


def _attn_kernel(q_ref, k_ref, v_ref, cq_ref, slq_ref, shq_ref, ck_ref, slk_ref, shk_ref,
                 qw_ref, kw_ref, aw_ref, o_ref, kr_ref, vb_ref):
    @pl.when(pl.program_id(2) == 0)
    def _():
        k = _rms(k_ref[...]) * kw_ref[...]
        kr_ref[...] = _rope(k, ck_ref[...], slk_ref[...], shk_ref[...]).astype(BF16)
        vb_ref[...] = v_ref[...].astype(BF16)

    scale = 1.0 / math.sqrt(HEAD_DIM)
    cq = cq_ref[...]
    slq = slq_ref[...]
    shq = shq_ref[...]
    for h in range(KV_GROUP):
        cols = slice(h * HEAD_DIM, (h + 1) * HEAD_DIM)
        q = _rms(q_ref[:, cols]) * qw_ref[...]
        q = _rope(q, cq, slq, shq) * scale
        s = lax.dot_general(q.astype(BF16), kr_ref[...], (((1,), (1,)), ((), ())),
                            preferred_element_type=F32)
        m = jnp.max(s, axis=-1, keepdims=True)
        p = jnp.exp(s - m)
        l = jnp.sum(p, axis=-1, keepdims=True)
        o = jnp.dot(p.astype(BF16), vb_ref[...], preferred_element_type=F32) / l
        o = _rms(o) * aw_ref[:, cols]
        o_ref[:, cols] = o.astype(BF16)


def _attention(proj, cos, sin_lo, sin_hi, qw, kw, aw):
    nq = SEQ // ATTN_TQ
    gw = KV_GROUP * HEAD_DIM
    tab_q = pl.BlockSpec((ATTN_TQ, HEAD_DIM), lambda b, g, i: (i, 0))
    tab_k = pl.BlockSpec((SEQ, HEAD_DIM), lambda b, g, i: (0, 0))
    return pl.pallas_call(
        _attn_kernel,
        grid=(BATCH, N_KV_HEADS, nq),
        in_specs=[pl.BlockSpec((ATTN_TQ, gw), lambda b, g, i: (b * nq + i, g)),
                  pl.BlockSpec((SEQ, HEAD_DIM), lambda b, g, i: (b, COL_K + g)),
                  pl.BlockSpec((SEQ, HEAD_DIM), lambda b, g, i: (b, COL_V + g)),
                  tab_q, tab_q, tab_q, tab_k, tab_k, tab_k,
                  pl.BlockSpec((1, HEAD_DIM), lambda b, g, i: (0, 0)),
                  pl.BlockSpec((1, HEAD_DIM), lambda b, g, i: (0, 0)),
                  pl.BlockSpec((1, gw), lambda b, g, i: (0, g))],
        out_specs=pl.BlockSpec((ATTN_TQ, gw), lambda b, g, i: (b * nq + i, g)),
        out_shape=jax.ShapeDtypeStruct((N_TOK, ATTN_WIDTH), BF16),
        scratch_shapes=[pltpu.VMEM((SEQ, HEAD_DIM), BF16), pltpu.VMEM((SEQ, HEAD_DIM), BF16)],
        compiler_params=_cparams(("arbitrary", "arbitrary", "arbitrary")),
        name="attn",
    )(proj, proj, proj, cos, sin_lo, sin_hi, cos, sin_lo, sin_hi, qw, kw, aw)


def _hgrn_kernel(qr_ref, ff_ref, fb_ref, iv_ref, go_ref, lb_ref, nw_ref, o_ref, acc_ref, st_ref):
    C = HGRN_CHUNK
    nc = SEQ // C
    trips = nc // HGRN_UN
    row = lax.broadcasted_iota(jnp.int32, (C, C), 0)
    col = lax.broadcasted_iota(jnp.int32, (C, C), 1)
    keeps = (row >= col, row <= col)
    lasts = (C - 1, 0)
    f_refs = (ff_ref, fb_ref)

    nt_dims = (((1,), (1,)), ((), ()))
    tn_dims = (((0,), (0,)), ((), ()))
    st_ref[...] = jnp.zeros_like(st_ref)

    def trip(it, finish):
        chains = []
        for h in range(HGRN_HB):
            cols = slice(h * HEAD_DIM, (h + 1) * HEAD_DIM)
            for d in range(2):
                for u in range(HGRN_UN):
                    n = it * HGRN_UN + u
                    cidx = n if d == 0 else nc - 1 - n
                    chains.append(dict(h=h, d=d, cols=cols, rows=pl.ds(pl.multiple_of(cidx * C, C), C)))

        for ch in chains:
            d = ch["d"]
            lb = lb_ref[d, :, ch["cols"]]
            fg = lb + (1.0 - lb) * _sigmoid(f_refs[d][ch["rows"], ch["cols"]])
            ch["kk"] = 1.0 - fg
            lf = jnp.log(fg)
            lf_hi = lf.astype(BF16)
            lf_lo = (lf - lf_hi.astype(F32)).astype(BF16)
            tri = jnp.where(keeps[d], 1.0, 0.0).astype(BF16)
            ch["b"] = (jnp.dot(tri, lf_hi, preferred_element_type=F32)
                       + jnp.dot(tri, lf_lo, preferred_element_type=F32))
        for ch in chains:
            b = ch["b"]
            bl = b[lasts[ch["d"]]:lasts[ch["d"]] + 1, :]
            qx = qr_ref[ch["rows"], ch["cols"]]
            ch["qd"] = (qx * _sigmoid(qx) * jnp.exp(b)).astype(BF16)
            kd = (ch["kk"] * jnp.exp(-b)).astype(BF16)
            ku = (ch["kk"] * jnp.exp(bl - b)).astype(BF16)
            ch["v"] = iv_ref[ch["rows"], ch["cols"]].astype(BF16)
            ch["decay"] = jnp.exp(bl)
            ch["sc"] = lax.dot_general(ch["qd"], kd, nt_dims, preferred_element_type=F32)
            ch["u_t"] = lax.dot_general(ch["v"], ku, tn_dims, preferred_element_type=F32)
        for h in range(HGRN_HB):
            for d in range(2):
                state = st_ref[2 * h + d]
                for ch in chains:
                    if ch["h"] == h and ch["d"] == d:
                        ch["state"] = state.astype(BF16)
                        state = state * ch["decay"] + ch["u_t"]
                st_ref[2 * h + d] = state
        for ch in chains:
            sc = jnp.where(keeps[ch["d"]], ch["sc"], 0.0).astype(BF16)
            ch["o"] = (jnp.dot(sc, ch["v"], preferred_element_type=F32)
                       + lax.dot_general(ch["qd"], ch["state"], nt_dims, preferred_element_type=F32))
        for ch in chains:
            rows, cols = ch["rows"], ch["cols"]
            if finish:
                o = _rms(acc_ref[rows, cols] + ch["o"]) * nw_ref[:, cols]
                g = go_ref[rows, cols]
                o_ref[rows, cols] = (o * (g * _sigmoid(g))).astype(BF16)
            else:
                acc_ref[rows, cols] = ch["o"]

    def first_half(it, carry):
        trip(it, False)
        return carry

    def second_half(it, carry):
        trip(it, True)
        return carry

    lax.fori_loop(0, trips // 2, first_half, 0)
    lax.fori_loop(trips // 2, trips, second_half, 0)


def _hgrn(proj, lb, nw):
    width = HGRN_HB * HEAD_DIM

    def col(c0):
        return pl.BlockSpec((SEQ, width), lambda b, h: (b, c0 // HGRN_HB + h))

    return pl.pallas_call(
        _hgrn_kernel,
        grid=(BATCH, N_HGRN_HEADS // HGRN_HB),
        in_specs=[col(COL_QR), col(COL_FF), col(COL_FB), col(COL_IN), col(COL_GO),
                  pl.BlockSpec((2, 1, width), lambda b, h: (0, 0, h)),
                  pl.BlockSpec((1, width), lambda b, h: (0, h))],
        out_specs=pl.BlockSpec((SEQ, width), lambda b, h: (b, h)),
        out_shape=jax.ShapeDtypeStruct((N_TOK, HGRN_WIDTH), BF16),
        scratch_shapes=[pltpu.VMEM((SEQ, width), F32),
                        pltpu.VMEM((2 * HGRN_HB, HEAD_DIM, HEAD_DIM), F32)],
        compiler_params=_cparams(("arbitrary", "arbitrary")),
        name="hgrn",
    )(proj, proj, proj, proj, proj, lb, nw)


def _mix_kernel(oa_ref, or_ref, wa_ref, wr_ref, x_ref, g1_ref, sc_ref, sh_ref, lg_ref, lbias_ref,
                wrt_ref, brt_ref, x1_ref, h2_ref, idx_ref, gate_ref, rank_ref, cnt_ref, carry_ref):
    i = pl.program_id(0)

    @pl.when(i == 0)
    def _():
        carry_ref[...] = jnp.zeros_like(carry_ref)

    y = jnp.dot(oa_ref[...], wa_ref[...], preferred_element_type=F32)
    y = y + jnp.dot(or_ref[...], wr_ref[...], preferred_element_type=F32)
    x1 = _layer_norm(DEEPNORM_ALPHA * x_ref[...] + g1_ref[...] * y) * lg_ref[...] + lbias_ref[...]
    x1_ref[...] = x1
    h2 = _layer_norm(x1) * (1.0 + sc_ref[...]) + sh_ref[...]
    h2_ref[...] = h2
    logits = jnp.dot(h2.astype(BF16), wrt_ref[...], preferred_element_type=F32) + brt_ref[...]

    tm = logits.shape[0]
    lane = lax.broadcasted_iota(jnp.int32, (tm, LANES), 1)
    neg = jnp.float32(-jnp.inf)
    work = logits
    vals, sels = [], []
    for _ in range(TOP_K):
        m = jnp.max(work, axis=-1, keepdims=True)
        sel = jnp.min(jnp.where(work == m, lane, LANES), axis=-1, keepdims=True)
        vals.append(m)
        sels.append(sel)
        work = jnp.where(lane == sel, neg, work)
    es = [jnp.exp(v - vals[0]) for v in vals]
    denom = es[0] + es[1] + es[2] + es[3]

    multi = jnp.zeros((tm, LANES), F32)
    for sel in sels:
        multi = multi + jnp.where(lane == sel, 1.0, 0.0)
    r = lax.broadcasted_iota(jnp.int32, (tm, tm), 0)
    c = lax.broadcasted_iota(jnp.int32, (tm, tm), 1)
    strict = jnp.where(r > c, 1.0, 0.0).astype(BF16)
    before = jnp.dot(strict, multi.astype(BF16), preferred_element_type=F32) + carry_ref[...]
    carry_ref[...] = carry_ref[...] + jnp.sum(multi, axis=0, keepdims=True)
    cnt_ref[...] = carry_ref[...]

    idx_out = jnp.zeros((tm, LANES), jnp.int32)
    gate_out = jnp.zeros((tm, LANES), F32)
    rank_out = jnp.zeros((tm, LANES), F32)
    for k in range(TOP_K):
        rk = jnp.sum(jnp.where(lane == sels[k], before, 0.0), axis=-1, keepdims=True)
        idx_out = jnp.where(lane == k, sels[k], idx_out)
        gate_out = jnp.where(lane == k, es[k] / denom, gate_out)
        rank_out = jnp.where(lane == k, rk, rank_out)
    idx_ref[...] = idx_out
    gate_ref[...] = gate_out
    rank_ref[...] = rank_out.astype(jnp.int32)


def _mix(o_attn, o_r, wa, wr, x2, g1, sc2, sh2, ln_g, ln_b, w_rt, b_rt):
    tiles_per_batch = SEQ // MIX_TM
    rows = lambda w: pl.BlockSpec((MIX_TM, w), lambda i: (i, 0))
    full = lambda a, b: pl.BlockSpec((a, b), lambda i: (0, 0))
    per_batch = pl.BlockSpec((None, 1, D_MODEL), lambda i: (i // tiles_per_batch, 0, 0))
    return pl.pallas_call(
        _mix_kernel,
        grid=(N_TOK // MIX_TM,),
        in_specs=[rows(ATTN_WIDTH), rows(HGRN_WIDTH), full(ATTN_WIDTH, D_MODEL), full(HGRN_WIDTH, D_MODEL),
                  rows(D_MODEL), per_batch, per_batch, per_batch, full(1, D_MODEL), full(1, D_MODEL),
                  full(D_MODEL, LANES), full(1, LANES)],
        out_specs=[rows(D_MODEL), rows(D_MODEL), rows(LANES), rows(LANES), rows(LANES), full(1, LANES)],
        out_shape=[jax.ShapeDtypeStruct((N_TOK, D_MODEL), F32),
                   jax.ShapeDtypeStruct((N_TOK, D_MODEL), F32),
                   jax.ShapeDtypeStruct((N_TOK, LANES), jnp.int32),
                   jax.ShapeDtypeStruct((N_TOK, LANES), F32),
                   jax.ShapeDtypeStruct((N_TOK, LANES), jnp.int32),
                   jax.ShapeDtypeStruct((1, LANES), F32)],
        scratch_shapes=[pltpu.VMEM((1, LANES), F32)],
        compiler_params=_cparams(("arbitrary",)),
        name="mix",
    )(o_attn, o_r, wa, wr, x2, g1, sc2, sh2, ln_g, ln_b, w_rt, b_rt)


def _wait_rows(n, make_copy):
    for bit in range(MOE_SB.bit_length()):
        @pl.when(((n >> bit) & 1) == 1)
        def _(bit=bit):
            make_copy(1 << bit).wait()


def _for_rows(lo, hi, fn):
    groups = (hi - lo) // MOE_ISSUE_UNROLL

    def group(q, carry):
        base = lo + q * MOE_ISSUE_UNROLL
        for u in range(MOE_ISSUE_UNROLL):
            fn(base + u)
        return carry

    def single(r, carry):
        fn(r)
        return carry

    lax.fori_loop(0, groups, group, 0)
    lax.fori_loop(lo + groups * MOE_ISSUE_UNROLL, hi, single, 0)


def _moe_kernel(se_ref, so_ref, sn_ref, src_ref, dst_ref, h2_hbm, w1g_ref, w1l_ref, b1g_ref, b1l_ref, w2_ref,
                b2_ref, y_hbm, xg_ref, xb_ref, acc_ref, gsem, ssem):
    g = pl.program_id(0)
    j = pl.program_id(1)
    nj = pl.num_programs(1)
    n_assign = N_TOK * TOP_K

    def tiles_of(rows):
        return (rows + MOE_TMI - 1) // MOE_TMI

    n = sn_ref[g]
    nt = tiles_of(n)
    slot = g % 2
    g_next = jnp.minimum(g + 1, MOE_G - 1)
    n_next = jnp.where(g + 1 < MOE_G, sn_ref[g_next], 0)
    off_next = so_ref[g_next]
    g_prev = jnp.maximum(g - 1, 0)
    n_prev = jnp.where(g >= 1, sn_ref[g_prev], 0)
    off_prev = so_ref[g_prev]
    eager = nt * nj * MOE_Q

    def gather_copy(tok, r):
        return pltpu.make_async_copy(h2_hbm.at[pl.ds(tok, 1)], xg_ref.at[pl.ds(r, 1)], gsem)

    def scatter_copy(s, r, a):
        return pltpu.make_async_copy(acc_ref.at[s, pl.ds(r, 1)], y_hbm.at[pl.ds(a, 1)], ssem.at[s])

    def gather_start(r):
        gather_copy(src_ref[jnp.minimum(off_next + r, n_assign - 1)], r).start()

    def scatter_start(r):
        dest = jnp.where(r < n_prev, dst_ref[jnp.minimum(off_prev + r, n_assign - 1)], n_assign + r)
        scatter_copy(1 - slot, r, dest).start()

    def eager_issue(first_tile, tiles):
        base = (j * nt + first_tile) * MOE_Q
        for q in range(tiles * MOE_Q):
            gather_start(base + q)
            scatter_start(base + q)

    def tile_rows(i):
        return pl.ds(pl.multiple_of(i * MOE_TMI, MOE_TMI), MOE_TMI)

    @pl.when(j == 0)
    def _():
        @pl.when(g == 0)
        def _():
            xg_ref[...] = jnp.zeros_like(xg_ref)
            acc_ref[...] = jnp.zeros_like(acc_ref)
            spare = pltpu.make_async_copy(acc_ref.at[0], y_hbm.at[pl.ds(n_assign, MOE_SB)], ssem.at[0])
            spare.start()
            spare.wait()
            _for_rows(0, n, lambda r: gather_copy(src_ref[so_ref[0] + r], r).start())

        nt_prev = jnp.where(g >= 1, tiles_of(n_prev), 0)
        n_prev2 = jnp.where(g >= 2, sn_ref[jnp.maximum(g - 2, 0)], 0)
        gathered = jnp.where(g >= 1, jnp.maximum(nt_prev * nj * MOE_Q, n), n)
        scattered = jnp.maximum(nt_prev * nj * MOE_Q, n_prev2)
        _wait_rows(gathered,
                   lambda k: pltpu.make_async_copy(h2_hbm.at[pl.ds(0, k)], xg_ref.at[pl.ds(0, k)], gsem))
        _wait_rows(scattered,
                   lambda k: pltpu.make_async_copy(acc_ref.at[slot, pl.ds(0, k)], y_hbm.at[pl.ds(0, k)],
                                                   ssem.at[slot]))

        def prep(i, carry):
            rows = tile_rows(i)
            xb_ref[rows, :] = xg_ref[rows, :].astype(BF16)
            acc_ref[slot, rows, :] = jnp.broadcast_to(b2_ref[...], (MOE_TMI, D_MODEL))
            return carry

        lax.fori_loop(0, nt, prep, 0)

    @pl.when(nt > 0)
    def _():
        w1g = w1g_ref[...].astype(BF16)
        w1l = w1l_ref[...].astype(BF16)
        w2 = w2_ref[...].astype(BF16)

        def tiles(first_tile, count):
            eager_issue(first_tile, count)
            rows = [tile_rows(first_tile + t) for t in range(count)]
            xs = [xb_ref[r, :] for r in rows]
            hid = [(jnp.dot(x, w1g, preferred_element_type=F32) + b1g_ref[...],
                    jnp.dot(x, w1l, preferred_element_type=F32) + b1l_ref[...]) for x in xs]
            parts = []
            for hg, hl in hid:
                hg = jnp.minimum(hg, SWIGLU_LIMIT)
                hl = jnp.clip(hl, -SWIGLU_LIMIT, SWIGLU_LIMIT)
                act = hg * _sigmoid(SWIGLU_ALPHA * hg) * (hl + 1.0)
                parts.append(jnp.dot(act.astype(BF16), w2, preferred_element_type=F32))
            for r, part in zip(rows, parts):
                acc_ref[slot, r, :] += part

        def pair(p, carry):
            tiles(2 * p, 2)
            return carry

        lax.fori_loop(0, nt // 2, pair, 0)

        @pl.when(nt % 2 == 1)
        def _():
            tiles(nt - 1, 1)

    @pl.when(j == nj - 1)
    def _():
        _for_rows(jnp.minimum(eager, n_next), n_next, gather_start)
        _for_rows(jnp.minimum(eager, n_prev), n_prev, scatter_start)

        @pl.when(g == MOE_G - 1)
        def _():
            _wait_rows(n_prev, lambda k: pltpu.make_async_copy(acc_ref.at[1 - slot, pl.ds(0, k)],
                                                                y_hbm.at[pl.ds(0, k)], ssem.at[1 - slot]))


def _moe(sb_expert, sb_off, sb_n, src_tok, dst_row, h2, w1, b1, w2, b2):
    nj = D_EXPERT // MOE_TH

    def hidden(j, sn, g):
        return jnp.where(sn[g] > 0, j, nj - 1)

    grid_spec = pltpu.PrefetchScalarGridSpec(
        num_scalar_prefetch=5,
        grid=(MOE_G, nj),
        in_specs=[
            pl.BlockSpec(memory_space=pl.ANY),
            pl.BlockSpec((None, D_MODEL, MOE_TH), lambda g, j, se, so, sn, *_: (se[g], 0, hidden(j, sn, g))),
            pl.BlockSpec((None, D_MODEL, MOE_TH), lambda g, j, se, so, sn, *_: (se[g], 0, nj + hidden(j, sn, g))),
            pl.BlockSpec((None, 1, MOE_TH), lambda g, j, se, so, sn, *_: (se[g], 0, hidden(j, sn, g))),
            pl.BlockSpec((None, 1, MOE_TH), lambda g, j, se, so, sn, *_: (se[g], 0, nj + hidden(j, sn, g))),
            pl.BlockSpec((None, MOE_TH, D_MODEL), lambda g, j, se, so, sn, *_: (se[g], hidden(j, sn, g), 0)),
            pl.BlockSpec((None, 1, D_MODEL), lambda g, j, se, so, sn, *_: (se[g], 0, 0)),
        ],
        out_specs=pl.BlockSpec(memory_space=pl.ANY),
        scratch_shapes=[pltpu.VMEM((MOE_SB, D_MODEL), F32),
                        pltpu.VMEM((MOE_SB, D_MODEL), BF16),
                        pltpu.VMEM((2, MOE_SB, D_MODEL), F32),
                        pltpu.SemaphoreType.DMA(()),
                        pltpu.SemaphoreType.DMA((2,))],
    )
    return pl.pallas_call(
        _moe_kernel,
        grid_spec=grid_spec,
        out_shape=jax.ShapeDtypeStruct((N_TOK * TOP_K + MOE_SB, D_MODEL), F32),
        compiler_params=_cparams(("arbitrary", "arbitrary")),
        name="moe",
    )(sb_expert, sb_off, sb_n, src_tok, dst_row, h2, w1, w1, b1, b1, w2, b2)


def _final_kernel(x1_ref, y0_ref, y1_ref, y2_ref, y3_ref, gate_ref, g2_ref, lg_ref, lb_ref, o_ref):
    y = gate_ref[:, 0:1] * y0_ref[...]
    for k, y_ref in enumerate((y1_ref, y2_ref, y3_ref), start=1):
        y = y + gate_ref[:, k:k + 1] * y_ref[...]
    z = DEEPNORM_ALPHA * x1_ref[...] + g2_ref[...] * y
    o_ref[...] = _layer_norm(z) * lg_ref[...] + lb_ref[...]


def _final(x1, y4, gates, g2, ln_g, ln_b):
    tiles_per_batch = SEQ // FIN_TM
    tiles = N_TOK // FIN_TM
    rows = lambda w: pl.BlockSpec((FIN_TM, w), lambda i: (i, 0))
    plane = lambda k: pl.BlockSpec((FIN_TM, D_MODEL), lambda i: (k * tiles + i, 0))
    vec = pl.BlockSpec((1, D_MODEL), lambda i: (0, 0))
    return pl.pallas_call(
        _final_kernel,
        grid=(tiles,),
        in_specs=[rows(D_MODEL), plane(0), plane(1), plane(2), plane(3), rows(LANES),
                  pl.BlockSpec((None, 1, D_MODEL), lambda i: (i // tiles_per_batch, 0, 0)), vec, vec],
        out_specs=rows(D_MODEL),
        out_shape=jax.ShapeDtypeStruct((N_TOK, D_MODEL), F32),
        compiler_params=_cparams(("arbitrary",)),
        name="final",
    )(x1, y4, y4, y4, y4, gates, g2, ln_g, ln_b)


def _rope_tables():
    rows = SEQ // GRID_W
    t = np.arange(SEQ)
    row = (t // GRID_W - rows // 2).astype(np.float32)
    col = (t % GRID_W - GRID_W // 2).astype(np.float32)
    inv_freq = jnp.asarray(ROPE_THETA, F32) ** (-jnp.arange(0, ROPE_AXIS_DIM, 2, dtype=F32) / ROPE_AXIS_DIM)
    ang_row = jnp.asarray(row)[:, None] * inv_freq[None, :]
    ang_col = jnp.asarray(col)[:, None] * inv_freq[None, :]
    zeros = jnp.zeros_like(ang_row)
    cos = jnp.concatenate([jnp.cos(ang_row)] * 2 + [jnp.cos(ang_col)] * 2, axis=-1)
    sin_lo = jnp.concatenate([-jnp.sin(ang_row), zeros, -jnp.sin(ang_col), zeros], axis=-1)
    sin_hi = jnp.concatenate([zeros, jnp.sin(ang_row), zeros, jnp.sin(ang_col)], axis=-1)
    return cos, sin_lo, sin_hi


def _routing(top_i, rank, counts):
    counts = counts.astype(jnp.int32)
    start = jnp.cumsum(counts) - counts
    assign = jnp.arange(N_TOK * TOP_K, dtype=jnp.int32)
    dest = (start[top_i] + rank).reshape(-1)
    sorted_assign = jnp.zeros((N_TOK * TOP_K,), jnp.int32).at[dest].set(assign)
    nsb = (counts + MOE_SB - 1) // MOE_SB
    sb_end = jnp.cumsum(nsb)
    sb_start = sb_end - nsb
    g = jnp.arange(MOE_G, dtype=jnp.int32)
    active = g < sb_end[-1]
    e_of_g = jnp.minimum(jnp.sum(g[:, None] >= sb_end[None, :], axis=1), N_EXPERTS - 1).astype(jnp.int32)
    last_e = e_of_g[jnp.maximum(sb_end[-1] - 1, 0)]
    first_row = (g - sb_start[e_of_g]) * MOE_SB
    sb_n = jnp.where(active, jnp.clip(counts[e_of_g] - first_row, 0, MOE_SB), 0).astype(jnp.int32)
    sb_off = jnp.where(active, start[e_of_g] + first_row, 0).astype(jnp.int32)
    sb_expert = jnp.where(active, e_of_g, last_e).astype(jnp.int32)
    src_tok = sorted_assign // TOP_K
    dst_row = (sorted_assign % TOP_K) * N_TOK + src_tok
    return sb_expert, sb_off, sb_n, src_tok, dst_row


def kernel(x, c, w_ada, b_ada, w_in, q_norm_w, k_norm_w, attn_norm_w, hgrn_lb, hgrn_norm_w, w_out, ln1_g, ln1_b, w_router, b_router, w_exp_in, b_exp_in, w_exp_out, b_exp_out, ln2_g, ln2_b):
    c_pad = jnp.zeros((8, D_MODEL), F32).at[:BATCH].set(c)
    cos, sin_lo, sin_hi = _rope_tables()
    x2 = x.reshape(N_TOK, D_MODEL)
    for l in range(DEPTH):
        mod = _ada(c_pad, w_ada[l], b_ada[l][None, :])[:BATCH]
        sh1, sc1, g1, sh2, sc2, g2 = [m.reshape(BATCH, 1, D_MODEL) for m in jnp.split(mod, 6, axis=-1)]

        proj = _proj(x2, sc1, sh1, w_in[l].astype(BF16))
        o_attn = _attention(proj, cos, sin_lo, sin_hi, q_norm_w[l][None, :], k_norm_w[l][None, :],
                            attn_norm_w[l][None, :])
        lb = jnp.cumsum(jax.nn.softmax(hgrn_lb.astype(F32), axis=1), axis=1)[:, l]
        o_r = _hgrn(proj, lb.reshape(2, 1, HGRN_WIDTH), hgrn_norm_w[l][None, :])

        w_o = w_out[l].astype(BF16)
        w_rt = jnp.zeros((D_MODEL, LANES), BF16).at[:, :N_EXPERTS].set(w_router[l].astype(BF16))
        b_rt = jnp.full((1, LANES), -1e30, F32).at[0, :N_EXPERTS].set(b_router[l])
        x1, h2, idx, gates, rank, counts = _mix(
            o_attn, o_r, w_o[:ATTN_WIDTH], w_o[ATTN_WIDTH:], x2, g1, sc2, sh2,
            ln1_g[l][None, :], ln1_b[l][None, :], w_rt, b_rt)
        sb_expert, sb_off, sb_n, src_tok, dst_row = _routing(idx[:, :TOP_K], rank[:, :TOP_K],
                                                             counts[0, :N_EXPERTS])
        y4 = _moe(sb_expert, sb_off, sb_n, src_tok, dst_row, h2, w_exp_in[l], b_exp_in[l][:, None, :],
                  w_exp_out[l], b_exp_out[l][:, None, :])
        x2 = _final(x1, y4, gates, g2, ln2_g[l][None, :], ln2_b[l][None, :])
    return x2.reshape(BATCH, SEQ, D_MODEL)
```

```python
import functools
import math

import numpy as np
import jax
import jax.numpy as jnp
from jax import lax
from jax.experimental import pallas as pl
from jax.experimental.pallas import tpu as pltpu

F32 = jnp.float32
BF16 = jnp.bfloat16

D_MODEL = 2048
BATCH = 4
SEQ = 2048
DEPTH = 1
N_TOK = BATCH * SEQ
HEAD_DIM = 128
ATTN_WIDTH = 1024
N_Q_HEADS = 8
N_KV_HEADS = 2
KV_GROUP = 4
HGRN_WIDTH = 1024
N_HGRN_HEADS = 8
HGRN_CHUNK = 64
GRID_W = 64
ROPE_THETA = 10000.0
ROPE_AXIS_DIM = 64
N_EXPERTS = 32
TOP_K = 4
D_EXPERT = 2048
SWIGLU_LIMIT = 7.0
SWIGLU_ALPHA = 1.702
NORM_EPS = 1e-6
DEEPNORM_ALPHA = (2 * DEPTH) ** 0.25
PROJ_WIDTH = 6656
LANES = 128

COL_Q = 0
COL_K = 8
COL_V = 10
COL_QR = 12
COL_FF = 20
COL_FB = 28
COL_IN = 36
COL_GO = 44

VMEM_LIMIT = 56 * 1024 * 1024

ADA_TN = 1024
PROJ_TM = 1024
PROJ_TN = 1664
ATTN_TQ = 256
HGRN_HB = 2
HGRN_UN = 2
MIX_TM = 256
MOE_ISSUE_UNROLL = 8
MOE_SB = 1024
MOE_TMI = 256
MOE_TH = 256
MOE_G = N_TOK * TOP_K // MOE_SB + N_EXPERTS + 1
MOE_Q_STEP = 32
MOE_Q_TILE = (MOE_SB // (D_EXPERT // MOE_TH) - MOE_Q_STEP) // (MOE_SB // MOE_TMI)
FIN_TM = 256


def _cparams(sem):
    return pltpu.CompilerParams(dimension_semantics=sem, vmem_limit_bytes=VMEM_LIMIT)


def _sigmoid(x):
    return 1.0 / (1.0 + jnp.exp(-x))


def _layer_norm(x):
    mu = jnp.mean(x, axis=-1, keepdims=True)
    xc = x - mu
    var = jnp.mean(xc * xc, axis=-1, keepdims=True)
    return xc * lax.rsqrt(var + NORM_EPS)


def _rms(x):
    return x * lax.rsqrt(jnp.mean(x * x, axis=-1, keepdims=True) + NORM_EPS)


def _ada_kernel(c_ref, w_ref, b_ref, o_ref):
    c = c_ref[...]
    ca = c * _sigmoid(c)
    o_ref[...] = jnp.dot(ca.astype(BF16), w_ref[...].astype(BF16),
                         preferred_element_type=F32) + b_ref[...]


def _ada(c_pad, w, b):
    n = w.shape[1]
    return pl.pallas_call(
        _ada_kernel,
        grid=(n // ADA_TN,),
        in_specs=[pl.BlockSpec((8, D_MODEL), lambda j: (0, 0)),
                  pl.BlockSpec((D_MODEL, ADA_TN), lambda j: (0, j)),
                  pl.BlockSpec((1, ADA_TN), lambda j: (0, j))],
        out_specs=pl.BlockSpec((8, ADA_TN), lambda j: (0, j)),
        out_shape=jax.ShapeDtypeStruct((8, n), F32),
        compiler_params=_cparams(("arbitrary",)),
        name="ada",
    )(c_pad, w, b)


def _proj_kernel(x_ref, sc_ref, sh_ref, w_ref, o_ref, h_ref):
    @pl.when(pl.program_id(1) == 0)
    def _():
        h = _layer_norm(x_ref[...]) * (1.0 + sc_ref[...]) + sh_ref[...]
        h_ref[...] = h.astype(BF16)

    o_ref[...] = jnp.dot(h_ref[...], w_ref[...], preferred_element_type=F32).astype(BF16)


def _proj(x2, sc, sh, w_bf):
    tiles_per_batch = SEQ // PROJ_TM
    return pl.pallas_call(
        _proj_kernel,
        grid=(N_TOK // PROJ_TM, PROJ_WIDTH // PROJ_TN),
        in_specs=[pl.BlockSpec((PROJ_TM, D_MODEL), lambda i, j: (i, 0)),
                  pl.BlockSpec((None, 1, D_MODEL), lambda i, j: (i // tiles_per_batch, 0, 0)),
                  pl.BlockSpec((None, 1, D_MODEL), lambda i, j: (i // tiles_per_batch, 0, 0)),
                  pl.BlockSpec((D_MODEL, PROJ_TN), lambda i, j: (0, j))],
        out_specs=pl.BlockSpec((PROJ_TM, PROJ_TN), lambda i, j: (i, j)),
        out_shape=jax.ShapeDtypeStruct((N_TOK, PROJ_WIDTH), BF16),
        scratch_shapes=[pltpu.VMEM((PROJ_TM, D_MODEL), BF16)],
        compiler_params=_cparams(("arbitrary", "arbitrary")),
        name="proj",
    )(x2, sc, sh, w_bf)


def _rope(x, cos, sin_lo, sin_hi):
    return (x * cos + pltpu.roll(x, 96, axis=1) * sin_lo + pltpu.roll(x, 32, axis=1) * sin_hi)


def _attn_kernel(q_ref, k_ref, v_ref, cq_ref, slq_ref, shq_ref, ck_ref, slk_ref, shk_ref,
                 qw_ref, kw_ref, aw_ref, o_ref, kr_ref):
    @pl.when(pl.program_id(2) == 0)
    def _():
        k = _rms(k_ref[...].astype(F32)) * kw_ref[...]
        kr_ref[...] = _rope(k, ck_ref[...], slk_ref[...], shk_ref[...]).astype(BF16)

    scale = math.log2(math.e) / math.sqrt(HEAD_DIM)
    cq = cq_ref[...]
    slq = slq_ref[...]
    shq = shq_ref[...]
    heads = [slice(h * HEAD_DIM, (h + 1) * HEAD_DIM) for h in range(KV_GROUP)]
    qs = []
    for cols in heads:
        q = _rms(q_ref[:, cols].astype(F32)) * qw_ref[...]
        qs.append((_rope(q, cq, slq, shq) * scale).astype(BF16))
    scores = [lax.dot_general(q, kr_ref[...], (((1,), (1,)), ((), ())), preferred_element_type=F32)
              for q in qs]
    outs = []
    for s in scores:
        p = jnp.exp2(s - jnp.max(s, axis=-1, keepdims=True))
        l = jnp.sum(p, axis=-1, keepdims=True)
        outs.append(jnp.dot(p.astype(BF16), v_ref[...], preferred_element_type=F32) / l)
    for cols, o in zip(heads, outs):
        o_ref[:, cols] = (_rms(o) * aw_ref[:, cols]).astype(BF16)


def _attention(proj, cos, sin_lo, sin_hi, qw, kw, aw):
    nq = SEQ // ATTN_TQ
    gw = KV_GROUP * HEAD_DIM
    tab_q = pl.BlockSpec((ATTN_TQ, HEAD_DIM), lambda b, g, i: (i, 0))
    tab_k = pl.BlockSpec((SEQ, HEAD_DIM), lambda b, g, i: (0, 0))
    return pl.pallas_call(
        _attn_kernel,
        grid=(BATCH, N_KV_HEADS, nq),
        in_specs=[pl.BlockSpec((ATTN_TQ, gw), lambda b, g, i: (b * nq + i, g)),
                  pl.BlockSpec((SEQ, HEAD_DIM), lambda b, g, i: (b, COL_K + g)),
                  pl.BlockSpec((SEQ, HEAD_DIM), lambda b, g, i: (b, COL_V + g)),
                  tab_q, tab_q, tab_q, tab_k, tab_k, tab_k,
                  pl.BlockSpec((1, HEAD_DIM), lambda b, g, i: (0, 0)),
                  pl.BlockSpec((1, HEAD_DIM), lambda b, g, i: (0, 0)),
                  pl.BlockSpec((1, gw), lambda b, g, i: (0, g))],
        out_specs=pl.BlockSpec((ATTN_TQ, gw), lambda b, g, i: (b * nq + i, g)),
        out_shape=jax.ShapeDtypeStruct((N_TOK, ATTN_WIDTH), BF16),
        scratch_shapes=[pltpu.VMEM((SEQ, HEAD_DIM), BF16)],
        compiler_params=_cparams(("arbitrary", "arbitrary", "arbitrary")),
        name="attn",
    )(proj, proj, proj, cos, sin_lo, sin_hi, cos, sin_lo, sin_hi, qw, kw, aw)


def _hgrn_kernel(qr_ref, ff_ref, fb_ref, iv_ref, go_ref, lb_ref, nw_ref, o_ref, acc_ref, st_ref):
    C = HGRN_CHUNK
    nc = SEQ // C
    trips = nc // HGRN_UN
    row = lax.broadcasted_iota(jnp.int32, (C, C), 0)
    col = lax.broadcasted_iota(jnp.int32, (C, C), 1)
    keeps = (row >= col, row <= col)
    lasts = (C - 1, 0)
    f_refs = (ff_ref, fb_ref)

    nt_dims = (((1,), (1,)), ((), ()))
    tn_dims = (((0,), (0,)), ((), ()))
    st_ref[...] = jnp.zeros_like(st_ref)

    def trip(it, finish):
        chains = []
        for h in range(HGRN_HB):
            cols = slice(h * HEAD_DIM, (h + 1) * HEAD_DIM)
            for d in range(2):
                for u in range(HGRN_UN):
                    n = it * HGRN_UN + u
                    cidx = n if d == 0 else nc - 1 - n
                    chains.append(dict(h=h, d=d, cols=cols, rows=pl.ds(pl.multiple_of(cidx * C, C), C)))

        for ch in chains:
            d = ch["d"]
            lb = lb_ref[d, :, ch["cols"]]
            fg = lb + (1.0 - lb) * _sigmoid(f_refs[d][ch["rows"], ch["cols"]].astype(F32))
            ch["kk"] = 1.0 - fg
            lf = jnp.log(fg)
            lf_hi = lf.astype(BF16)
            lf_lo = (lf - lf_hi.astype(F32)).astype(BF16)
            tri = jnp.where(keeps[d], 1.0, 0.0).astype(BF16)
            ch["b"] = (jnp.dot(tri, lf_hi, preferred_element_type=F32)
                       + jnp.dot(tri, lf_lo, preferred_element_type=F32))
        for ch in chains:
            b = ch["b"]
            bl = b[lasts[ch["d"]]:lasts[ch["d"]] + 1, :]
            qx = qr_ref[ch["rows"], ch["cols"]].astype(F32)
            ch["qd"] = (qx * _sigmoid(qx) * jnp.exp(b)).astype(BF16)
            kd = (ch["kk"] * jnp.exp(-b)).astype(BF16)
            ku = (ch["kk"] * jnp.exp(bl - b)).astype(BF16)
            ch["v"] = iv_ref[ch["rows"], ch["cols"]]
            ch["decay"] = jnp.exp(bl)
            ch["sc"] = lax.dot_general(ch["qd"], kd, nt_dims, preferred_element_type=F32)
            ch["u_t"] = lax.dot_general(ch["v"], ku, tn_dims, preferred_element_type=F32)
        for h in range(HGRN_HB):
            for d in range(2):
                state = st_ref[2 * h + d]
                for ch in chains:
                    if ch["h"] == h and ch["d"] == d:
                        ch["state"] = state.astype(BF16)
                        state = state * ch["decay"] + ch["u_t"]
                st_ref[2 * h + d] = state
        for ch in chains:
            sc = jnp.where(keeps[ch["d"]], ch["sc"], 0.0).astype(BF16)
            ch["o"] = (jnp.dot(sc, ch["v"], preferred_element_type=F32)
                       + lax.dot_general(ch["qd"], ch["state"], nt_dims, preferred_element_type=F32))
        for ch in chains:
            rows, cols = ch["rows"], ch["cols"]
            if finish:
                o = _rms(acc_ref[rows, cols] + ch["o"]) * nw_ref[:, cols]
                g = go_ref[rows, cols].astype(F32)
                o_ref[rows, cols] = (o * (g * _sigmoid(g))).astype(BF16)
            else:
                acc_ref[rows, cols] = ch["o"]

    def first_half(it, carry):
        trip(it, False)
        return carry

    def second_half(it, carry):
        trip(it, True)
        return carry

    lax.fori_loop(0, trips // 2, first_half, 0)
    lax.fori_loop(trips // 2, trips, second_half, 0)


def _hgrn(proj, lb, nw):
    width = HGRN_HB * HEAD_DIM

    def col(c0):
        return pl.BlockSpec((SEQ, width), lambda b, h: (b, c0 // HGRN_HB + h))

    return pl.pallas_call(
        _hgrn_kernel,
        grid=(BATCH, N_HGRN_HEADS // HGRN_HB),
        in_specs=[col(COL_QR), col(COL_FF), col(COL_FB), col(COL_IN), col(COL_GO),
                  pl.BlockSpec((2, 1, width), lambda b, h: (0, 0, h)),
                  pl.BlockSpec((1, width), lambda b, h: (0, h))],
        out_specs=pl.BlockSpec((SEQ, width), lambda b, h: (b, h)),
        out_shape=jax.ShapeDtypeStruct((N_TOK, HGRN_WIDTH), BF16),
        scratch_shapes=[pltpu.VMEM((SEQ, width), F32),
                        pltpu.VMEM((2 * HGRN_HB, HEAD_DIM, HEAD_DIM), F32)],
        compiler_params=_cparams(("arbitrary", "arbitrary")),
        name="hgrn",
    )(proj, proj, proj, proj, proj, lb, nw)


def _mix_kernel(oa_ref, or_ref, wa_ref, wr_ref, x_ref, g1_ref, sc_ref, sh_ref, lg_ref, lbias_ref,
                wrt_ref, brt_ref, x1_ref, h2_ref, idx_ref, gate_ref, rank_ref, cnt_ref, carry_ref):
    i = pl.program_id(0)

    @pl.when(i == 0)
    def _():
        carry_ref[...] = jnp.zeros_like(carry_ref)

    y = jnp.dot(oa_ref[...], wa_ref[...], preferred_element_type=F32)
    y = y + jnp.dot(or_ref[...], wr_ref[...], preferred_element_type=F32)
    x1 = _layer_norm(DEEPNORM_ALPHA * x_ref[...] + g1_ref[...] * y) * lg_ref[...] + lbias_ref[...]
    x1_ref[...] = x1
    h2 = _layer_norm(x1) * (1.0 + sc_ref[...]) + sh_ref[...]
    h2_ref[...] = h2
    logits = jnp.dot(h2.astype(BF16), wrt_ref[...], preferred_element_type=F32) + brt_ref[...]

    tm = logits.shape[0]
    lane = lax.broadcasted_iota(jnp.int32, (tm, LANES), 1)
    neg = jnp.float32(-jnp.inf)
    work = logits
    vals, sels = [], []
    for _ in range(TOP_K):
        m = jnp.max(work, axis=-1, keepdims=True)
        sel = jnp.min(jnp.where(work == m, lane, LANES), axis=-1, keepdims=True)
        vals.append(m)
        sels.append(sel)
        work = jnp.where(lane == sel, neg, work)
    es = [jnp.exp(v - vals[0]) for v in vals]
    denom = es[0] + es[1] + es[2] + es[3]

    multi = jnp.zeros((tm, LANES), F32)
    for sel in sels:
        multi = multi + jnp.where(lane == sel, 1.0, 0.0)
    r = lax.broadcasted_iota(jnp.int32, (tm, tm), 0)
    c = lax.broadcasted_iota(jnp.int32, (tm, tm), 1)
    strict = jnp.where(r > c, 1.0, 0.0).astype(BF16)
    before = jnp.dot(strict, multi.astype(BF16), preferred_element_type=F32) + carry_ref[...]
    carry_ref[...] = carry_ref[...] + jnp.sum(multi, axis=0, keepdims=True)
    cnt_ref[...] = carry_ref[...]

    idx_out = jnp.zeros((tm, LANES), jnp.int32)
    gate_out = jnp.zeros((tm, LANES), F32)
    rank_out = jnp.zeros((tm, LANES), F32)
    for k in range(TOP_K):
        rk = jnp.sum(jnp.where(lane == sels[k], before, 0.0), axis=-1, keepdims=True)
        idx_out = jnp.where(lane == k, sels[k], idx_out)
        gate_out = jnp.where(lane == k, es[k] / denom, gate_out)
        rank_out = jnp.where(lane == k, rk, rank_out)
    idx_ref[...] = idx_out
    gate_ref[...] = gate_out
    rank_ref[...] = rank_out.astype(jnp.int32)


def _mix(o_attn, o_r, wa, wr, x2, g1, sc2, sh2, ln_g, ln_b, w_rt, b_rt):
    tiles_per_batch = SEQ // MIX_TM
    rows = lambda w: pl.BlockSpec((MIX_TM, w), lambda i: (i, 0))
    full = lambda a, b: pl.BlockSpec((a, b), lambda i: (0, 0))
    per_batch = pl.BlockSpec((None, 1, D_MODEL), lambda i: (i // tiles_per_batch, 0, 0))
    return pl.pallas_call(
        _mix_kernel,
        grid=(N_TOK // MIX_TM,),
        in_specs=[rows(ATTN_WIDTH), rows(HGRN_WIDTH), full(ATTN_WIDTH, D_MODEL), full(HGRN_WIDTH, D_MODEL),
                  rows(D_MODEL), per_batch, per_batch, per_batch, full(1, D_MODEL), full(1, D_MODEL),
                  full(D_MODEL, LANES), full(1, LANES)],
        out_specs=[rows(D_MODEL), rows(D_MODEL), rows(LANES), rows(LANES), rows(LANES), full(1, LANES)],
        out_shape=[jax.ShapeDtypeStruct((N_TOK, D_MODEL), F32),
                   jax.ShapeDtypeStruct((N_TOK, D_MODEL), F32),
                   jax.ShapeDtypeStruct((N_TOK, LANES), jnp.int32),
                   jax.ShapeDtypeStruct((N_TOK, LANES), F32),
                   jax.ShapeDtypeStruct((N_TOK, LANES), jnp.int32),
                   jax.ShapeDtypeStruct((1, LANES), F32)],
        scratch_shapes=[pltpu.VMEM((1, LANES), F32)],
        compiler_params=_cparams(("arbitrary",)),
        name="mix",
    )(o_attn, o_r, wa, wr, x2, g1, sc2, sh2, ln_g, ln_b, w_rt, b_rt)


def _wait_rows(n, make_copy):
    for bit in range(MOE_SB.bit_length()):
        @pl.when(((n >> bit) & 1) == 1)
        def _(bit=bit):
            make_copy(1 << bit).wait()


def _for_rows(lo, hi, fn):
    groups = (hi - lo) // MOE_ISSUE_UNROLL

    def group(q, carry):
        base = lo + q * MOE_ISSUE_UNROLL
        for u in range(MOE_ISSUE_UNROLL):
            fn(base + u)
        return carry

    def single(r, carry):
        fn(r)
        return carry

    lax.fori_loop(0, groups, group, 0)
    lax.fori_loop(lo + groups * MOE_ISSUE_UNROLL, hi, single, 0)


def _moe_kernel(se_ref, so_ref, sn_ref, src_ref, dst_ref, h2_hbm, w1g_ref, w1l_ref, b1g_ref, b1l_ref, w2_ref,
                b2_ref, y_hbm, xg_ref, xb_ref, acc_ref, gsem, ssem):
    g = pl.program_id(0)
    j = pl.program_id(1)
    nj = pl.num_programs(1)
    n_assign = N_TOK * TOP_K

    def tiles_of(rows):
        return (rows + MOE_TMI - 1) // MOE_TMI

    n = sn_ref[g]
    nt = tiles_of(n)
    slot = g % 2
    g_next = jnp.minimum(g + 1, MOE_G - 1)
    n_next = jnp.where(g + 1 < MOE_G, sn_ref[g_next], 0)
    off_next = so_ref[g_next]
    g_prev = jnp.maximum(g - 1, 0)
    n_prev = jnp.where(g >= 1, sn_ref[g_prev], 0)
    off_prev = so_ref[g_prev]
    step_rows = MOE_Q_STEP + nt * MOE_Q_TILE
    eager = jnp.where(nt > 0, nj * step_rows, 0)

    def gather_copy(tok, r):
        return pltpu.make_async_copy(h2_hbm.at[pl.ds(tok, 1)], xg_ref.at[pl.ds(r, 1)], gsem)

    def scatter_copy(s, r, a):
        return pltpu.make_async_copy(acc_ref.at[s, pl.ds(r, 1)], y_hbm.at[pl.ds(a, 1)], ssem.at[s])

    def gather_start(r):
        gather_copy(src_ref[off_next + r], r).start()

    def scatter_start(r):
        scatter_copy(1 - slot, r, dst_ref[off_prev + r]).start()

    def eager_issue(first, count):
        for q in range(count):
            gather_start(first + q)
            scatter_start(first + q)

    def tile_rows(i):
        return pl.ds(pl.multiple_of(i * MOE_TMI, MOE_TMI), MOE_TMI)

    @pl.when(j == 0)
    def _():
        @pl.when(g == 0)
        def _():
            xg_ref[...] = jnp.zeros_like(xg_ref)
            acc_ref[...] = jnp.zeros_like(acc_ref)
            spare = pltpu.make_async_copy(acc_ref.at[0], y_hbm.at[pl.ds(n_assign, MOE_SB)], ssem.at[0])
            spare.start()
            spare.wait()
            _for_rows(0, n, lambda r: gather_copy(src_ref[so_ref[0] + r], r).start())

        nt_prev = tiles_of(n_prev)
        eager_prev = jnp.where(nt_prev > 0, nj * (MOE_Q_STEP + nt_prev * MOE_Q_TILE), 0)
        n_prev2 = jnp.where(g >= 2, sn_ref[jnp.maximum(g - 2, 0)], 0)
        gathered = jnp.maximum(eager_prev, n)
        scattered = jnp.maximum(eager_prev, n_prev2)
        _wait_rows(gathered,
                   lambda k: pltpu.make_async_copy(h2_hbm.at[pl.ds(0, k)], xg_ref.at[pl.ds(0, k)], gsem))
        _wait_rows(scattered,
                   lambda k: pltpu.make_async_copy(acc_ref.at[slot, pl.ds(0, k)], y_hbm.at[pl.ds(0, k)],
                                                   ssem.at[slot]))

        def prep(i, carry):
            rows = tile_rows(i)
            xb_ref[rows, :] = xg_ref[rows, :].astype(BF16)
            acc_ref[slot, rows, :] = jnp.broadcast_to(b2_ref[...], (MOE_TMI, D_MODEL))
            return carry

        lax.fori_loop(0, nt, prep, 0)

    @pl.when(nt > 0)
    def _():
        eager_issue(j * step_rows, MOE_Q_STEP)
        w1g = w1g_ref[...].astype(BF16)
        w1l = w1l_ref[...].astype(BF16)
        w2 = w2_ref[...].astype(BF16)

        def tiles(first_tile, count):
            eager_issue(j * step_rows + MOE_Q_STEP + first_tile * MOE_Q_TILE, count * MOE_Q_TILE)
            rows = [tile_rows(first_tile + t) for t in range(count)]
            xs = [xb_ref[r, :] for r in rows]
            hid = [(jnp.dot(x, w1g, preferred_element_type=F32) + b1g_ref[...],
                    jnp.dot(x, w1l, preferred_element_type=F32) + b1l_ref[...]) for x in xs]
            parts = []
            for hg, hl in hid:
                hg = jnp.minimum(hg, SWIGLU_LIMIT)
                hl = jnp.clip(hl, -SWIGLU_LIMIT, SWIGLU_LIMIT)
                act = hg * _sigmoid(SWIGLU_ALPHA * hg) * (hl + 1.0)
                parts.append(jnp.dot(act.astype(BF16), w2, preferred_element_type=F32))
            for r, part in zip(rows, parts):
                acc_ref[slot, r, :] += part

        def pair(p, carry):
            tiles(2 * p, 2)
            return carry

        lax.fori_loop(0, nt // 2, pair, 0)

        @pl.when(nt % 2 == 1)
        def _():
            tiles(nt - 1, 1)

    @pl.when(j == nj - 1)
    def _():
        _for_rows(jnp.minimum(eager, n_next), n_next, gather_start)
        _for_rows(jnp.minimum(eager, n_prev), n_prev, scatter_start)

        @pl.when(g == MOE_G - 1)
        def _():
            _wait_rows(n_prev, lambda k: pltpu.make_async_copy(acc_ref.at[1 - slot, pl.ds(0, k)],
                                                                y_hbm.at[pl.ds(0, k)], ssem.at[1 - slot]))


def _moe(sb_expert, sb_off, sb_n, src_tok, dst_row, h2, w1, b1, w2, b2):
    nj = D_EXPERT // MOE_TH

    def hidden(j, sn, g):
        return jnp.where(sn[g] > 0, j, nj - 1)

    grid_spec = pltpu.PrefetchScalarGridSpec(
        num_scalar_prefetch=5,
        grid=(MOE_G, nj),
        in_specs=[
            pl.BlockSpec(memory_space=pl.ANY),
            pl.BlockSpec((None, D_MODEL, MOE_TH), lambda g, j, se, so, sn, *_: (se[g], 0, hidden(j, sn, g))),
            pl.BlockSpec((None, D_MODEL, MOE_TH), lambda g, j, se, so, sn, *_: (se[g], 0, nj + hidden(j, sn, g))),
            pl.BlockSpec((None, 1, MOE_TH), lambda g, j, se, so, sn, *_: (se[g], 0, hidden(j, sn, g))),
            pl.BlockSpec((None, 1, MOE_TH), lambda g, j, se, so, sn, *_: (se[g], 0, nj + hidden(j, sn, g))),
            pl.BlockSpec((None, MOE_TH, D_MODEL), lambda g, j, se, so, sn, *_: (se[g], hidden(j, sn, g), 0)),
            pl.BlockSpec((None, 1, D_MODEL), lambda g, j, se, so, sn, *_: (se[g], 0, 0)),
        ],
        out_specs=pl.BlockSpec(memory_space=pl.ANY),
        scratch_shapes=[pltpu.VMEM((MOE_SB, D_MODEL), F32),
                        pltpu.VMEM((MOE_SB, D_MODEL), BF16),
                        pltpu.VMEM((2, MOE_SB, D_MODEL), F32),
                        pltpu.SemaphoreType.DMA(()),
                        pltpu.SemaphoreType.DMA((2,))],
    )
    return pl.pallas_call(
        _moe_kernel,
        grid_spec=grid_spec,
        out_shape=jax.ShapeDtypeStruct((N_TOK * TOP_K + MOE_SB, D_MODEL), F32),
        compiler_params=_cparams(("arbitrary", "arbitrary")),
        name="moe",
    )(sb_expert, sb_off, sb_n, src_tok, dst_row, h2, w1, w1, b1, b1, w2, b2)


def _final_kernel(x1_ref, y0_ref, y1_ref, y2_ref, y3_ref, gate_ref, g2_ref, lg_ref, lb_ref, o_ref):
    y = gate_ref[:, 0:1] * y0_ref[...]
    for k, y_ref in enumerate((y1_ref, y2_ref, y3_ref), start=1):
        y = y + gate_ref[:, k:k + 1] * y_ref[...]
    z = DEEPNORM_ALPHA * x1_ref[...] + g2_ref[...] * y
    o_ref[...] = _layer_norm(z) * lg_ref[...] + lb_ref[...]


def _final(x1, y4, gates, g2, ln_g, ln_b):
    tiles_per_batch = SEQ // FIN_TM
    tiles = N_TOK // FIN_TM
    rows = lambda w: pl.BlockSpec((FIN_TM, w), lambda i: (i, 0))
    plane = lambda k: pl.BlockSpec((FIN_TM, D_MODEL), lambda i: (k * tiles + i, 0))
    vec = pl.BlockSpec((1, D_MODEL), lambda i: (0, 0))
    return pl.pallas_call(
        _final_kernel,
        grid=(tiles,),
        in_specs=[rows(D_MODEL), plane(0), plane(1), plane(2), plane(3), rows(LANES),
                  pl.BlockSpec((None, 1, D_MODEL), lambda i: (i // tiles_per_batch, 0, 0)), vec, vec],
        out_specs=rows(D_MODEL),
        out_shape=jax.ShapeDtypeStruct((N_TOK, D_MODEL), F32),
        compiler_params=_cparams(("arbitrary",)),
        name="final",
    )(x1, y4, y4, y4, y4, gates, g2, ln_g, ln_b)


def _rope_tables():
    rows = SEQ // GRID_W
    t = np.arange(SEQ)
    row = (t // GRID_W - rows // 2).astype(np.float32)
    col = (t % GRID_W - GRID_W // 2).astype(np.float32)
    inv_freq = jnp.asarray(ROPE_THETA, F32) ** (-jnp.arange(0, ROPE_AXIS_DIM, 2, dtype=F32) / ROPE_AXIS_DIM)
    ang_row = jnp.asarray(row)[:, None] * inv_freq[None, :]
    ang_col = jnp.asarray(col)[:, None] * inv_freq[None, :]
    zeros = jnp.zeros_like(ang_row)
    cos = jnp.concatenate([jnp.cos(ang_row)] * 2 + [jnp.cos(ang_col)] * 2, axis=-1)
    sin_lo = jnp.concatenate([-jnp.sin(ang_row), zeros, -jnp.sin(ang_col), zeros], axis=-1)
    sin_hi = jnp.concatenate([zeros, jnp.sin(ang_row), zeros, jnp.sin(ang_col)], axis=-1)
    return cos, sin_lo, sin_hi


def _routing(top_i, rank, counts):
    counts = counts.astype(jnp.int32)
    start = jnp.cumsum(counts) - counts
    assign = jnp.arange(N_TOK * TOP_K, dtype=jnp.int32)
    dest = (start[top_i] + rank).reshape(-1)
    sorted_assign = jnp.zeros((N_TOK * TOP_K,), jnp.int32).at[dest].set(assign)
    nsb = (counts + MOE_SB - 1) // MOE_SB
    sb_end = jnp.cumsum(nsb)
    sb_start = sb_end - nsb
    g = jnp.arange(MOE_G, dtype=jnp.int32)
    active = g < sb_end[-1]
    e_of_g = jnp.minimum(jnp.sum(g[:, None] >= sb_end[None, :], axis=1), N_EXPERTS - 1).astype(jnp.int32)
    last_e = e_of_g[jnp.maximum(sb_end[-1] - 1, 0)]
    first_row = (g - sb_start[e_of_g]) * MOE_SB
    sb_n = jnp.where(active, jnp.clip(counts[e_of_g] - first_row, 0, MOE_SB), 0).astype(jnp.int32)
    sb_off = jnp.where(active, start[e_of_g] + first_row, 0).astype(jnp.int32)
    sb_expert = jnp.where(active, e_of_g, last_e).astype(jnp.int32)
    src_tok = sorted_assign // TOP_K
    dst_row = (sorted_assign % TOP_K) * N_TOK + src_tok
    src_tok = jnp.concatenate([src_tok, jnp.zeros((MOE_SB,), jnp.int32)])
    dst_row = jnp.concatenate([dst_row, N_TOK * TOP_K + jnp.arange(MOE_SB, dtype=jnp.int32)])
    return sb_expert, sb_off, sb_n, src_tok, dst_row


def kernel(x, c, w_ada, b_ada, w_in, q_norm_w, k_norm_w, attn_norm_w, hgrn_lb, hgrn_norm_w, w_out, ln1_g, ln1_b, w_router, b_router, w_exp_in, b_exp_in, w_exp_out, b_exp_out, ln2_g, ln2_b):
    c_pad = jnp.zeros((8, D_MODEL), F32).at[:BATCH].set(c)
    cos, sin_lo, sin_hi = _rope_tables()
    x2 = x.reshape(N_TOK, D_MODEL)
    for l in range(DEPTH):
        mod = _ada(c_pad, w_ada[l], b_ada[l][None, :])[:BATCH]
        sh1, sc1, g1, sh2, sc2, g2 = [m.reshape(BATCH, 1, D_MODEL) for m in jnp.split(mod, 6, axis=-1)]

        proj = _proj(x2, sc1, sh1, w_in[l].astype(BF16))
        o_attn = _attention(proj, cos, sin_lo, sin_hi, q_norm_w[l][None, :], k_norm_w[l][None, :],
                            attn_norm_w[l][None, :])
        lb = jnp.cumsum(jax.nn.softmax(hgrn_lb.astype(F32), axis=1), axis=1)[:, l]
        o_r = _hgrn(proj, lb.reshape(2, 1, HGRN_WIDTH), hgrn_norm_w[l][None, :])

        w_o = w_out[l].astype(BF16)
        w_rt = jnp.zeros((D_MODEL, LANES), BF16).at[:, :N_EXPERTS].set(w_router[l].astype(BF16))
        b_rt = jnp.full((1, LANES), -1e30, F32).at[0, :N_EXPERTS].set(b_router[l])
        x1, h2, idx, gates, rank, counts = _mix(
            o_attn, o_r, w_o[:ATTN_WIDTH], w_o[ATTN_WIDTH:], x2, g1, sc2, sh2,
            ln1_g[l][None, :], ln1_b[l][None, :], w_rt, b_rt)
        sb_expert, sb_off, sb_n, src_tok, dst_row = _routing(idx[:, :TOP_K], rank[:, :TOP_K],
                                                             counts[0, :N_EXPERTS])
        y4 = _moe(sb_expert, sb_off, sb_n, src_tok, dst_row, h2, w_exp_in[l], b_exp_in[l][:, None, :],
                  w_exp_out[l], b_exp_out[l][:, None, :])
        x2 = _final(x1, y4, gates, g2, ln2_g[l][None, :], ln2_b[l][None, :])
    return x2.reshape(BATCH, SEQ, D_MODEL)
```

```python
import functools
import math

import numpy as np
import jax
import jax.numpy as jnp
from jax import lax
from jax.experimental import pallas as pl
from jax.experimental.pallas import tpu as pltpu

F32 = jnp.float32
BF16 = jnp.bfloat16

D_MODEL = 2048
BATCH = 4
SEQ = 2048
DEPTH = 1
N_TOK = BATCH * SEQ
HEAD_DIM = 128
ATTN_WIDTH = 1024
N_Q_HEADS = 8
N_KV_HEADS = 2
KV_GROUP = 4
HGRN_WIDTH = 1024
N_HGRN_HEADS = 8
HGRN_CHUNK = 64
GRID_W = 64
ROPE_THETA = 10000.0
ROPE_AXIS_DIM = 64
N_EXPERTS = 32
TOP_K = 4
D_EXPERT = 2048
SWIGLU_LIMIT = 7.0
SWIGLU_ALPHA = 1.702
NORM_EPS = 1e-6
DEEPNORM_ALPHA = (2 * DEPTH) ** 0.25
PROJ_WIDTH = 6656
LANES = 128

COL_Q = 0
COL_K = 8
COL_V = 10
COL_QR = 12
COL_FF = 20
COL_FB = 28
COL_IN = 36
COL_GO = 44

VMEM_LIMIT = 56 * 1024 * 1024

ADA_TN = 1024
PROJ_TM = 1024
PROJ_TN = 1664
ATTN_TQ = 256
HGRN_HB = 2
HGRN_UN = 2
MIX_TM = 256
MOE_ISSUE_UNROLL = 8
MOE_SB = 1024
MOE_TMI = 256
MOE_TH = 256
MOE_NJ = D_EXPERT // MOE_TH
MOE_G = N_TOK * TOP_K // MOE_SB + N_EXPERTS + 1
MOE_Q_STEP = 32
MOE_Q_TILE = (MOE_SB // (D_EXPERT // MOE_TH) - MOE_Q_STEP) // (MOE_SB // MOE_TMI)
FIN_TM = 256


def _cparams(sem):
    return pltpu.CompilerParams(dimension_semantics=sem, vmem_limit_bytes=VMEM_LIMIT)


def _sigmoid(x):
    return 1.0 / (1.0 + jnp.exp(-x))


def _layer_norm(x):
    mu = jnp.mean(x, axis=-1, keepdims=True)
    xc = x - mu
    var = jnp.mean(xc * xc, axis=-1, keepdims=True)
    return xc * lax.rsqrt(var + NORM_EPS)


def _rms(x):
    return x * lax.rsqrt(jnp.mean(x * x, axis=-1, keepdims=True) + NORM_EPS)


def _ada_kernel(c_ref, w_ref, b_ref, o_ref):
    c = c_ref[...]
    ca = c * _sigmoid(c)
    o_ref[...] = jnp.dot(ca.astype(BF16), w_ref[...].astype(BF16),
                         preferred_element_type=F32) + b_ref[...]


def _ada(c_pad, w, b):
    n = w.shape[1]
    return pl.pallas_call(
        _ada_kernel,
        grid=(n // ADA_TN,),
        in_specs=[pl.BlockSpec((8, D_MODEL), lambda j: (0, 0)),
                  pl.BlockSpec((D_MODEL, ADA_TN), lambda j: (0, j)),
                  pl.BlockSpec((1, ADA_TN), lambda j: (0, j))],
        out_specs=pl.BlockSpec((8, ADA_TN), lambda j: (0, j)),
        out_shape=jax.ShapeDtypeStruct((8, n), F32),
        compiler_params=_cparams(("arbitrary",)),
        name="ada",
    )(c_pad, w, b)


def _proj_kernel(x_ref, sc_ref, sh_ref, w_ref, o_ref, h_ref):
    @pl.when(pl.program_id(1) == 0)
    def _():
        h = _layer_norm(x_ref[...]) * (1.0 + sc_ref[...]) + sh_ref[...]
        h_ref[...] = h.astype(BF16)

    o_ref[...] = jnp.dot(h_ref[...], w_ref[...], preferred_element_type=F32).astype(BF16)


def _proj(x2, sc, sh, w_bf):
    tiles_per_batch = SEQ // PROJ_TM
    return pl.pallas_call(
        _proj_kernel,
        grid=(N_TOK // PROJ_TM, PROJ_WIDTH // PROJ_TN),
        in_specs=[pl.BlockSpec((PROJ_TM, D_MODEL), lambda i, j: (i, 0)),
                  pl.BlockSpec((None, 1, D_MODEL), lambda i, j: (i // tiles_per_batch, 0, 0)),
                  pl.BlockSpec((None, 1, D_MODEL), lambda i, j: (i // tiles_per_batch, 0, 0)),
                  pl.BlockSpec((D_MODEL, PROJ_TN), lambda i, j: (0, j))],
        out_specs=pl.BlockSpec((PROJ_TM, PROJ_TN), lambda i, j: (i, j)),
        out_shape=jax.ShapeDtypeStruct((N_TOK, PROJ_WIDTH), BF16),
        scratch_shapes=[pltpu.VMEM((PROJ_TM, D_MODEL), BF16)],
        compiler_params=_cparams(("arbitrary", "arbitrary")),
        name="proj",
    )(x2, sc, sh, w_bf)


def _rope(x, cos, sin_lo, sin_hi):
    return (x * cos + pltpu.roll(x, 96, axis=1) * sin_lo + pltpu.roll(x, 32, axis=1) * sin_hi)


def _attn_kernel(q_ref, k_ref, v_ref, cq_ref, slq_ref, shq_ref, ck_ref, slk_ref, shk_ref,
                 qw_ref, kw_ref, aw_ref, o_ref, kr_ref):
    @pl.when(pl.program_id(2) == 0)
    def _():
        k = _rms(k_ref[...].astype(F32)) * kw_ref[...]
        kr_ref[...] = _rope(k, ck_ref[...], slk_ref[...], shk_ref[...]).astype(BF16)

    scale = math.log2(math.e) / math.sqrt(HEAD_DIM)
    cq = cq_ref[...]
    slq = slq_ref[...]
    shq = shq_ref[...]
    heads = [slice(h * HEAD_DIM, (h + 1) * HEAD_DIM) for h in range(KV_GROUP)]
    qs = []
    for cols in heads:
        q = _rms(q_ref[:, cols].astype(F32)) * qw_ref[...]
        qs.append((_rope(q, cq, slq, shq) * scale).astype(BF16))
    scores = [lax.dot_general(q, kr_ref[...], (((1,), (1,)), ((), ())), preferred_element_type=F32)
              for q in qs]
    outs = []
    for s in scores:
        p = jnp.exp2(s - jnp.max(s, axis=-1, keepdims=True))
        l = jnp.sum(p, axis=-1, keepdims=True)
        outs.append(jnp.dot(p.astype(BF16), v_ref[...], preferred_element_type=F32) / l)
    for cols, o in zip(heads, outs):
        o_ref[:, cols] = (_rms(o) * aw_ref[:, cols]).astype(BF16)


def _attention(proj, cos, sin_lo, sin_hi, qw, kw, aw):
    nq = SEQ // ATTN_TQ
    gw = KV_GROUP * HEAD_DIM
    tab_q = pl.BlockSpec((ATTN_TQ, HEAD_DIM), lambda b, g, i: (i, 0))
    tab_k = pl.BlockSpec((SEQ, HEAD_DIM), lambda b, g, i: (0, 0))
    return pl.pallas_call(
        _attn_kernel,
        grid=(BATCH, N_KV_HEADS, nq),
        in_specs=[pl.BlockSpec((ATTN_TQ, gw), lambda b, g, i: (b * nq + i, g)),
                  pl.BlockSpec((SEQ, HEAD_DIM), lambda b, g, i: (b, COL_K + g)),
                  pl.BlockSpec((SEQ, HEAD_DIM), lambda b, g, i: (b, COL_V + g)),
                  tab_q, tab_q, tab_q, tab_k, tab_k, tab_k,
                  pl.BlockSpec((1, HEAD_DIM), lambda b, g, i: (0, 0)),
                  pl.BlockSpec((1, HEAD_DIM), lambda b, g, i: (0, 0)),
                  pl.BlockSpec((1, gw), lambda b, g, i: (0, g))],
        out_specs=pl.BlockSpec((ATTN_TQ, gw), lambda b, g, i: (b * nq + i, g)),
        out_shape=jax.ShapeDtypeStruct((N_TOK, ATTN_WIDTH), BF16),
        scratch_shapes=[pltpu.VMEM((SEQ, HEAD_DIM), BF16)],
        compiler_params=_cparams(("arbitrary", "arbitrary", "arbitrary")),
        name="attn",
    )(proj, proj, proj, cos, sin_lo, sin_hi, cos, sin_lo, sin_hi, qw, kw, aw)


def _hgrn_kernel(qr_ref, ff_ref, fb_ref, iv_ref, go_ref, lb_ref, nw_ref, o_ref, acc_ref, st_ref):
    C = HGRN_CHUNK
    nc = SEQ // C
    trips = nc // HGRN_UN
    row = lax.broadcasted_iota(jnp.int32, (C, C), 0)
    col = lax.broadcasted_iota(jnp.int32, (C, C), 1)
    keeps = (row >= col, row <= col)
    lasts = (C - 1, 0)
    f_refs = (ff_ref, fb_ref)

    nt_dims = (((1,), (1,)), ((), ()))
    tn_dims = (((0,), (0,)), ((), ()))
    st_ref[...] = jnp.zeros_like(st_ref)

    def trip(it, finish):
        chains = []
        for h in range(HGRN_HB):
            cols = slice(h * HEAD_DIM, (h + 1) * HEAD_DIM)
            for d in range(2):
                for u in range(HGRN_UN):
                    n = it * HGRN_UN + u
                    cidx = n if d == 0 else nc - 1 - n
                    chains.append(dict(h=h, d=d, cols=cols, rows=pl.ds(pl.multiple_of(cidx * C, C), C)))

        for ch in chains:
            d = ch["d"]
            lb = lb_ref[d, :, ch["cols"]]
            fg = lb + (1.0 - lb) * _sigmoid(f_refs[d][ch["rows"], ch["cols"]].astype(F32))
            ch["kk"] = 1.0 - fg
            lf = jnp.log(fg)
            lf_hi = lf.astype(BF16)
            lf_lo = (lf - lf_hi.astype(F32)).astype(BF16)
            tri = jnp.where(keeps[d], 1.0, 0.0).astype(BF16)
            ch["b"] = (jnp.dot(tri, lf_hi, preferred_element_type=F32)
                       + jnp.dot(tri, lf_lo, preferred_element_type=F32))
        for ch in chains:
            b = ch["b"]
            bl = b[lasts[ch["d"]]:lasts[ch["d"]] + 1, :]
            qx = qr_ref[ch["rows"], ch["cols"]].astype(F32)
            ch["qd"] = (qx * _sigmoid(qx) * jnp.exp(b)).astype(BF16)
            kd = (ch["kk"] * jnp.exp(-b)).astype(BF16)
            ku = (ch["kk"] * jnp.exp(bl - b)).astype(BF16)
            ch["v"] = iv_ref[ch["rows"], ch["cols"]]
            ch["decay"] = jnp.exp(bl)
            ch["sc"] = lax.dot_general(ch["qd"], kd, nt_dims, preferred_element_type=F32)
            ch["u_t"] = lax.dot_general(ch["v"], ku, tn_dims, preferred_element_type=F32)
        for h in range(HGRN_HB):
            for d in range(2):
                state = st_ref[2 * h + d]
                for ch in chains:
                    if ch["h"] == h and ch["d"] == d:
                        ch["state"] = state.astype(BF16)
                        state = state * ch["decay"] + ch["u_t"]
                st_ref[2 * h + d] = state
        for ch in chains:
            sc = jnp.where(keeps[ch["d"]], ch["sc"], 0.0).astype(BF16)
            ch["o"] = (jnp.dot(sc, ch["v"], preferred_element_type=F32)
                       + lax.dot_general(ch["qd"], ch["state"], nt_dims, preferred_element_type=F32))
        for ch in chains:
            rows, cols = ch["rows"], ch["cols"]
            if finish:
                o = _rms(acc_ref[rows, cols] + ch["o"]) * nw_ref[:, cols]
                g = go_ref[rows, cols].astype(F32)
                o_ref[rows, cols] = (o * (g * _sigmoid(g))).astype(BF16)
            else:
                acc_ref[rows, cols] = ch["o"]

    def first_half(it, carry):
        trip(it, False)
        return carry

    def second_half(it, carry):
        trip(it, True)
        return carry

    lax.fori_loop(0, trips // 2, first_half, 0)
    lax.fori_loop(trips // 2, trips, second_half, 0)


def _hgrn(proj, lb, nw):
    width = HGRN_HB * HEAD_DIM

    def col(c0):
        return pl.BlockSpec((SEQ, width), lambda b, h: (b, c0 // HGRN_HB + h))

    return pl.pallas_call(
        _hgrn_kernel,
        grid=(BATCH, N_HGRN_HEADS // HGRN_HB),
        in_specs=[col(COL_QR), col(COL_FF), col(COL_FB), col(COL_IN), col(COL_GO),
                  pl.BlockSpec((2, 1, width), lambda b, h: (0, 0, h)),
                  pl.BlockSpec((1, width), lambda b, h: (0, h))],
        out_specs=pl.BlockSpec((SEQ, width), lambda b, h: (b, h)),
        out_shape=jax.ShapeDtypeStruct((N_TOK, HGRN_WIDTH), BF16),
        scratch_shapes=[pltpu.VMEM((SEQ, width), F32),
                        pltpu.VMEM((2 * HGRN_HB, HEAD_DIM, HEAD_DIM), F32)],
        compiler_params=_cparams(("arbitrary", "arbitrary")),
        name="hgrn",
    )(proj, proj, proj, proj, proj, lb, nw)


def _mix_kernel(oa_ref, or_ref, wa_ref, wr_ref, x_ref, g1_ref, sc_ref, sh_ref, lg_ref, lbias_ref,
                wrt_ref, brt_ref, x1_ref, h2_ref, idx_ref, gate_ref, rank_ref, cnt_ref, carry_ref):
    i = pl.program_id(0)

    @pl.when(i == 0)
    def _():
        carry_ref[...] = jnp.zeros_like(carry_ref)

    y = jnp.dot(oa_ref[...], wa_ref[...], preferred_element_type=F32)
    y = y + jnp.dot(or_ref[...], wr_ref[...], preferred_element_type=F32)
    x1 = _layer_norm(DEEPNORM_ALPHA * x_ref[...] + g1_ref[...] * y) * lg_ref[...] + lbias_ref[...]
    x1_ref[...] = x1
    h2 = _layer_norm(x1) * (1.0 + sc_ref[...]) + sh_ref[...]
    h2_ref[...] = h2
    logits = jnp.dot(h2.astype(BF16), wrt_ref[...], preferred_element_type=F32) + brt_ref[...]

    tm = logits.shape[0]
    lane = lax.broadcasted_iota(jnp.int32, (tm, LANES), 1)
    neg = jnp.float32(-jnp.inf)
    work = logits
    vals, sels = [], []
    for _ in range(TOP_K):
        m = jnp.max(work, axis=-1, keepdims=True)
        sel = jnp.min(jnp.where(work == m, lane, LANES), axis=-1, keepdims=True)
        vals.append(m)
        sels.append(sel)
        work = jnp.where(lane == sel, neg, work)
    es = [jnp.exp(v - vals[0]) for v in vals]
    denom = es[0] + es[1] + es[2] + es[3]

    multi = jnp.zeros((tm, LANES), F32)
    for sel in sels:
        multi = multi + jnp.where(lane == sel, 1.0, 0.0)
    r = lax.broadcasted_iota(jnp.int32, (tm, tm), 0)
    c = lax.broadcasted_iota(jnp.int32, (tm, tm), 1)
    strict = jnp.where(r > c, 1.0, 0.0).astype(BF16)
    before = jnp.dot(strict, multi.astype(BF16), preferred_element_type=F32) + carry_ref[...]
    carry_ref[...] = carry_ref[...] + jnp.sum(multi, axis=0, keepdims=True)
    cnt_ref[...] = carry_ref[...]

    idx_out = jnp.zeros((tm, LANES), jnp.int32)
    gate_out = jnp.zeros((tm, LANES), F32)
    rank_out = jnp.zeros((tm, LANES), F32)
    for k in range(TOP_K):
        rk = jnp.sum(jnp.where(lane == sels[k], before, 0.0), axis=-1, keepdims=True)
        idx_out = jnp.where(lane == k, sels[k], idx_out)
        gate_out = jnp.where(lane == k, es[k] / denom, gate_out)
        rank_out = jnp.where(lane == k, rk, rank_out)
    idx_ref[...] = idx_out
    gate_ref[...] = gate_out
    rank_ref[...] = rank_out.astype(jnp.int32)


def _mix(o_attn, o_r, wa, wr, x2, g1, sc2, sh2, ln_g, ln_b, w_rt, b_rt):
    tiles_per_batch = SEQ // MIX_TM
    rows = lambda w: pl.BlockSpec((MIX_TM, w), lambda i: (i, 0))
    full = lambda a, b: pl.BlockSpec((a, b), lambda i: (0, 0))
    per_batch = pl.BlockSpec((None, 1, D_MODEL), lambda i: (i // tiles_per_batch, 0, 0))
    return pl.pallas_call(
        _mix_kernel,
        grid=(N_TOK // MIX_TM,),
        in_specs=[rows(ATTN_WIDTH), rows(HGRN_WIDTH), full(ATTN_WIDTH, D_MODEL), full(HGRN_WIDTH, D_MODEL),
                  rows(D_MODEL), per_batch, per_batch, per_batch, full(1, D_MODEL), full(1, D_MODEL),
                  full(D_MODEL, LANES), full(1, LANES)],
        out_specs=[rows(D_MODEL), rows(D_MODEL), rows(LANES), rows(LANES), rows(LANES), full(1, LANES)],
        out_shape=[jax.ShapeDtypeStruct((N_TOK, D_MODEL), F32),
                   jax.ShapeDtypeStruct((N_TOK, D_MODEL), F32),
                   jax.ShapeDtypeStruct((N_TOK, LANES), jnp.int32),
                   jax.ShapeDtypeStruct((N_TOK, LANES), F32),
                   jax.ShapeDtypeStruct((N_TOK, LANES), jnp.int32),
                   jax.ShapeDtypeStruct((1, LANES), F32)],
        scratch_shapes=[pltpu.VMEM((1, LANES), F32)],
        compiler_params=_cparams(("arbitrary",)),
        name="mix",
    )(o_attn, o_r, wa, wr, x2, g1, sc2, sh2, ln_g, ln_b, w_rt, b_rt)


def _wait_rows(n, make_copy):
    for bit in range(MOE_SB.bit_length()):
        @pl.when(((n >> bit) & 1) == 1)
        def _(bit=bit):
            make_copy(1 << bit).wait()


def _for_rows(lo, hi, fn):
    groups = (hi - lo) // MOE_ISSUE_UNROLL

    def group(q, carry):
        base = lo + q * MOE_ISSUE_UNROLL
        for u in range(MOE_ISSUE_UNROLL):
            fn(base + u)
        return carry

    def single(r, carry):
        fn(r)
        return carry

    lax.fori_loop(0, groups, group, 0)
    lax.fori_loop(lo + groups * MOE_ISSUE_UNROLL, hi, single, 0)


def _moe_kernel(se_ref, so_ref, sn_ref, na_ref, src_ref, dst_ref, h2_hbm, w1_hbm, b1_ref, w2_hbm, b2_ref,
                y_hbm, xg_ref, xb_ref, acc_ref, wf1g_ref, wf1l_ref, wf2_ref, wb1g_ref, wb1l_ref, wb2_ref,
                gsem, ssem, wsem):
    g = pl.program_id(0)
    nj = MOE_NJ
    n_assign = N_TOK * TOP_K
    n_slices = na_ref[0] * nj

    def weight_copies(t, s):
        e = se_ref[jnp.minimum(t // nj, MOE_G - 1)]
        col = pl.multiple_of((t % nj) * MOE_TH, MOE_TH)
        return (pltpu.make_async_copy(w1_hbm.at[e, :, pl.ds(col, MOE_TH)], wf1g_ref.at[s], wsem.at[s]),
                pltpu.make_async_copy(w1_hbm.at[e, :, pl.ds(D_EXPERT + col, MOE_TH)], wf1l_ref.at[s], wsem.at[s]),
                pltpu.make_async_copy(w2_hbm.at[e, pl.ds(col, MOE_TH), :], wf2_ref.at[s], wsem.at[s]))

    def cast_weights(s):
        wb1g_ref[s] = wf1g_ref[s].astype(BF16)
        wb1l_ref[s] = wf1l_ref[s].astype(BF16)
        wb2_ref[s] = wf2_ref[s].astype(BF16)

    def tiles_of(rows):
        return (rows + MOE_TMI - 1) // MOE_TMI

    n = sn_ref[g]
    nt = tiles_of(n)
    slot = g % 2
    g_next = jnp.minimum(g + 1, MOE_G - 1)
    n_next = jnp.where(g + 1 < MOE_G, sn_ref[g_next], 0)
    off_next = so_ref[g_next]
    g_prev = jnp.maximum(g - 1, 0)
    n_prev = jnp.where(g >= 1, sn_ref[g_prev], 0)
    off_prev = so_ref[g_prev]
    step_rows = MOE_Q_STEP + nt * MOE_Q_TILE
    eager = jnp.where(nt > 0, nj * step_rows, 0)

    def gather_copy(tok, r):
        return pltpu.make_async_copy(h2_hbm.at[pl.ds(tok, 1)], xg_ref.at[pl.ds(r, 1)], gsem)

    def scatter_copy(s, r, a):
        return pltpu.make_async_copy(acc_ref.at[s, pl.ds(r, 1)], y_hbm.at[pl.ds(a, 1)], ssem.at[s])

    def gather_start(r):
        gather_copy(src_ref[off_next + r], r).start()

    def scatter_start(r):
        scatter_copy(1 - slot, r, dst_ref[off_prev + r]).start()

    def eager_issue(first, count):
        for q in range(count):
            gather_start(first + q)
            scatter_start(first + q)

    def tile_rows(i):
        return pl.ds(pl.multiple_of(i * MOE_TMI, MOE_TMI), MOE_TMI)

    def begin():
        @pl.when(g == 0)
        def _():
            xg_ref[...] = jnp.zeros_like(xg_ref)
            acc_ref[...] = jnp.zeros_like(acc_ref)
            spare = pltpu.make_async_copy(acc_ref.at[0], y_hbm.at[pl.ds(n_assign, MOE_SB)], ssem.at[0])
            spare.start()
            spare.wait()
            _for_rows(0, n, lambda r: gather_copy(src_ref[so_ref[0] + r], r).start())
            for t in range(2):
                @pl.when(t < n_slices)
                def _(t=t):
                    for c in weight_copies(t, t):
                        c.start()

            @pl.when(n_slices > 0)
            def _():
                for c in weight_copies(0, 0):
                    c.wait()
                cast_weights(0)

        nt_prev = tiles_of(n_prev)
        eager_prev = jnp.where(nt_prev > 0, nj * (MOE_Q_STEP + nt_prev * MOE_Q_TILE), 0)
        n_prev2 = jnp.where(g >= 2, sn_ref[jnp.maximum(g - 2, 0)], 0)
        gathered = jnp.maximum(eager_prev, n)
        scattered = jnp.maximum(eager_prev, n_prev2)
        _wait_rows(gathered,
                   lambda k: pltpu.make_async_copy(h2_hbm.at[pl.ds(0, k)], xg_ref.at[pl.ds(0, k)], gsem))
        _wait_rows(scattered,
                   lambda k: pltpu.make_async_copy(acc_ref.at[slot, pl.ds(0, k)], y_hbm.at[pl.ds(0, k)],
                                                   ssem.at[slot]))

        def prep(i, carry):
            rows = tile_rows(i)
            xb_ref[rows, :] = xg_ref[rows, :].astype(BF16)
            acc_ref[slot, rows, :] = jnp.broadcast_to(b2_ref[...], (MOE_TMI, D_MODEL))
            return carry

        lax.fori_loop(0, nt, prep, 0)

    begin()

    def hidden_slice(j, ws):
        t = g * nj + j

        @pl.when(t + 1 < n_slices)
        def _():
            for c in weight_copies(t + 1, 1 - ws):
                c.wait()

        @pl.when(t + 2 < n_slices)
        def _():
            for c in weight_copies(t + 2, ws):
                c.start()

        b1g = b1_ref[pl.ds(j, 1), :]
        b1l = b1_ref[pl.ds(nj + j, 1), :]

        def tiles(first_tile, count, cast_next):
            first_row = j * step_rows + first_tile * MOE_Q_TILE
            if cast_next:
                eager_issue(first_row, MOE_Q_STEP + count * MOE_Q_TILE)
            else:
                eager_issue(first_row + MOE_Q_STEP, count * MOE_Q_TILE)
            rows = [tile_rows(first_tile + i) for i in range(count)]
            xs = [xb_ref[r, :] for r in rows]
            hid = [(jnp.dot(x, wb1g_ref[ws], preferred_element_type=F32) + b1g,
                    jnp.dot(x, wb1l_ref[ws], preferred_element_type=F32) + b1l) for x in xs]
            if cast_next:
                cast_weights(1 - ws)
            parts = []
            for hg, hl in hid:
                hg = jnp.minimum(hg, SWIGLU_LIMIT)
                hl = jnp.clip(hl, -SWIGLU_LIMIT, SWIGLU_LIMIT)
                act = hg * _sigmoid(SWIGLU_ALPHA * hg) * (hl + 1.0)
                parts.append(jnp.dot(act.astype(BF16), wb2_ref[ws], preferred_element_type=F32))
            for r, part in zip(rows, parts):
                acc_ref[slot, r, :] += part

        @pl.when(nt >= 2)
        def _():
            tiles(0, 2, True)

        @pl.when(nt >= 4)
        def _():
            tiles(2, 2, False)

        @pl.when(nt == 1)
        def _():
            tiles(0, 1, True)

        @pl.when(nt == 3)
        def _():
            tiles(2, 1, False)

    def slice_pair(jj, carry):
        for ws in range(2):
            hidden_slice(2 * jj + ws, ws)
        return carry

    @pl.when(nt > 0)
    def _():
        lax.fori_loop(0, nj // 2, slice_pair, 0)

    _for_rows(jnp.minimum(eager, n_next), n_next, gather_start)
    _for_rows(jnp.minimum(eager, n_prev), n_prev, scatter_start)

    @pl.when(g == MOE_G - 1)
    def _():
        _wait_rows(n_prev, lambda k: pltpu.make_async_copy(acc_ref.at[1 - slot, pl.ds(0, k)],
                                                            y_hbm.at[pl.ds(0, k)], ssem.at[1 - slot]))


def _moe(sb_expert, sb_off, sb_n, n_active, src_tok, dst_row, h2, w1, b1, w2, b2):
    grid_spec = pltpu.PrefetchScalarGridSpec(
        num_scalar_prefetch=6,
        grid=(MOE_G,),
        in_specs=[
            pl.BlockSpec(memory_space=pl.ANY),
            pl.BlockSpec(memory_space=pl.ANY),
            pl.BlockSpec((None, 2 * MOE_NJ, MOE_TH), lambda g, se, *_: (se[g], 0, 0)),
            pl.BlockSpec(memory_space=pl.ANY),
            pl.BlockSpec((None, 1, D_MODEL), lambda g, se, *_: (se[g], 0, 0)),
        ],
        out_specs=pl.BlockSpec(memory_space=pl.ANY),
        scratch_shapes=[pltpu.VMEM((MOE_SB, D_MODEL), F32),
                        pltpu.VMEM((MOE_SB, D_MODEL), BF16),
                        pltpu.VMEM((2, MOE_SB, D_MODEL), F32),
                        pltpu.VMEM((2, D_MODEL, MOE_TH), F32),
                        pltpu.VMEM((2, D_MODEL, MOE_TH), F32),
                        pltpu.VMEM((2, MOE_TH, D_MODEL), F32),
                        pltpu.VMEM((2, D_MODEL, MOE_TH), BF16),
                        pltpu.VMEM((2, D_MODEL, MOE_TH), BF16),
                        pltpu.VMEM((2, MOE_TH, D_MODEL), BF16),
                        pltpu.SemaphoreType.DMA(()),
                        pltpu.SemaphoreType.DMA((2,)),
                        pltpu.SemaphoreType.DMA((2,))],
    )
    return pl.pallas_call(
        _moe_kernel,
        grid_spec=grid_spec,
        out_shape=jax.ShapeDtypeStruct((N_TOK * TOP_K + MOE_SB, D_MODEL), F32),
        compiler_params=_cparams(("arbitrary",)),
        name="moe",
    )(sb_expert, sb_off, sb_n, n_active, src_tok, dst_row, h2, w1, b1, w2, b2)


def _final_kernel(x1_ref, y0_ref, y1_ref, y2_ref, y3_ref, gate_ref, g2_ref, lg_ref, lb_ref, o_ref):
    y = gate_ref[:, 0:1] * y0_ref[...]
    for k, y_ref in enumerate((y1_ref, y2_ref, y3_ref), start=1):
        y = y + gate_ref[:, k:k + 1] * y_ref[...]
    z = DEEPNORM_ALPHA * x1_ref[...] + g2_ref[...] * y
    o_ref[...] = _layer_norm(z) * lg_ref[...] + lb_ref[...]


def _final(x1, y4, gates, g2, ln_g, ln_b):
    tiles_per_batch = SEQ // FIN_TM
    tiles = N_TOK // FIN_TM
    rows = lambda w: pl.BlockSpec((FIN_TM, w), lambda i: (i, 0))
    plane = lambda k: pl.BlockSpec((FIN_TM, D_MODEL), lambda i: (k * tiles + i, 0))
    vec = pl.BlockSpec((1, D_MODEL), lambda i: (0, 0))
    return pl.pallas_call(
        _final_kernel,
        grid=(tiles,),
        in_specs=[rows(D_MODEL), plane(0), plane(1), plane(2), plane(3), rows(LANES),
                  pl.BlockSpec((None, 1, D_MODEL), lambda i: (i // tiles_per_batch, 0, 0)), vec, vec],
        out_specs=rows(D_MODEL),
        out_shape=jax.ShapeDtypeStruct((N_TOK, D_MODEL), F32),
        compiler_params=_cparams(("arbitrary",)),
        name="final",
    )(x1, y4, y4, y4, y4, gates, g2, ln_g, ln_b)


def _rope_tables():
    rows = SEQ // GRID_W
    t = np.arange(SEQ)
    row = (t // GRID_W - rows // 2).astype(np.float32)
    col = (t % GRID_W - GRID_W // 2).astype(np.float32)
    inv_freq = jnp.asarray(ROPE_THETA, F32) ** (-jnp.arange(0, ROPE_AXIS_DIM, 2, dtype=F32) / ROPE_AXIS_DIM)
    ang_row = jnp.asarray(row)[:, None] * inv_freq[None, :]
    ang_col = jnp.asarray(col)[:, None] * inv_freq[None, :]
    zeros = jnp.zeros_like(ang_row)
    cos = jnp.concatenate([jnp.cos(ang_row)] * 2 + [jnp.cos(ang_col)] * 2, axis=-1)
    sin_lo = jnp.concatenate([-jnp.sin(ang_row), zeros, -jnp.sin(ang_col), zeros], axis=-1)
    sin_hi = jnp.concatenate([zeros, jnp.sin(ang_row), zeros, jnp.sin(ang_col)], axis=-1)
    return cos, sin_lo, sin_hi


def _routing(top_i, rank, counts):
    counts = counts.astype(jnp.int32)
    start = jnp.cumsum(counts) - counts
    assign = jnp.arange(N_TOK * TOP_K, dtype=jnp.int32)
    dest = (start[top_i] + rank).reshape(-1)
    sorted_assign = jnp.zeros((N_TOK * TOP_K,), jnp.int32).at[dest].set(assign)
    nsb = (counts + MOE_SB - 1) // MOE_SB
    sb_end = jnp.cumsum(nsb)
    sb_start = sb_end - nsb
    g = jnp.arange(MOE_G, dtype=jnp.int32)
    active = g < sb_end[-1]
    e_of_g = jnp.minimum(jnp.sum(g[:, None] >= sb_end[None, :], axis=1), N_EXPERTS - 1).astype(jnp.int32)
    last_e = e_of_g[jnp.maximum(sb_end[-1] - 1, 0)]
    first_row = (g - sb_start[e_of_g]) * MOE_SB
    sb_n = jnp.where(active, jnp.clip(counts[e_of_g] - first_row, 0, MOE_SB), 0).astype(jnp.int32)
    sb_off = jnp.where(active, start[e_of_g] + first_row, 0).astype(jnp.int32)
    sb_expert = jnp.where(active, e_of_g, last_e).astype(jnp.int32)
    src_tok = sorted_assign // TOP_K
    dst_row = (sorted_assign % TOP_K) * N_TOK + src_tok
    src_tok = jnp.concatenate([src_tok, jnp.zeros((MOE_SB,), jnp.int32)])
    dst_row = jnp.concatenate([dst_row, N_TOK * TOP_K + jnp.arange(MOE_SB, dtype=jnp.int32)])
    n_active = sb_end[-1:].astype(jnp.int32)
    return sb_expert, sb_off, sb_n, n_active, src_tok, dst_row


def kernel(x, c, w_ada, b_ada, w_in, q_norm_w, k_norm_w, attn_norm_w, hgrn_lb, hgrn_norm_w, w_out, ln1_g, ln1_b, w_router, b_router, w_exp_in, b_exp_in, w_exp_out, b_exp_out, ln2_g, ln2_b):
    c_pad = jnp.zeros((8, D_MODEL), F32).at[:BATCH].set(c)
    cos, sin_lo, sin_hi = _rope_tables()
    x2 = x.reshape(N_TOK, D_MODEL)
    for l in range(DEPTH):
        mod = _ada(c_pad, w_ada[l], b_ada[l][None, :])[:BATCH]
        sh1, sc1, g1, sh2, sc2, g2 = [m.reshape(BATCH, 1, D_MODEL) for m in jnp.split(mod, 6, axis=-1)]

        proj = _proj(x2, sc1, sh1, w_in[l].astype(BF16))
        o_attn = _attention(proj, cos, sin_lo, sin_hi, q_norm_w[l][None, :], k_norm_w[l][None, :],
                            attn_norm_w[l][None, :])
        lb = jnp.cumsum(jax.nn.softmax(hgrn_lb.astype(F32), axis=1), axis=1)[:, l]
        o_r = _hgrn(proj, lb.reshape(2, 1, HGRN_WIDTH), hgrn_norm_w[l][None, :])

        w_o = w_out[l].astype(BF16)
        w_rt = jnp.zeros((D_MODEL, LANES), BF16).at[:, :N_EXPERTS].set(w_router[l].astype(BF16))
        b_rt = jnp.full((1, LANES), -1e30, F32).at[0, :N_EXPERTS].set(b_router[l])
        x1, h2, idx, gates, rank, counts = _mix(
            o_attn, o_r, w_o[:ATTN_WIDTH], w_o[ATTN_WIDTH:], x2, g1, sc2, sh2,
            ln1_g[l][None, :], ln1_b[l][None, :], w_rt, b_rt)
        sb_expert, sb_off, sb_n, n_active, src_tok, dst_row = _routing(idx[:, :TOP_K], rank[:, :TOP_K],
                                                                       counts[0, :N_EXPERTS])
        y4 = _moe(sb_expert, sb_off, sb_n, n_active, src_tok, dst_row, h2, w_exp_in[l],
                  b_exp_in[l].reshape(N_EXPERTS, 2 * MOE_NJ, MOE_TH), w_exp_out[l], b_exp_out[l][:, None, :])
        x2 = _final(x1, y4, gates, g2, ln2_g[l][None, :], ln2_b[l][None, :])
    return x2.reshape(BATCH, SEQ, D_MODEL)
```

```python
import functools
import math

import numpy as np
import jax
import jax.numpy as jnp
from jax import lax
from jax.experimental import pallas as pl
from jax.experimental.pallas import tpu as pltpu

F32 = jnp.float32
BF16 = jnp.bfloat16

D_MODEL = 2048
BATCH = 4
SEQ = 2048
DEPTH = 1
N_TOK = BATCH * SEQ
HEAD_DIM = 128
ATTN_WIDTH = 1024
N_Q_HEADS = 8
N_KV_HEADS = 2
KV_GROUP = 4
HGRN_WIDTH = 1024
N_HGRN_HEADS = 8
HGRN_CHUNK = 64
GRID_W = 64
ROPE_THETA = 10000.0
ROPE_AXIS_DIM = 64
N_EXPERTS = 32
TOP_K = 4
D_EXPERT = 2048
SWIGLU_LIMIT = 7.0
SWIGLU_ALPHA = 1.702
NORM_EPS = 1e-6
DEEPNORM_ALPHA = (2 * DEPTH) ** 0.25
PROJ_WIDTH = 6656
LANES = 128

COL_Q = 0
COL_K = 8
COL_V = 10
COL_QR = 12
COL_FF = 20
COL_FB = 28
COL_IN = 36
COL_GO = 44

VMEM_LIMIT = 56 * 1024 * 1024

ADA_TN = 1024
PROJ_TM = 1024
PROJ_TN = 1664
ATTN_TQ = 256
HGRN_HB = 2
HGRN_UN = 2
MIX_TM = 256
MOE_ISSUE_UNROLL = 8
MOE_SB = 1024
MOE_TMI = 256
MOE_TH = 256
MOE_NJ = D_EXPERT // MOE_TH
MOE_G = N_TOK * TOP_K // MOE_SB + N_EXPERTS + 1
MOE_Q_STEP = 0
MOE_Q_TILE = (MOE_SB // (D_EXPERT // MOE_TH) - MOE_Q_STEP) // (MOE_SB // MOE_TMI)
FIN_TM = 256


def _cparams(sem):
    return pltpu.CompilerParams(dimension_semantics=sem, vmem_limit_bytes=VMEM_LIMIT)


def _sigmoid(x):
    return 1.0 / (1.0 + jnp.exp(-x))


def _layer_norm(x):
    mu = jnp.mean(x, axis=-1, keepdims=True)
    xc = x - mu
    var = jnp.mean(xc * xc, axis=-1, keepdims=True)
    return xc * lax.rsqrt(var + NORM_EPS)


def _rms(x):
    return x * lax.rsqrt(jnp.mean(x * x, axis=-1, keepdims=True) + NORM_EPS)


def _ada_kernel(c_ref, w_ref, b_ref, o_ref):
    c = c_ref[...]
    ca = c * _sigmoid(c)
    o_ref[...] = jnp.dot(ca.astype(BF16), w_ref[...].astype(BF16),
                         preferred_element_type=F32) + b_ref[...]


def _ada(c_pad, w, b):
    n = w.shape[1]
    return pl.pallas_call(
        _ada_kernel,
        grid=(n // ADA_TN,),
        in_specs=[pl.BlockSpec((8, D_MODEL), lambda j: (0, 0)),
                  pl.BlockSpec((D_MODEL, ADA_TN), lambda j: (0, j)),
                  pl.BlockSpec((1, ADA_TN), lambda j: (0, j))],
        out_specs=pl.BlockSpec((8, ADA_TN), lambda j: (0, j)),
        out_shape=jax.ShapeDtypeStruct((8, n), F32),
        compiler_params=_cparams(("arbitrary",)),
        name="ada",
    )(c_pad, w, b)


def _proj_kernel(x_ref, sc_ref, sh_ref, w_ref, o_ref, h_ref):
    @pl.when(pl.program_id(1) == 0)
    def _():
        h = _layer_norm(x_ref[...]) * (1.0 + sc_ref[...]) + sh_ref[...]
        h_ref[...] = h.astype(BF16)

    o_ref[...] = jnp.dot(h_ref[...], w_ref[...], preferred_element_type=F32).astype(BF16)


def _proj(x2, sc, sh, w_bf):
    tiles_per_batch = SEQ // PROJ_TM
    return pl.pallas_call(
        _proj_kernel,
        grid=(N_TOK // PROJ_TM, PROJ_WIDTH // PROJ_TN),
        in_specs=[pl.BlockSpec((PROJ_TM, D_MODEL), lambda i, j: (i, 0)),
                  pl.BlockSpec((None, 1, D_MODEL), lambda i, j: (i // tiles_per_batch, 0, 0)),
                  pl.BlockSpec((None, 1, D_MODEL), lambda i, j: (i // tiles_per_batch, 0, 0)),
                  pl.BlockSpec((D_MODEL, PROJ_TN), lambda i, j: (0, j))],
        out_specs=pl.BlockSpec((PROJ_TM, PROJ_TN), lambda i, j: (i, j)),
        out_shape=jax.ShapeDtypeStruct((N_TOK, PROJ_WIDTH), BF16),
        scratch_shapes=[pltpu.VMEM((PROJ_TM, D_MODEL), BF16)],
        compiler_params=_cparams(("arbitrary", "arbitrary")),
        name="proj",
    )(x2, sc, sh, w_bf)


def _rope(x, cos, sin_lo, sin_hi):
    return (x * cos + pltpu.roll(x, 96, axis=1) * sin_lo + pltpu.roll(x, 32, axis=1) * sin_hi)


def _attn_kernel(q_ref, k_ref, v_ref, cq_ref, slq_ref, shq_ref, ck_ref, slk_ref, shk_ref,
                 qw_ref, kw_ref, aw_ref, o_ref, kr_ref, v1_ref):
    @pl.when(pl.program_id(2) == 0)
    def _():
        k = _rms(k_ref[...].astype(F32)) * kw_ref[...]
        kr_ref[...] = _rope(k, ck_ref[...], slk_ref[...], shk_ref[...]).astype(BF16)
        v1_ref[:, :HEAD_DIM] = v_ref[...]
        v1_ref[:, HEAD_DIM:] = jnp.ones((SEQ, HEAD_DIM), BF16)

    scale = math.log2(math.e) / math.sqrt(HEAD_DIM)
    cq = cq_ref[...]
    slq = slq_ref[...]
    shq = shq_ref[...]
    heads = [slice(h * HEAD_DIM, (h + 1) * HEAD_DIM) for h in range(KV_GROUP)]
    qs = []
    for cols in heads:
        q = _rms(q_ref[:, cols].astype(F32)) * qw_ref[...]
        qs.append((_rope(q, cq, slq, shq) * scale).astype(BF16))
    scores = [lax.dot_general(q, kr_ref[...], (((1,), (1,)), ((), ())), preferred_element_type=F32)
              for q in qs]
    outs = []
    for s in scores:
        p = jnp.exp2((s - jnp.max(s, axis=-1, keepdims=True)).astype(BF16))
        ov = jnp.dot(p, v1_ref[...], preferred_element_type=F32)
        outs.append(ov[:, :HEAD_DIM] / ov[:, HEAD_DIM:HEAD_DIM + 1])
    for cols, o in zip(heads, outs):
        o_ref[:, cols] = (_rms(o) * aw_ref[:, cols]).astype(BF16)


def _attention(proj, cos, sin_lo, sin_hi, qw, kw, aw):
    nq = SEQ // ATTN_TQ
    gw = KV_GROUP * HEAD_DIM
    tab_q = pl.BlockSpec((ATTN_TQ, HEAD_DIM), lambda b, g, i: (i, 0))
    tab_k = pl.BlockSpec((SEQ, HEAD_DIM), lambda b, g, i: (0, 0))
    return pl.pallas_call(
        _attn_kernel,
        grid=(BATCH, N_KV_HEADS, nq),
        in_specs=[pl.BlockSpec((ATTN_TQ, gw), lambda b, g, i: (b * nq + i, g)),
                  pl.BlockSpec((SEQ, HEAD_DIM), lambda b, g, i: (b, COL_K + g)),
                  pl.BlockSpec((SEQ, HEAD_DIM), lambda b, g, i: (b, COL_V + g)),
                  tab_q, tab_q, tab_q, tab_k, tab_k, tab_k,
                  pl.BlockSpec((1, HEAD_DIM), lambda b, g, i: (0, 0)),
                  pl.BlockSpec((1, HEAD_DIM), lambda b, g, i: (0, 0)),
                  pl.BlockSpec((1, gw), lambda b, g, i: (0, g))],
        out_specs=pl.BlockSpec((ATTN_TQ, gw), lambda b, g, i: (b * nq + i, g)),
        out_shape=jax.ShapeDtypeStruct((N_TOK, ATTN_WIDTH), BF16),
        scratch_shapes=[pltpu.VMEM((SEQ, HEAD_DIM), BF16), pltpu.VMEM((SEQ, 2 * HEAD_DIM), BF16)],
        compiler_params=_cparams(("arbitrary", "arbitrary", "arbitrary")),
        name="attn",
    )(proj, proj, proj, cos, sin_lo, sin_hi, cos, sin_lo, sin_hi, qw, kw, aw)


def _hgrn_kernel(qr_ref, ff_ref, fb_ref, iv_ref, go_ref, lb_ref, nw_ref, o_ref, acc_ref, st_ref):
    C = HGRN_CHUNK
    nc = SEQ // C
    trips = nc // HGRN_UN
    row = lax.broadcasted_iota(jnp.int32, (C, C), 0)
    col = lax.broadcasted_iota(jnp.int32, (C, C), 1)
    keeps = (row >= col, row <= col)
    lasts = (C - 1, 0)
    f_refs = (ff_ref, fb_ref)

    nt_dims = (((1,), (1,)), ((), ()))
    tn_dims = (((0,), (0,)), ((), ()))
    st_ref[...] = jnp.zeros_like(st_ref)

    def trip(it, finish):
        chains = []
        for h in range(HGRN_HB):
            cols = slice(h * HEAD_DIM, (h + 1) * HEAD_DIM)
            for d in range(2):
                for u in range(HGRN_UN):
                    n = it * HGRN_UN + u
                    cidx = n if d == 0 else nc - 1 - n
                    chains.append(dict(h=h, d=d, cols=cols, rows=pl.ds(pl.multiple_of(cidx * C, C), C)))

        for ch in chains:
            d = ch["d"]
            lb = lb_ref[d, :, ch["cols"]]
            fg = lb + (1.0 - lb) * _sigmoid(f_refs[d][ch["rows"], ch["cols"]].astype(F32))
            ch["kk"] = 1.0 - fg
            lf = jnp.log(fg)
            lf_hi = lf.astype(BF16)
            lf_lo = (lf - lf_hi.astype(F32)).astype(BF16)
            tri = jnp.where(keeps[d], 1.0, 0.0).astype(BF16)
            ch["b"] = (jnp.dot(tri, lf_hi, preferred_element_type=F32)
                       + jnp.dot(tri, lf_lo, preferred_element_type=F32))
        for ch in chains:
            b = ch["b"]
            bl = b[lasts[ch["d"]]:lasts[ch["d"]] + 1, :]
            qx = qr_ref[ch["rows"], ch["cols"]].astype(F32)
            ch["qd"] = (qx * _sigmoid(qx) * jnp.exp(b)).astype(BF16)
            kd = (ch["kk"] * jnp.exp(-b)).astype(BF16)
            ku = (ch["kk"] * jnp.exp(bl - b)).astype(BF16)
            ch["v"] = iv_ref[ch["rows"], ch["cols"]]
            ch["decay"] = jnp.exp(bl)
            ch["sc"] = lax.dot_general(ch["qd"], kd, nt_dims, preferred_element_type=F32)
            ch["u_t"] = lax.dot_general(ch["v"], ku, tn_dims, preferred_element_type=F32)
        for h in range(HGRN_HB):
            for d in range(2):
                state = st_ref[2 * h + d]
                for ch in chains:
                    if ch["h"] == h and ch["d"] == d:
                        ch["state"] = state.astype(BF16)
                        state = state * ch["decay"] + ch["u_t"]
                st_ref[2 * h + d] = state
        for ch in chains:
            sc = jnp.where(keeps[ch["d"]], ch["sc"], 0.0).astype(BF16)
            ch["o"] = (jnp.dot(sc, ch["v"], preferred_element_type=F32)
                       + lax.dot_general(ch["qd"], ch["state"], nt_dims, preferred_element_type=F32))
        for ch in chains:
            rows, cols = ch["rows"], ch["cols"]
            if finish:
                o = _rms(acc_ref[rows, cols] + ch["o"]) * nw_ref[:, cols]
                g = go_ref[rows, cols].astype(F32)
                o_ref[rows, cols] = (o * (g * _sigmoid(g))).astype(BF16)
            else:
                acc_ref[rows, cols] = ch["o"]

    def first_half(it, carry):
        trip(it, False)
        return carry

    def second_half(it, carry):
        trip(it, True)
        return carry

    lax.fori_loop(0, trips // 2, first_half, 0)
    lax.fori_loop(trips // 2, trips, second_half, 0)


def _hgrn(proj, lb, nw):
    width = HGRN_HB * HEAD_DIM

    def col(c0):
        return pl.BlockSpec((SEQ, width), lambda b, h: (b, c0 // HGRN_HB + h))

    return pl.pallas_call(
        _hgrn_kernel,
        grid=(BATCH, N_HGRN_HEADS // HGRN_HB),
        in_specs=[col(COL_QR), col(COL_FF), col(COL_FB), col(COL_IN), col(COL_GO),
                  pl.BlockSpec((2, 1, width), lambda b, h: (0, 0, h)),
                  pl.BlockSpec((1, width), lambda b, h: (0, h))],
        out_specs=pl.BlockSpec((SEQ, width), lambda b, h: (b, h)),
        out_shape=jax.ShapeDtypeStruct((N_TOK, HGRN_WIDTH), BF16),
        scratch_shapes=[pltpu.VMEM((SEQ, width), F32),
                        pltpu.VMEM((2 * HGRN_HB, HEAD_DIM, HEAD_DIM), F32)],
        compiler_params=_cparams(("arbitrary", "arbitrary")),
        name="hgrn",
    )(proj, proj, proj, proj, proj, lb, nw)


def _mix_kernel(oa_ref, or_ref, wa_ref, wr_ref, x_ref, g1_ref, sc_ref, sh_ref, lg_ref, lbias_ref,
                wrt_ref, brt_ref, x1_ref, h2_ref, idx_ref, gate_ref, rank_ref, cnt_ref, carry_ref):
    i = pl.program_id(0)

    @pl.when(i == 0)
    def _():
        carry_ref[...] = jnp.zeros_like(carry_ref)

    y = jnp.dot(oa_ref[...], wa_ref[...], preferred_element_type=F32)
    y = y + jnp.dot(or_ref[...], wr_ref[...], preferred_element_type=F32)
    x1 = _layer_norm(DEEPNORM_ALPHA * x_ref[...] + g1_ref[...] * y) * lg_ref[...] + lbias_ref[...]
    x1_ref[...] = x1
    h2 = _layer_norm(x1) * (1.0 + sc_ref[...]) + sh_ref[...]
    h2_ref[...] = h2
    logits = jnp.dot(h2.astype(BF16), wrt_ref[...], preferred_element_type=F32) + brt_ref[...]

    tm = logits.shape[0]
    lane = lax.broadcasted_iota(jnp.int32, (tm, LANES), 1)
    neg = jnp.float32(-jnp.inf)
    work = logits
    vals, sels = [], []
    for _ in range(TOP_K):
        m = jnp.max(work, axis=-1, keepdims=True)
        sel = jnp.min(jnp.where(work == m, lane, LANES), axis=-1, keepdims=True)
        vals.append(m)
        sels.append(sel)
        work = jnp.where(lane == sel, neg, work)
    es = [jnp.exp(v - vals[0]) for v in vals]
    denom = es[0] + es[1] + es[2] + es[3]

    multi = jnp.zeros((tm, LANES), F32)
    for sel in sels:
        multi = multi + jnp.where(lane == sel, 1.0, 0.0)
    r = lax.broadcasted_iota(jnp.int32, (tm, tm), 0)
    c = lax.broadcasted_iota(jnp.int32, (tm, tm), 1)
    strict = jnp.where(r > c, 1.0, 0.0).astype(BF16)
    before = jnp.dot(strict, multi.astype(BF16), preferred_element_type=F32) + carry_ref[...]
    carry_ref[...] = carry_ref[...] + jnp.sum(multi, axis=0, keepdims=True)
    cnt_ref[...] = carry_ref[...]

    idx_out = jnp.zeros((tm, LANES), jnp.int32)
    gate_out = jnp.zeros((tm, LANES), F32)
    rank_out = jnp.zeros((tm, LANES), F32)
    for k in range(TOP_K):
        rk = jnp.sum(jnp.where(lane == sels[k], before, 0.0), axis=-1, keepdims=True)
        idx_out = jnp.where(lane == k, sels[k], idx_out)
        gate_out = jnp.where(lane == k, es[k] / denom, gate_out)
        rank_out = jnp.where(lane == k, rk, rank_out)
    idx_ref[...] = idx_out
    gate_ref[...] = gate_out
    rank_ref[...] = rank_out.astype(jnp.int32)


def _mix(o_attn, o_r, wa, wr, x2, g1, sc2, sh2, ln_g, ln_b, w_rt, b_rt):
    tiles_per_batch = SEQ // MIX_TM
    rows = lambda w: pl.BlockSpec((MIX_TM, w), lambda i: (i, 0))
    full = lambda a, b: pl.BlockSpec((a, b), lambda i: (0, 0))
    per_batch = pl.BlockSpec((None, 1, D_MODEL), lambda i: (i // tiles_per_batch, 0, 0))
    return pl.pallas_call(
        _mix_kernel,
        grid=(N_TOK // MIX_TM,),
        in_specs=[rows(ATTN_WIDTH), rows(HGRN_WIDTH), full(ATTN_WIDTH, D_MODEL), full(HGRN_WIDTH, D_MODEL),
                  rows(D_MODEL), per_batch, per_batch, per_batch, full(1, D_MODEL), full(1, D_MODEL),
                  full(D_MODEL, LANES), full(1, LANES)],
        out_specs=[rows(D_MODEL), rows(D_MODEL), rows(LANES), rows(LANES), rows(LANES), full(1, LANES)],
        out_shape=[jax.ShapeDtypeStruct((N_TOK, D_MODEL), F32),
                   jax.ShapeDtypeStruct((N_TOK, D_MODEL), F32),
                   jax.ShapeDtypeStruct((N_TOK, LANES), jnp.int32),
                   jax.ShapeDtypeStruct((N_TOK, LANES), F32),
                   jax.ShapeDtypeStruct((N_TOK, LANES), jnp.int32),
                   jax.ShapeDtypeStruct((1, LANES), F32)],
        scratch_shapes=[pltpu.VMEM((1, LANES), F32)],
        compiler_params=_cparams(("arbitrary",)),
        name="mix",
    )(o_attn, o_r, wa, wr, x2, g1, sc2, sh2, ln_g, ln_b, w_rt, b_rt)


def _wait_rows(n, make_copy):
    for bit in range(MOE_SB.bit_length()):
        @pl.when(((n >> bit) & 1) == 1)
        def _(bit=bit):
            make_copy(1 << bit).wait()


def _for_rows(lo, hi, fn):
    groups = (hi - lo) // MOE_ISSUE_UNROLL

    def group(q, carry):
        base = lo + q * MOE_ISSUE_UNROLL
        for u in range(MOE_ISSUE_UNROLL):
            fn(base + u)
        return carry

    def single(r, carry):
        fn(r)
        return carry

    lax.fori_loop(0, groups, group, 0)
    lax.fori_loop(lo + groups * MOE_ISSUE_UNROLL, hi, single, 0)


def _moe_kernel(se_ref, so_ref, sn_ref, na_ref, src_ref, dst_ref, h2_hbm, w1_hbm, b1_ref, w2_hbm, b2_ref,
                y_hbm, xg_ref, xb_ref, acc_ref, wf1g_ref, wf1l_ref, wf2_ref, wb1g_ref, wb1l_ref, wb2_ref,
                gsem, ssem, wsem):
    g = pl.program_id(0)
    nj = MOE_NJ
    n_assign = N_TOK * TOP_K
    n_slices = na_ref[0] * nj

    def weight_copies(t, s):
        e = se_ref[jnp.minimum(t // nj, MOE_G - 1)]
        col = pl.multiple_of((t % nj) * MOE_TH, MOE_TH)
        return (pltpu.make_async_copy(w1_hbm.at[e, :, pl.ds(col, MOE_TH)], wf1g_ref.at[s], wsem.at[s]),
                pltpu.make_async_copy(w1_hbm.at[e, :, pl.ds(D_EXPERT + col, MOE_TH)], wf1l_ref.at[s], wsem.at[s]),
                pltpu.make_async_copy(w2_hbm.at[e, pl.ds(col, MOE_TH), :], wf2_ref.at[s], wsem.at[s]))

    def cast_weights(s):
        wb1g_ref[s] = wf1g_ref[s].astype(BF16)
        wb1l_ref[s] = wf1l_ref[s].astype(BF16)
        wb2_ref[s] = wf2_ref[s].astype(BF16)

    def tiles_of(rows):
        return (rows + MOE_TMI - 1) // MOE_TMI

    n = sn_ref[g]
    nt = tiles_of(n)
    slot = g % 2
    g_next = jnp.minimum(g + 1, MOE_G - 1)
    n_next = jnp.where(g + 1 < MOE_G, sn_ref[g_next], 0)
    off_next = so_ref[g_next]
    g_prev = jnp.maximum(g - 1, 0)
    n_prev = jnp.where(g >= 1, sn_ref[g_prev], 0)
    off_prev = so_ref[g_prev]
    step_rows = MOE_Q_STEP + nt * MOE_Q_TILE
    eager = jnp.where(nt > 0, nj * step_rows, 0)

    def gather_copy(tok, r):
        return pltpu.make_async_copy(h2_hbm.at[pl.ds(tok, 1)], xg_ref.at[pl.ds(r, 1)], gsem)

    def scatter_copy(s, r, a):
        return pltpu.make_async_copy(acc_ref.at[s, pl.ds(r, 1)], y_hbm.at[pl.ds(a, 1)], ssem.at[s])

    def gather_start(r):
        gather_copy(src_ref[off_next + r], r).start()

    def scatter_start(r):
        scatter_copy(1 - slot, r, dst_ref[off_prev + r]).start()

    def eager_issue(first, count):
        for q in range(count):
            gather_start(first + q)
            scatter_start(first + q)

    def tile_rows(i):
        return pl.ds(pl.multiple_of(i * MOE_TMI, MOE_TMI), MOE_TMI)

    def begin():
        @pl.when(g == 0)
        def _():
            xg_ref[...] = jnp.zeros_like(xg_ref)
            acc_ref[...] = jnp.zeros_like(acc_ref)
            spare = pltpu.make_async_copy(acc_ref.at[0], y_hbm.at[pl.ds(n_assign, MOE_SB)], ssem.at[0])
            spare.start()
            spare.wait()
            _for_rows(0, n, lambda r: gather_copy(src_ref[so_ref[0] + r], r).start())
            for t in range(2):
                @pl.when(t < n_slices)
                def _(t=t):
                    for c in weight_copies(t, t):
                        c.start()

            @pl.when(n_slices > 0)
            def _():
                for c in weight_copies(0, 0):
                    c.wait()
                cast_weights(0)

        nt_prev = tiles_of(n_prev)
        eager_prev = jnp.where(nt_prev > 0, nj * (MOE_Q_STEP + nt_prev * MOE_Q_TILE), 0)
        n_prev2 = jnp.where(g >= 2, sn_ref[jnp.maximum(g - 2, 0)], 0)
        gathered = jnp.maximum(eager_prev, n)
        scattered = jnp.maximum(eager_prev, n_prev2)
        _wait_rows(gathered,
                   lambda k: pltpu.make_async_copy(h2_hbm.at[pl.ds(0, k)], xg_ref.at[pl.ds(0, k)], gsem))
        _wait_rows(scattered,
                   lambda k: pltpu.make_async_copy(acc_ref.at[slot, pl.ds(0, k)], y_hbm.at[pl.ds(0, k)],
                                                   ssem.at[slot]))

        def prep(i, carry):
            rows = tile_rows(i)
            xb_ref[rows, :] = xg_ref[rows, :].astype(BF16)
            acc_ref[slot, rows, :] = jnp.broadcast_to(b2_ref[...], (MOE_TMI, D_MODEL))
            return carry

        lax.fori_loop(0, nt, prep, 0)

    begin()

    def hidden_slice(j, ws):
        t = g * nj + j

        @pl.when(t + 1 < n_slices)
        def _():
            for c in weight_copies(t + 1, 1 - ws):
                c.wait()

        @pl.when(t + 2 < n_slices)
        def _():
            for c in weight_copies(t + 2, ws):
                c.start()

        b1g = b1_ref[pl.ds(j, 1), :]
        b1l = b1_ref[pl.ds(nj + j, 1), :]

        def tiles(first_tile, count, cast_next):
            first_row = j * step_rows + first_tile * MOE_Q_TILE
            if cast_next:
                eager_issue(first_row, MOE_Q_STEP + count * MOE_Q_TILE)
            else:
                eager_issue(first_row + MOE_Q_STEP, count * MOE_Q_TILE)
            rows = [tile_rows(first_tile + i) for i in range(count)]
            xs = [xb_ref[r, :] for r in rows]
            hid = [(jnp.dot(x, wb1g_ref[ws], preferred_element_type=F32) + b1g,
                    jnp.dot(x, wb1l_ref[ws], preferred_element_type=F32) + b1l) for x in xs]
            if cast_next:
                cast_weights(1 - ws)
            parts = []
            for hg, hl in hid:
                hg = jnp.minimum(hg, SWIGLU_LIMIT)
                hl = jnp.clip(hl, -SWIGLU_LIMIT, SWIGLU_LIMIT)
                act = hg * _sigmoid(SWIGLU_ALPHA * hg) * (hl + 1.0)
                parts.append(jnp.dot(act.astype(BF16), wb2_ref[ws], preferred_element_type=F32))
            for r, part in zip(rows, parts):
                acc_ref[slot, r, :] += part

        @pl.when(nt >= 2)
        def _():
            tiles(0, 2, True)

        @pl.when(nt >= 4)
        def _():
            tiles(2, 2, False)

        @pl.when(nt == 1)
        def _():
            tiles(0, 1, True)

        @pl.when(nt == 3)
        def _():
            tiles(2, 1, False)

    def slice_pair(jj, carry):
        for ws in range(2):
            hidden_slice(2 * jj + ws, ws)
        return carry

    @pl.when(nt > 0)
    def _():
        lax.fori_loop(0, nj // 2, slice_pair, 0)

    _for_rows(jnp.minimum(eager, n_next), n_next, gather_start)
    _for_rows(jnp.minimum(eager, n_prev), n_prev, scatter_start)

    @pl.when(g == MOE_G - 1)
    def _():
        _wait_rows(n_prev, lambda k: pltpu.make_async_copy(acc_ref.at[1 - slot, pl.ds(0, k)],
                                                            y_hbm.at[pl.ds(0, k)], ssem.at[1 - slot]))


def _moe(sb_expert, sb_off, sb_n, n_active, src_tok, dst_row, h2, w1, b1, w2, b2):
    grid_spec = pltpu.PrefetchScalarGridSpec(
        num_scalar_prefetch=6,
        grid=(MOE_G,),
        in_specs=[
            pl.BlockSpec(memory_space=pl.ANY),
            pl.BlockSpec(memory_space=pl.ANY),
            pl.BlockSpec((None, 2 * MOE_NJ, MOE_TH), lambda g, se, *_: (se[g], 0, 0)),
            pl.BlockSpec(memory_space=pl.ANY),
            pl.BlockSpec((None, 1, D_MODEL), lambda g, se, *_: (se[g], 0, 0)),
        ],
        out_specs=pl.BlockSpec(memory_space=pl.ANY),
        scratch_shapes=[pltpu.VMEM((MOE_SB, D_MODEL), F32),
                        pltpu.VMEM((MOE_SB, D_MODEL), BF16),
                        pltpu.VMEM((2, MOE_SB, D_MODEL), F32),
                        pltpu.VMEM((2, D_MODEL, MOE_TH), F32),
                        pltpu.VMEM((2, D_MODEL, MOE_TH), F32),
                        pltpu.VMEM((2, MOE_TH, D_MODEL), F32),
                        pltpu.VMEM((2, D_MODEL, MOE_TH), BF16),
                        pltpu.VMEM((2, D_MODEL, MOE_TH), BF16),
                        pltpu.VMEM((2, MOE_TH, D_MODEL), BF16),
                        pltpu.SemaphoreType.DMA(()),
                        pltpu.SemaphoreType.DMA((2,)),
                        pltpu.SemaphoreType.DMA((2,))],
    )
    return pl.pallas_call(
        _moe_kernel,
        grid_spec=grid_spec,
        out_shape=jax.ShapeDtypeStruct((N_TOK * TOP_K + MOE_SB, D_MODEL), F32),
        compiler_params=_cparams(("arbitrary",)),
        name="moe",
    )(sb_expert, sb_off, sb_n, n_active, src_tok, dst_row, h2, w1, b1, w2, b2)


def _final_kernel(x1_ref, y0_ref, y1_ref, y2_ref, y3_ref, gate_ref, g2_ref, lg_ref, lb_ref, o_ref):
    y = gate_ref[:, 0:1] * y0_ref[...]
    for k, y_ref in enumerate((y1_ref, y2_ref, y3_ref), start=1):
        y = y + gate_ref[:, k:k + 1] * y_ref[...]
    z = DEEPNORM_ALPHA * x1_ref[...] + g2_ref[...] * y
    o_ref[...] = _layer_norm(z) * lg_ref[...] + lb_ref[...]


def _final(x1, y4, gates, g2, ln_g, ln_b):
    tiles_per_batch = SEQ // FIN_TM
    tiles = N_TOK // FIN_TM
    rows = lambda w: pl.BlockSpec((FIN_TM, w), lambda i: (i, 0))
    plane = lambda k: pl.BlockSpec((FIN_TM, D_MODEL), lambda i: (k * tiles + i, 0))
    vec = pl.BlockSpec((1, D_MODEL), lambda i: (0, 0))
    return pl.pallas_call(
        _final_kernel,
        grid=(tiles,),
        in_specs=[rows(D_MODEL), plane(0), plane(1), plane(2), plane(3), rows(LANES),
                  pl.BlockSpec((None, 1, D_MODEL), lambda i: (i // tiles_per_batch, 0, 0)), vec, vec],
        out_specs=rows(D_MODEL),
        out_shape=jax.ShapeDtypeStruct((N_TOK, D_MODEL), F32),
        compiler_params=_cparams(("arbitrary",)),
        name="final",
    )(x1, y4, y4, y4, y4, gates, g2, ln_g, ln_b)


def _rope_tables():
    rows = SEQ // GRID_W
    t = np.arange(SEQ)
    row = (t // GRID_W - rows // 2).astype(np.float32)
    col = (t % GRID_W - GRID_W // 2).astype(np.float32)
    inv_freq = jnp.asarray(ROPE_THETA, F32) ** (-jnp.arange(0, ROPE_AXIS_DIM, 2, dtype=F32) / ROPE_AXIS_DIM)
    ang_row = jnp.asarray(row)[:, None] * inv_freq[None, :]
    ang_col = jnp.asarray(col)[:, None] * inv_freq[None, :]
    zeros = jnp.zeros_like(ang_row)
    cos = jnp.concatenate([jnp.cos(ang_row)] * 2 + [jnp.cos(ang_col)] * 2, axis=-1)
    sin_lo = jnp.concatenate([-jnp.sin(ang_row), zeros, -jnp.sin(ang_col), zeros], axis=-1)
    sin_hi = jnp.concatenate([zeros, jnp.sin(ang_row), zeros, jnp.sin(ang_col)], axis=-1)
    return cos, sin_lo, sin_hi


def _routing(top_i, rank, counts):
    counts = counts.astype(jnp.int32)
    nsb = (counts + MOE_SB - 1) // MOE_SB
    sb_end = jnp.cumsum(nsb)
    sb_start = sb_end - nsb
    g = jnp.arange(MOE_G, dtype=jnp.int32)
    active = g < sb_end[-1]
    e_of_g = jnp.minimum(jnp.sum(g[:, None] >= sb_end[None, :], axis=1), N_EXPERTS - 1).astype(jnp.int32)
    first_row = (g - sb_start[e_of_g]) * MOE_SB
    n_of_g = jnp.where(active, jnp.clip(counts[e_of_g] - first_row, 0, MOE_SB), 0).astype(jnp.int32)
    order = jnp.argsort(-n_of_g, stable=True).astype(jnp.int32)
    place = jnp.zeros((MOE_G,), jnp.int32).at[order].set(g)
    sb_n = n_of_g[order]
    sb_off = (jnp.cumsum(sb_n) - sb_n).astype(jnp.int32)
    last_e = e_of_g[order[jnp.maximum(sb_end[-1] - 1, 0)]]
    sb_expert = jnp.where(sb_n > 0, e_of_g[order], last_e).astype(jnp.int32)
    assign = jnp.arange(N_TOK * TOP_K, dtype=jnp.int32)
    dest = (sb_off[place[sb_start[top_i] + rank // MOE_SB]] + rank % MOE_SB).reshape(-1)
    sorted_assign = jnp.zeros((N_TOK * TOP_K,), jnp.int32).at[dest].set(assign)
    src_tok = sorted_assign // TOP_K
    dst_row = (sorted_assign % TOP_K) * N_TOK + src_tok
    src_tok = jnp.concatenate([src_tok, jnp.zeros((MOE_SB,), jnp.int32)])
    dst_row = jnp.concatenate([dst_row, N_TOK * TOP_K + jnp.arange(MOE_SB, dtype=jnp.int32)])
    n_active = sb_end[-1:].astype(jnp.int32)
    return sb_expert, sb_off, sb_n, n_active, src_tok, dst_row


def kernel(x, c, w_ada, b_ada, w_in, q_norm_w, k_norm_w, attn_norm_w, hgrn_lb, hgrn_norm_w, w_out, ln1_g, ln1_b, w_router, b_router, w_exp_in, b_exp_in, w_exp_out, b_exp_out, ln2_g, ln2_b):
    c_pad = jnp.zeros((8, D_MODEL), F32).at[:BATCH].set(c)
    cos, sin_lo, sin_hi = _rope_tables()
    x2 = x.reshape(N_TOK, D_MODEL)
    for l in range(DEPTH):
        mod = _ada(c_pad, w_ada[l], b_ada[l][None, :])[:BATCH]
        sh1, sc1, g1, sh2, sc2, g2 = [m.reshape(BATCH, 1, D_MODEL) for m in jnp.split(mod, 6, axis=-1)]

        proj = _proj(x2, sc1, sh1, w_in[l].astype(BF16))
        o_attn = _attention(proj, cos, sin_lo, sin_hi, q_norm_w[l][None, :], k_norm_w[l][None, :],
                            attn_norm_w[l][None, :])
        lb = jnp.cumsum(jax.nn.softmax(hgrn_lb.astype(F32), axis=1), axis=1)[:, l]
        o_r = _hgrn(proj, lb.reshape(2, 1, HGRN_WIDTH), hgrn_norm_w[l][None, :])

        w_o = w_out[l].astype(BF16)
        w_rt = jnp.zeros((D_MODEL, LANES), BF16).at[:, :N_EXPERTS].set(w_router[l].astype(BF16))
        b_rt = jnp.full((1, LANES), -1e30, F32).at[0, :N_EXPERTS].set(b_router[l])
        x1, h2, idx, gates, rank, counts = _mix(
            o_attn, o_r, w_o[:ATTN_WIDTH], w_o[ATTN_WIDTH:], x2, g1, sc2, sh2,
            ln1_g[l][None, :], ln1_b[l][None, :], w_rt, b_rt)
        sb_expert, sb_off, sb_n, n_active, src_tok, dst_row = _routing(idx[:, :TOP_K], rank[:, :TOP_K],
                                                                       counts[0, :N_EXPERTS])
        y4 = _moe(sb_expert, sb_off, sb_n, n_active, src_tok, dst_row, h2, w_exp_in[l],
                  b_exp_in[l].reshape(N_EXPERTS, 2 * MOE_NJ, MOE_TH), w_exp_out[l], b_exp_out[l][:, None, :])
        x2 = _final(x1, y4, gates, g2, ln2_g[l][None, :], ln2_b[l][None, :])
    return x2.reshape(BATCH, SEQ, D_MODEL)
```

```python
import functools
import math

import numpy as np
import jax
import jax.numpy as jnp
from jax import lax
from jax.experimental import pallas as pl
from jax.experimental.pallas import tpu as pltpu

F32 = jnp.float32
BF16 = jnp.bfloat16

D_MODEL = 2048
BATCH = 4
SEQ = 2048
DEPTH = 1
N_TOK = BATCH * SEQ
HEAD_DIM = 128
ATTN_WIDTH = 1024
N_Q_HEADS = 8
N_KV_HEADS = 2
KV_GROUP = 4
HGRN_WIDTH = 1024
N_HGRN_HEADS = 8
HGRN_CHUNK = 64
GRID_W = 64
ROPE_THETA = 10000.0
ROPE_AXIS_DIM = 64
N_EXPERTS = 32
TOP_K = 4
D_EXPERT = 2048
SWIGLU_LIMIT = 7.0
SWIGLU_ALPHA = 1.702
NORM_EPS = 1e-6
DEEPNORM_ALPHA = (2 * DEPTH) ** 0.25
PROJ_WIDTH = 6656
LANES = 128

COL_Q = 0
COL_K = 8
COL_V = 10
COL_QR = 12
COL_FF = 20
COL_FB = 28
COL_IN = 36
COL_GO = 44

VMEM_LIMIT = 56 * 1024 * 1024

ADA_TN = 1024
PROJ_TM = 1024
PROJ_TN = 1664
ATTN_TQ = 256
HGRN_HB = 4
HGRN_UN = 2
MIX_TM = 512
MIX_SUB = 256
MOE_ISSUE_UNROLL = 8
MOE_SB = 1024
MOE_TMI = 256
MOE_TH = 256
MOE_NJ = D_EXPERT // MOE_TH
MOE_G = N_TOK * TOP_K // MOE_SB + N_EXPERTS + 1
MOE_Q_STEP = 0
MOE_Q_TILE = (MOE_SB // (D_EXPERT // MOE_TH) - MOE_Q_STEP) // (MOE_SB // MOE_TMI)
FIN_TM = 256


def _cparams(sem):
    return pltpu.CompilerParams(dimension_semantics=sem, vmem_limit_bytes=VMEM_LIMIT)


def _sigmoid(x):
    return 1.0 / (1.0 + jnp.exp(-x))


def _layer_norm(x):
    mu = jnp.mean(x, axis=-1, keepdims=True)
    xc = x - mu
    var = jnp.mean(xc * xc, axis=-1, keepdims=True)
    return xc * lax.rsqrt(var + NORM_EPS)


def _rms(x):
    return x * lax.rsqrt(jnp.mean(x * x, axis=-1, keepdims=True) + NORM_EPS)


def _ada_kernel(c_ref, w_ref, b_ref, o_ref):
    c = c_ref[...]
    ca = c * _sigmoid(c)
    o_ref[...] = jnp.dot(ca.astype(BF16), w_ref[...].astype(BF16),
                         preferred_element_type=F32) + b_ref[...]


def _ada(c_pad, w, b):
    n = w.shape[1]
    return pl.pallas_call(
        _ada_kernel,
        grid=(n // ADA_TN,),
        in_specs=[pl.BlockSpec((8, D_MODEL), lambda j: (0, 0)),
                  pl.BlockSpec((D_MODEL, ADA_TN), lambda j: (0, j)),
                  pl.BlockSpec((1, ADA_TN), lambda j: (0, j))],
        out_specs=pl.BlockSpec((8, ADA_TN), lambda j: (0, j)),
        out_shape=jax.ShapeDtypeStruct((8, n), F32),
        compiler_params=_cparams(("arbitrary",)),
        name="ada",
    )(c_pad, w, b)


def _proj_kernel(x_ref, sc_ref, sh_ref, w_ref, o_ref, h_ref):
    @pl.when(pl.program_id(1) == 0)
    def _():
        h = _layer_norm(x_ref[...]) * (1.0 + sc_ref[...]) + sh_ref[...]
        h_ref[...] = h.astype(BF16)

    o_ref[...] = jnp.dot(h_ref[...], w_ref[...], preferred_element_type=F32).astype(BF16)


def _proj(x2, sc, sh, w_bf):
    tiles_per_batch = SEQ // PROJ_TM
    return pl.pallas_call(
        _proj_kernel,
        grid=(N_TOK // PROJ_TM, PROJ_WIDTH // PROJ_TN),
        in_specs=[pl.BlockSpec((PROJ_TM, D_MODEL), lambda i, j: (i, 0)),
                  pl.BlockSpec((None, 1, D_MODEL), lambda i, j: (i // tiles_per_batch, 0, 0)),
                  pl.BlockSpec((None, 1, D_MODEL), lambda i, j: (i // tiles_per_batch, 0, 0)),
                  pl.BlockSpec((D_MODEL, PROJ_TN), lambda i, j: (0, j))],
        out_specs=pl.BlockSpec((PROJ_TM, PROJ_TN), lambda i, j: (i, j)),
        out_shape=jax.ShapeDtypeStruct((N_TOK, PROJ_WIDTH), BF16),
        scratch_shapes=[pltpu.VMEM((PROJ_TM, D_MODEL), BF16)],
        compiler_params=_cparams(("arbitrary", "arbitrary")),
        name="proj",
    )(x2, sc, sh, w_bf)


def _rope(x, cos, sin_lo, sin_hi):
    return (x * cos + pltpu.roll(x, 96, axis=1) * sin_lo + pltpu.roll(x, 32, axis=1) * sin_hi)


def _attn_kernel(q_ref, k_ref, v_ref, cq_ref, slq_ref, shq_ref, ck_ref, slk_ref, shk_ref,
                 qw_ref, kw_ref, aw_ref, o_ref, kr_ref, v1_ref):
    @pl.when(pl.program_id(2) == 0)
    def _():
        k = _rms(k_ref[...].astype(F32)) * kw_ref[...]
        kr_ref[...] = _rope(k, ck_ref[...], slk_ref[...], shk_ref[...]).astype(BF16)
        v1_ref[:, :HEAD_DIM] = v_ref[...]
        v1_ref[:, HEAD_DIM:] = jnp.ones((SEQ, HEAD_DIM), BF16)

    scale = math.log2(math.e) / math.sqrt(HEAD_DIM)
    cq = cq_ref[...]
    slq = slq_ref[...]
    shq = shq_ref[...]
    heads = [slice(h * HEAD_DIM, (h + 1) * HEAD_DIM) for h in range(KV_GROUP)]
    qs = []
    for cols in heads:
        q = _rms(q_ref[:, cols].astype(F32)) * qw_ref[...]
        qs.append((_rope(q, cq, slq, shq) * scale).astype(BF16))
    scores = [lax.dot_general(q, kr_ref[...], (((1,), (1,)), ((), ())), preferred_element_type=F32)
              for q in qs]
    outs = []
    for s in scores:
        p = jnp.exp2((s - jnp.max(s, axis=-1, keepdims=True)).astype(BF16))
        ov = jnp.dot(p, v1_ref[...], preferred_element_type=F32)
        outs.append(ov[:, :HEAD_DIM] / ov[:, HEAD_DIM:HEAD_DIM + 1])
    for cols, o in zip(heads, outs):
        o_ref[:, cols] = (_rms(o) * aw_ref[:, cols]).astype(BF16)


def _attention(proj, cos, sin_lo, sin_hi, qw, kw, aw):
    nq = SEQ // ATTN_TQ
    gw = KV_GROUP * HEAD_DIM
    tab_q = pl.BlockSpec((ATTN_TQ, HEAD_DIM), lambda b, g, i: (i, 0))
    tab_k = pl.BlockSpec((SEQ, HEAD_DIM), lambda b, g, i: (0, 0))
    return pl.pallas_call(
        _attn_kernel,
        grid=(BATCH, N_KV_HEADS, nq),
        in_specs=[pl.BlockSpec((ATTN_TQ, gw), lambda b, g, i: (b * nq + i, g)),
                  pl.BlockSpec((SEQ, HEAD_DIM), lambda b, g, i: (b, COL_K + g)),
                  pl.BlockSpec((SEQ, HEAD_DIM), lambda b, g, i: (b, COL_V + g)),
                  tab_q, tab_q, tab_q, tab_k, tab_k, tab_k,
                  pl.BlockSpec((1, HEAD_DIM), lambda b, g, i: (0, 0)),
                  pl.BlockSpec((1, HEAD_DIM), lambda b, g, i: (0, 0)),
                  pl.BlockSpec((1, gw), lambda b, g, i: (0, g))],
        out_specs=pl.BlockSpec((ATTN_TQ, gw), lambda b, g, i: (b * nq + i, g)),
        out_shape=jax.ShapeDtypeStruct((N_TOK, ATTN_WIDTH), BF16),
        scratch_shapes=[pltpu.VMEM((SEQ, HEAD_DIM), BF16), pltpu.VMEM((SEQ, 2 * HEAD_DIM), BF16)],
        compiler_params=_cparams(("arbitrary", "arbitrary", "arbitrary")),
        name="attn",
    )(proj, proj, proj, cos, sin_lo, sin_hi, cos, sin_lo, sin_hi, qw, kw, aw)


def _hgrn_kernel(qr_ref, ff_ref, fb_ref, iv_ref, go_ref, lb_ref, nw_ref, o_ref, acc_ref, st_ref):
    C = HGRN_CHUNK
    nc = SEQ // C
    trips = nc // HGRN_UN
    row = lax.broadcasted_iota(jnp.int32, (C, C), 0)
    col = lax.broadcasted_iota(jnp.int32, (C, C), 1)
    keeps = (row >= col, row <= col)
    lasts = (C - 1, 0)
    f_refs = (ff_ref, fb_ref)

    nt_dims = (((1,), (1,)), ((), ()))
    tn_dims = (((0,), (0,)), ((), ()))
    st_ref[...] = jnp.zeros_like(st_ref)

    def trip(it, finish):
        chains = []
        for h in range(HGRN_HB):
            cols = slice(h * HEAD_DIM, (h + 1) * HEAD_DIM)
            for d in range(2):
                for u in range(HGRN_UN):
                    n = it * HGRN_UN + u
                    cidx = n if d == 0 else nc - 1 - n
                    chains.append(dict(h=h, d=d, cols=cols, rows=pl.ds(pl.multiple_of(cidx * C, C), C)))

        for ch in chains:
            d = ch["d"]
            lb = lb_ref[d, :, ch["cols"]]
            fg = lb + (1.0 - lb) * _sigmoid(f_refs[d][ch["rows"], ch["cols"]].astype(F32))
            ch["kk"] = 1.0 - fg
            lf = jnp.log(fg)
            lf_hi = lf.astype(BF16)
            lf_lo = (lf - lf_hi.astype(F32)).astype(BF16)
            tri = jnp.where(keeps[d], 1.0, 0.0).astype(BF16)
            ch["b"] = (jnp.dot(tri, lf_hi, preferred_element_type=F32)
                       + jnp.dot(tri, lf_lo, preferred_element_type=F32))
        for ch in chains:
            b = ch["b"]
            bl = b[lasts[ch["d"]]:lasts[ch["d"]] + 1, :]
            qx = qr_ref[ch["rows"], ch["cols"]].astype(F32)
            ch["qd"] = (qx * _sigmoid(qx) * jnp.exp(b)).astype(BF16)
            kd = (ch["kk"] * jnp.exp(-b)).astype(BF16)
            ku = (ch["kk"] * jnp.exp(bl - b)).astype(BF16)
            ch["v"] = iv_ref[ch["rows"], ch["cols"]]
            ch["decay"] = jnp.exp(bl)
            ch["sc"] = lax.dot_general(ch["qd"], kd, nt_dims, preferred_element_type=F32)
            ch["u_t"] = lax.dot_general(ch["v"], ku, tn_dims, preferred_element_type=F32)
        for h in range(HGRN_HB):
            for d in range(2):
                state = st_ref[2 * h + d]
                for ch in chains:
                    if ch["h"] == h and ch["d"] == d:
                        ch["state"] = state.astype(BF16)
                        state = state * ch["decay"] + ch["u_t"]
                st_ref[2 * h + d] = state
        for ch in chains:
            sc = jnp.where(keeps[ch["d"]], ch["sc"], 0.0).astype(BF16)
            ch["o"] = (jnp.dot(sc, ch["v"], preferred_element_type=F32)
                       + lax.dot_general(ch["qd"], ch["state"], nt_dims, preferred_element_type=F32))
        for ch in chains:
            rows, cols = ch["rows"], ch["cols"]
            if finish:
                o = _rms(acc_ref[rows, cols] + ch["o"]) * nw_ref[:, cols]
                g = go_ref[rows, cols].astype(F32)
                o_ref[rows, cols] = (o * (g * _sigmoid(g))).astype(BF16)
            else:
                acc_ref[rows, cols] = ch["o"]

    def first_half(it, carry):
        trip(it, False)
        return carry

    def second_half(it, carry):
        trip(it, True)
        return carry

    lax.fori_loop(0, trips // 2, first_half, 0)
    lax.fori_loop(trips // 2, trips, second_half, 0)


def _hgrn(proj, lb, nw):
    width = HGRN_HB * HEAD_DIM

    def col(c0):
        return pl.BlockSpec((SEQ, width), lambda b, h: (b, c0 // HGRN_HB + h))

    return pl.pallas_call(
        _hgrn_kernel,
        grid=(BATCH, N_HGRN_HEADS // HGRN_HB),
        in_specs=[col(COL_QR), col(COL_FF), col(COL_FB), col(COL_IN), col(COL_GO),
                  pl.BlockSpec((2, 1, width), lambda b, h: (0, 0, h)),
                  pl.BlockSpec((1, width), lambda b, h: (0, h))],
        out_specs=pl.BlockSpec((SEQ, width), lambda b, h: (b, h)),
        out_shape=jax.ShapeDtypeStruct((N_TOK, HGRN_WIDTH), BF16),
        scratch_shapes=[pltpu.VMEM((SEQ, width), F32),
                        pltpu.VMEM((2 * HGRN_HB, HEAD_DIM, HEAD_DIM), F32)],
        compiler_params=_cparams(("arbitrary", "arbitrary")),
        name="hgrn",
    )(proj, proj, proj, proj, proj, lb, nw)


def _mix_kernel(oa_ref, or_ref, wa_ref, wr_ref, x_ref, g1_ref, sc_ref, sh_ref, lg_ref, lbias_ref,
                wrt_ref, brt_ref, x1_ref, h2_ref, idx_ref, gate_ref, rank_ref, cnt_ref, carry_ref):
    i = pl.program_id(0)

    @pl.when(i == 0)
    def _():
        carry_ref[...] = jnp.zeros_like(carry_ref)

    tm = MIX_SUB
    subs = [slice(u * tm, (u + 1) * tm) for u in range(MIX_TM // MIX_SUB)]
    lane = lax.broadcasted_iota(jnp.int32, (tm, LANES), 1)
    neg = jnp.float32(-jnp.inf)

    ys = [jnp.dot(oa_ref[rs, :], wa_ref[...], preferred_element_type=F32)
          + jnp.dot(or_ref[rs, :], wr_ref[...], preferred_element_type=F32) for rs in subs]
    h2s = []
    for rs, y in zip(subs, ys):
        x1 = _layer_norm(DEEPNORM_ALPHA * x_ref[rs, :] + g1_ref[...] * y) * lg_ref[...] + lbias_ref[...]
        x1_ref[rs, :] = x1
        h2 = _layer_norm(x1) * (1.0 + sc_ref[...]) + sh_ref[...]
        h2_ref[rs, :] = h2
        h2s.append(h2.astype(BF16))
    logit_list = [jnp.dot(h2, wrt_ref[...], preferred_element_type=F32) + brt_ref[...] for h2 in h2s]

    picks = []
    for work in logit_list:
        vals, sels = [], []
        for _ in range(TOP_K):
            m = jnp.max(work, axis=-1, keepdims=True)
            sel = jnp.min(jnp.where(work == m, lane, LANES), axis=-1, keepdims=True)
            vals.append(m)
            sels.append(sel)
            work = jnp.where(lane == sel, neg, work)
        es = [jnp.exp(v - vals[0]) for v in vals]
        multi = jnp.zeros((tm, LANES), F32)
        for sel in sels:
            multi = multi + jnp.where(lane == sel, 1.0, 0.0)
        picks.append((sels, es, es[0] + es[1] + es[2] + es[3], multi))

    r = lax.broadcasted_iota(jnp.int32, (tm, tm), 0)
    c = lax.broadcasted_iota(jnp.int32, (tm, tm), 1)
    strict = jnp.where(r > c, 1.0, 0.0).astype(BF16)
    within = [jnp.dot(strict, multi.astype(BF16), preferred_element_type=F32) for _, _, _, multi in picks]
    carry = carry_ref[...]
    for rs, (sels, es, denom, multi), inside in zip(subs, picks, within):
        before = inside + carry
        carry = carry + jnp.sum(multi, axis=0, keepdims=True)
        idx_out = jnp.zeros((tm, LANES), jnp.int32)
        gate_out = jnp.zeros((tm, LANES), F32)
        rank_out = jnp.zeros((tm, LANES), F32)
        for k in range(TOP_K):
            rk = jnp.sum(jnp.where(lane == sels[k], before, 0.0), axis=-1, keepdims=True)
            idx_out = jnp.where(lane == k, sels[k], idx_out)
            gate_out = jnp.where(lane == k, es[k] / denom, gate_out)
            rank_out = jnp.where(lane == k, rk, rank_out)
        idx_ref[rs, :] = idx_out
        gate_ref[rs, :] = gate_out
        rank_ref[rs, :] = rank_out.astype(jnp.int32)
    carry_ref[...] = carry
    cnt_ref[...] = carry


def _mix(o_attn, o_r, wa, wr, x2, g1, sc2, sh2, ln_g, ln_b, w_rt, b_rt):
    tiles_per_batch = SEQ // MIX_TM
    rows = lambda w: pl.BlockSpec((MIX_TM, w), lambda i: (i, 0))
    full = lambda a, b: pl.BlockSpec((a, b), lambda i: (0, 0))
    per_batch = pl.BlockSpec((None, 1, D_MODEL), lambda i: (i // tiles_per_batch, 0, 0))
    return pl.pallas_call(
        _mix_kernel,
        grid=(N_TOK // MIX_TM,),
        in_specs=[rows(ATTN_WIDTH), rows(HGRN_WIDTH), full(ATTN_WIDTH, D_MODEL), full(HGRN_WIDTH, D_MODEL),
                  rows(D_MODEL), per_batch, per_batch, per_batch, full(1, D_MODEL), full(1, D_MODEL),
                  full(D_MODEL, LANES), full(1, LANES)],
        out_specs=[rows(D_MODEL), rows(D_MODEL), rows(LANES), rows(LANES), rows(LANES), full(1, LANES)],
        out_shape=[jax.ShapeDtypeStruct((N_TOK, D_MODEL), F32),
                   jax.ShapeDtypeStruct((N_TOK, D_MODEL), F32),
                   jax.ShapeDtypeStruct((N_TOK, LANES), jnp.int32),
                   jax.ShapeDtypeStruct((N_TOK, LANES), F32),
                   jax.ShapeDtypeStruct((N_TOK, LANES), jnp.int32),
                   jax.ShapeDtypeStruct((1, LANES), F32)],
        scratch_shapes=[pltpu.VMEM((1, LANES), F32)],
        compiler_params=_cparams(("arbitrary",)),
        name="mix",
    )(o_attn, o_r, wa, wr, x2, g1, sc2, sh2, ln_g, ln_b, w_rt, b_rt)


def _wait_rows(n, make_copy):
    for bit in range(MOE_SB.bit_length()):
        @pl.when(((n >> bit) & 1) == 1)
        def _(bit=bit):
            make_copy(1 << bit).wait()


def _for_rows(lo, hi, fn):
    groups = (hi - lo) // MOE_ISSUE_UNROLL

    def group(q, carry):
        base = lo + q * MOE_ISSUE_UNROLL
        for u in range(MOE_ISSUE_UNROLL):
            fn(base + u)
        return carry

    def single(r, carry):
        fn(r)
        return carry

    lax.fori_loop(0, groups, group, 0)
    lax.fori_loop(lo + groups * MOE_ISSUE_UNROLL, hi, single, 0)


def _moe_kernel(se_ref, so_ref, sn_ref, na_ref, src_ref, dst_ref, h2_hbm, w1_hbm, b1_ref, w2_hbm, b2_ref,
                y_hbm, xg_ref, xb_ref, acc_ref, wf1g_ref, wf1l_ref, wf2_ref, wb1g_ref, wb1l_ref, wb2_ref,
                gsem, ssem, wsem):
    g = pl.program_id(0)
    nj = MOE_NJ
    n_assign = N_TOK * TOP_K
    n_slices = na_ref[0] * nj

    def weight_copies(t, s):
        e = se_ref[jnp.minimum(t // nj, MOE_G - 1)]
        col = pl.multiple_of((t % nj) * MOE_TH, MOE_TH)
        return (pltpu.make_async_copy(w1_hbm.at[e, :, pl.ds(col, MOE_TH)], wf1g_ref.at[s], wsem.at[s]),
                pltpu.make_async_copy(w1_hbm.at[e, :, pl.ds(D_EXPERT + col, MOE_TH)], wf1l_ref.at[s], wsem.at[s]),
                pltpu.make_async_copy(w2_hbm.at[e, pl.ds(col, MOE_TH), :], wf2_ref.at[s], wsem.at[s]))

    def cast_weights(s):
        wb1g_ref[s] = wf1g_ref[s].astype(BF16)
        wb1l_ref[s] = wf1l_ref[s].astype(BF16)
        wb2_ref[s] = wf2_ref[s].astype(BF16)

    def tiles_of(rows):
        return (rows + MOE_TMI - 1) // MOE_TMI

    n = sn_ref[g]
    nt = tiles_of(n)
    slot = g % 2
    g_next = jnp.minimum(g + 1, MOE_G - 1)
    n_next = jnp.where(g + 1 < MOE_G, sn_ref[g_next], 0)
    off_next = so_ref[g_next]
    g_prev = jnp.maximum(g - 1, 0)
    n_prev = jnp.where(g >= 1, sn_ref[g_prev], 0)
    off_prev = so_ref[g_prev]
    step_rows = MOE_Q_STEP + nt * MOE_Q_TILE
    eager = jnp.where(nt > 0, nj * step_rows, 0)

    def gather_copy(tok, r):
        return pltpu.make_async_copy(h2_hbm.at[pl.ds(tok, 1)], xg_ref.at[pl.ds(r, 1)], gsem)

    def scatter_copy(s, r, a):
        return pltpu.make_async_copy(acc_ref.at[s, pl.ds(r, 1)], y_hbm.at[pl.ds(a, 1)], ssem.at[s])

    def gather_start(r):
        gather_copy(src_ref[off_next + r], r).start()

    def scatter_start(r):
        scatter_copy(1 - slot, r, dst_ref[off_prev + r]).start()

    def eager_issue(first, count):
        for q in range(count):
            gather_start(first + q)
            scatter_start(first + q)

    def tile_rows(i):
        return pl.ds(pl.multiple_of(i * MOE_TMI, MOE_TMI), MOE_TMI)

    def begin():
        @pl.when(g == 0)
        def _():
            xg_ref[...] = jnp.zeros_like(xg_ref)
            acc_ref[...] = jnp.zeros_like(acc_ref)
            spare = pltpu.make_async_copy(acc_ref.at[0], y_hbm.at[pl.ds(n_assign, MOE_SB)], ssem.at[0])
            spare.start()
            spare.wait()
            _for_rows(0, n, lambda r: gather_copy(src_ref[so_ref[0] + r], r).start())
            for t in range(2):
                @pl.when(t < n_slices)
                def _(t=t):
                    for c in weight_copies(t, t):
                        c.start()

            @pl.when(n_slices > 0)
            def _():
                for c in weight_copies(0, 0):
                    c.wait()
                cast_weights(0)

        nt_prev = tiles_of(n_prev)
        eager_prev = jnp.where(nt_prev > 0, nj * (MOE_Q_STEP + nt_prev * MOE_Q_TILE), 0)
        n_prev2 = jnp.where(g >= 2, sn_ref[jnp.maximum(g - 2, 0)], 0)
        gathered = jnp.maximum(eager_prev, n)
        scattered = jnp.maximum(eager_prev, n_prev2)
        _wait_rows(gathered,
                   lambda k: pltpu.make_async_copy(h2_hbm.at[pl.ds(0, k)], xg_ref.at[pl.ds(0, k)], gsem))
        _wait_rows(scattered,
                   lambda k: pltpu.make_async_copy(acc_ref.at[slot, pl.ds(0, k)], y_hbm.at[pl.ds(0, k)],
                                                   ssem.at[slot]))

        def prep(i, carry):
            rows = tile_rows(i)
            xb_ref[rows, :] = xg_ref[rows, :].astype(BF16)
            acc_ref[slot, rows, :] = jnp.broadcast_to(b2_ref[...], (MOE_TMI, D_MODEL))
            return carry

        lax.fori_loop(0, nt, prep, 0)

    begin()

    def hidden_slice(j, ws):
        t = g * nj + j

        @pl.when(t + 1 < n_slices)
        def _():
            for c in weight_copies(t + 1, 1 - ws):
                c.wait()

        @pl.when(t + 2 < n_slices)
        def _():
            for c in weight_copies(t + 2, ws):
                c.start()

        b1g = b1_ref[pl.ds(j, 1), :]
        b1l = b1_ref[pl.ds(nj + j, 1), :]

        def tiles(first_tile, count, cast_next):
            first_row = j * step_rows + first_tile * MOE_Q_TILE
            if cast_next:
                eager_issue(first_row, MOE_Q_STEP + count * MOE_Q_TILE)
            else:
                eager_issue(first_row + MOE_Q_STEP, count * MOE_Q_TILE)
            rows = [tile_rows(first_tile + i) for i in range(count)]
            xs = [xb_ref[r, :] for r in rows]
            hid = [(jnp.dot(x, wb1g_ref[ws], preferred_element_type=F32) + b1g,
                    jnp.dot(x, wb1l_ref[ws], preferred_element_type=F32) + b1l) for x in xs]
            if cast_next:
                cast_weights(1 - ws)
            parts = []
            for hg, hl in hid:
                hg = jnp.minimum(hg, SWIGLU_LIMIT)
                hl = jnp.clip(hl, -SWIGLU_LIMIT, SWIGLU_LIMIT)
                act = hg * _sigmoid(SWIGLU_ALPHA * hg) * (hl + 1.0)
                parts.append(jnp.dot(act.astype(BF16), wb2_ref[ws], preferred_element_type=F32))
            for r, part in zip(rows, parts):
                acc_ref[slot, r, :] += part

        @pl.when(nt >= 2)
        def _():
            tiles(0, 2, True)

        @pl.when(nt >= 4)
        def _():
            tiles(2, 2, False)

        @pl.when(nt == 1)
        def _():
            tiles(0, 1, True)

        @pl.when(nt == 3)
        def _():
            tiles(2, 1, False)

    def slice_pair(jj, carry):
        for ws in range(2):
            hidden_slice(2 * jj + ws, ws)
        return carry

    @pl.when(nt > 0)
    def _():
        lax.fori_loop(0, nj // 2, slice_pair, 0)

    _for_rows(jnp.minimum(eager, n_next), n_next, gather_start)
    _for_rows(jnp.minimum(eager, n_prev), n_prev, scatter_start)

    @pl.when(g == MOE_G - 1)
    def _():
        _wait_rows(n_prev, lambda k: pltpu.make_async_copy(acc_ref.at[1 - slot, pl.ds(0, k)],
                                                            y_hbm.at[pl.ds(0, k)], ssem.at[1 - slot]))


def _moe(sb_expert, sb_off, sb_n, n_active, src_tok, dst_row, h2, w1, b1, w2, b2):
    grid_spec = pltpu.PrefetchScalarGridSpec(
        num_scalar_prefetch=6,
        grid=(MOE_G,),
        in_specs=[
            pl.BlockSpec(memory_space=pl.ANY),
            pl.BlockSpec(memory_space=pl.ANY),
            pl.BlockSpec((None, 2 * MOE_NJ, MOE_TH), lambda g, se, *_: (se[g], 0, 0)),
            pl.BlockSpec(memory_space=pl.ANY),
            pl.BlockSpec((None, 1, D_MODEL), lambda g, se, *_: (se[g], 0, 0)),
        ],
        out_specs=pl.BlockSpec(memory_space=pl.ANY),
        scratch_shapes=[pltpu.VMEM((MOE_SB, D_MODEL), F32),
                        pltpu.VMEM((MOE_SB, D_MODEL), BF16),
                        pltpu.VMEM((2, MOE_SB, D_MODEL), F32),
                        pltpu.VMEM((2, D_MODEL, MOE_TH), F32),
                        pltpu.VMEM((2, D_MODEL, MOE_TH), F32),
                        pltpu.VMEM((2, MOE_TH, D_MODEL), F32),
                        pltpu.VMEM((2, D_MODEL, MOE_TH), BF16),
                        pltpu.VMEM((2, D_MODEL, MOE_TH), BF16),
                        pltpu.VMEM((2, MOE_TH, D_MODEL), BF16),
                        pltpu.SemaphoreType.DMA(()),
                        pltpu.SemaphoreType.DMA((2,)),
                        pltpu.SemaphoreType.DMA((2,))],
    )
    return pl.pallas_call(
        _moe_kernel,
        grid_spec=grid_spec,
        out_shape=jax.ShapeDtypeStruct((N_TOK * TOP_K + MOE_SB, D_MODEL), F32),
        compiler_params=_cparams(("arbitrary",)),
        name="moe",
    )(sb_expert, sb_off, sb_n, n_active, src_tok, dst_row, h2, w1, b1, w2, b2)


def _final_kernel(x1_ref, y0_ref, y1_ref, y2_ref, y3_ref, gate_ref, g2_ref, lg_ref, lb_ref, o_ref):
    y = gate_ref[:, 0:1] * y0_ref[...]
    for k, y_ref in enumerate((y1_ref, y2_ref, y3_ref), start=1):
        y = y + gate_ref[:, k:k + 1] * y_ref[...]
    z = DEEPNORM_ALPHA * x1_ref[...] + g2_ref[...] * y
    o_ref[...] = _layer_norm(z) * lg_ref[...] + lb_ref[...]


def _final(x1, y4, gates, g2, ln_g, ln_b):
    tiles_per_batch = SEQ // FIN_TM
    tiles = N_TOK // FIN_TM
    rows = lambda w: pl.BlockSpec((FIN_TM, w), lambda i: (i, 0))
    plane = lambda k: pl.BlockSpec((FIN_TM, D_MODEL), lambda i: (k * tiles + i, 0))
    vec = pl.BlockSpec((1, D_MODEL), lambda i: (0, 0))
    return pl.pallas_call(
        _final_kernel,
        grid=(tiles,),
        in_specs=[rows(D_MODEL), plane(0), plane(1), plane(2), plane(3), rows(LANES),
                  pl.BlockSpec((None, 1, D_MODEL), lambda i: (i // tiles_per_batch, 0, 0)), vec, vec],
        out_specs=rows(D_MODEL),
        out_shape=jax.ShapeDtypeStruct((N_TOK, D_MODEL), F32),
        compiler_params=_cparams(("arbitrary",)),
        name="final",
    )(x1, y4, y4, y4, y4, gates, g2, ln_g, ln_b)


def _rope_tables():
    rows = SEQ // GRID_W
    t = np.arange(SEQ)
    row = (t // GRID_W - rows // 2).astype(np.float32)
    col = (t % GRID_W - GRID_W // 2).astype(np.float32)
    inv_freq = jnp.asarray(ROPE_THETA, F32) ** (-jnp.arange(0, ROPE_AXIS_DIM, 2, dtype=F32) / ROPE_AXIS_DIM)
    ang_row = jnp.asarray(row)[:, None] * inv_freq[None, :]
    ang_col = jnp.asarray(col)[:, None] * inv_freq[None, :]
    zeros = jnp.zeros_like(ang_row)
    cos = jnp.concatenate([jnp.cos(ang_row)] * 2 + [jnp.cos(ang_col)] * 2, axis=-1)
    sin_lo = jnp.concatenate([-jnp.sin(ang_row), zeros, -jnp.sin(ang_col), zeros], axis=-1)
    sin_hi = jnp.concatenate([zeros, jnp.sin(ang_row), zeros, jnp.sin(ang_col)], axis=-1)
    return cos, sin_lo, sin_hi


def _routing(top_i, rank, counts):
    counts = counts.astype(jnp.int32)
    nsb = (counts + MOE_SB - 1) // MOE_SB
    sb_end = jnp.cumsum(nsb)
    sb_start = sb_end - nsb
    g = jnp.arange(MOE_G, dtype=jnp.int32)
    active = g < sb_end[-1]
    e_of_g = jnp.minimum(jnp.sum(g[:, None] >= sb_end[None, :], axis=1), N_EXPERTS - 1).astype(jnp.int32)
    first_row = (g - sb_start[e_of_g]) * MOE_SB
    n_of_g = jnp.where(active, jnp.clip(counts[e_of_g] - first_row, 0, MOE_SB), 0).astype(jnp.int32)
    order = jnp.argsort(-n_of_g, stable=True).astype(jnp.int32)
    place = jnp.zeros((MOE_G,), jnp.int32).at[order].set(g)
    sb_n = n_of_g[order]
    sb_off = (jnp.cumsum(sb_n) - sb_n).astype(jnp.int32)
    last_e = e_of_g[order[jnp.maximum(sb_end[-1] - 1, 0)]]
    sb_expert = jnp.where(sb_n > 0, e_of_g[order], last_e).astype(jnp.int32)
    assign = jnp.arange(N_TOK * TOP_K, dtype=jnp.int32)
    max_chunks = N_TOK // MOE_SB
    chunk_ids = jnp.arange(max_chunks, dtype=jnp.int32)
    base = sb_off[place[jnp.minimum(sb_start[:, None] + chunk_ids[None, :], MOE_G - 1)]]
    is_e = top_i[:, :, None] == jnp.arange(N_EXPERTS, dtype=jnp.int32)
    is_c = (rank // MOE_SB)[:, :, None] == chunk_ids
    base_e = jnp.sum(jnp.where(is_e[:, :, :, None], base[None, None], 0), axis=2)
    dest = (jnp.sum(jnp.where(is_c, base_e, 0), axis=-1) + rank % MOE_SB).reshape(-1)
    sorted_assign = jnp.zeros((N_TOK * TOP_K,), jnp.int32).at[dest].set(assign)
    src_tok = sorted_assign // TOP_K
    dst_row = (sorted_assign % TOP_K) * N_TOK + src_tok
    src_tok = jnp.concatenate([src_tok, jnp.zeros((MOE_SB,), jnp.int32)])
    dst_row = jnp.concatenate([dst_row, N_TOK * TOP_K + jnp.arange(MOE_SB, dtype=jnp.int32)])
    n_active = sb_end[-1:].astype(jnp.int32)
    return sb_expert, sb_off, sb_n, n_active, src_tok, dst_row


def kernel(x, c, w_ada, b_ada, w_in, q_norm_w, k_norm_w, attn_norm_w, hgrn_lb, hgrn_norm_w, w_out, ln1_g, ln1_b, w_router, b_router, w_exp_in, b_exp_in, w_exp_out, b_exp_out, ln2_g, ln2_b):
    c_pad = jnp.zeros((8, D_MODEL), F32).at[:BATCH].set(c)
    cos, sin_lo, sin_hi = _rope_tables()
    x2 = x.reshape(N_TOK, D_MODEL)
    for l in range(DEPTH):
        mod = _ada(c_pad, w_ada[l], b_ada[l][None, :])[:BATCH]
        sh1, sc1, g1, sh2, sc2, g2 = [m.reshape(BATCH, 1, D_MODEL) for m in jnp.split(mod, 6, axis=-1)]

        proj = _proj(x2, sc1, sh1, w_in[l].astype(BF16))
        o_attn = _attention(proj, cos, sin_lo, sin_hi, q_norm_w[l][None, :], k_norm_w[l][None, :],
                            attn_norm_w[l][None, :])
        lb = jnp.cumsum(jax.nn.softmax(hgrn_lb.astype(F32), axis=1), axis=1)[:, l]
        o_r = _hgrn(proj, lb.reshape(2, 1, HGRN_WIDTH), hgrn_norm_w[l][None, :])

        w_o = w_out[l].astype(BF16)
        w_rt = jnp.zeros((D_MODEL, LANES), BF16).at[:, :N_EXPERTS].set(w_router[l].astype(BF16))
        b_rt = jnp.full((1, LANES), -1e30, F32).at[0, :N_EXPERTS].set(b_router[l])
        x1, h2, idx, gates, rank, counts = _mix(
            o_attn, o_r, w_o[:ATTN_WIDTH], w_o[ATTN_WIDTH:], x2, g1, sc2, sh2,
            ln1_g[l][None, :], ln1_b[l][None, :], w_rt, b_rt)
        sb_expert, sb_off, sb_n, n_active, src_tok, dst_row = _routing(idx[:, :TOP_K], rank[:, :TOP_K],
                                                                       counts[0, :N_EXPERTS])
        y4 = _moe(sb_expert, sb_off, sb_n, n_active, src_tok, dst_row, h2, w_exp_in[l],
                  b_exp_in[l].reshape(N_EXPERTS, 2 * MOE_NJ, MOE_TH), w_exp_out[l], b_exp_out[l][:, None, :])
        x2 = _final(x1, y4, gates, g2, ln2_g[l][None, :], ln2_b[l][None, :])
    return x2.reshape(BATCH, SEQ, D_MODEL)
```

```python
import functools
import math

import numpy as np
import jax
import jax.numpy as jnp
from jax import lax
from jax.experimental import pallas as pl
from jax.experimental.pallas import tpu as pltpu

F32 = jnp.float32
BF16 = jnp.bfloat16

D_MODEL = 2048
BATCH = 4
SEQ = 2048
DEPTH = 1
N_TOK = BATCH * SEQ
HEAD_DIM = 128
ATTN_WIDTH = 1024
N_Q_HEADS = 8
N_KV_HEADS = 2
KV_GROUP = 4
HGRN_WIDTH = 1024
N_HGRN_HEADS = 8
HGRN_CHUNK = 64
GRID_W = 64
ROPE_THETA = 10000.0
ROPE_AXIS_DIM = 64
N_EXPERTS = 32
TOP_K = 4
D_EXPERT = 2048
SWIGLU_LIMIT = 7.0
SWIGLU_ALPHA = 1.702
NORM_EPS = 1e-6
DEEPNORM_ALPHA = (2 * DEPTH) ** 0.25
PROJ_WIDTH = 6656
LANES = 128

COL_Q = 0
COL_K = 8
COL_V = 10
COL_QR = 12
COL_FF = 20
COL_FB = 28
COL_IN = 36
COL_GO = 44

VMEM_LIMIT = 56 * 1024 * 1024

ADA_TN = 1024
PROJ_TM = 1024
PROJ_TN = 1664
ATTN_TQ = 256
HGRN_HB = 4
HGRN_UN = 2
MIX_TM = 512
MIX_SUB = 256
MOE_ISSUE_UNROLL = 8
MOE_SB = 1024
MOE_TMI = 256
MOE_TH = 256
MOE_NJ = D_EXPERT // MOE_TH
MOE_G = N_TOK * TOP_K // MOE_SB + N_EXPERTS + 1
MOE_Q_STEP = 0
MOE_Q_TILE = (MOE_SB // (D_EXPERT // MOE_TH) - MOE_Q_STEP) // (MOE_SB // MOE_TMI)
FIN_TM = 256


def _cparams(sem):
    return pltpu.CompilerParams(dimension_semantics=sem, vmem_limit_bytes=VMEM_LIMIT)


def _sigmoid(x):
    return 1.0 / (1.0 + jnp.exp(-x))


def _layer_norm(x):
    mu = jnp.mean(x, axis=-1, keepdims=True)
    xc = x - mu
    var = jnp.mean(xc * xc, axis=-1, keepdims=True)
    return xc * lax.rsqrt(var + NORM_EPS)


def _rms(x):
    return x * lax.rsqrt(jnp.mean(x * x, axis=-1, keepdims=True) + NORM_EPS)


def _ada_kernel(c_ref, w_ref, b_ref, o_ref):
    c = c_ref[...]
    ca = c * _sigmoid(c)
    o_ref[...] = jnp.dot(ca.astype(BF16), w_ref[...].astype(BF16),
                         preferred_element_type=F32) + b_ref[...]


def _ada(c_pad, w, b):
    n = w.shape[1]
    return pl.pallas_call(
        _ada_kernel,
        grid=(n // ADA_TN,),
        in_specs=[pl.BlockSpec((8, D_MODEL), lambda j: (0, 0)),
                  pl.BlockSpec((D_MODEL, ADA_TN), lambda j: (0, j)),
                  pl.BlockSpec((1, ADA_TN), lambda j: (0, j))],
        out_specs=pl.BlockSpec((8, ADA_TN), lambda j: (0, j)),
        out_shape=jax.ShapeDtypeStruct((8, n), F32),
        compiler_params=_cparams(("arbitrary",)),
        name="ada",
    )(c_pad, w, b)


def _proj_kernel(x_ref, sc_ref, sh_ref, w_ref, o_ref, h_ref):
    @pl.when(pl.program_id(1) == 0)
    def _():
        h = _layer_norm(x_ref[...]) * (1.0 + sc_ref[...]) + sh_ref[...]
        h_ref[...] = h.astype(BF16)

    o_ref[...] = jnp.dot(h_ref[...], w_ref[...], preferred_element_type=F32).astype(BF16)


def _proj(x2, sc, sh, w_bf):
    tiles_per_batch = SEQ // PROJ_TM
    return pl.pallas_call(
        _proj_kernel,
        grid=(N_TOK // PROJ_TM, PROJ_WIDTH // PROJ_TN),
        in_specs=[pl.BlockSpec((PROJ_TM, D_MODEL), lambda i, j: (i, 0)),
                  pl.BlockSpec((None, 1, D_MODEL), lambda i, j: (i // tiles_per_batch, 0, 0)),
                  pl.BlockSpec((None, 1, D_MODEL), lambda i, j: (i // tiles_per_batch, 0, 0)),
                  pl.BlockSpec((D_MODEL, PROJ_TN), lambda i, j: (0, j))],
        out_specs=pl.BlockSpec((PROJ_TM, PROJ_TN), lambda i, j: (i, j)),
        out_shape=jax.ShapeDtypeStruct((N_TOK, PROJ_WIDTH), BF16),
        scratch_shapes=[pltpu.VMEM((PROJ_TM, D_MODEL), BF16)],
        compiler_params=_cparams(("arbitrary", "arbitrary")),
        name="proj",
    )(x2, sc, sh, w_bf)


def _rope(x, cos, sin_lo, sin_hi):
    return (x * cos + pltpu.roll(x, 96, axis=1) * sin_lo + pltpu.roll(x, 32, axis=1) * sin_hi)


def _attn_kernel(q_ref, k_ref, v_ref, cq_ref, slq_ref, shq_ref, ck_ref, slk_ref, shk_ref,
                 qw_ref, kw_ref, aw_ref, o_ref, kr_ref, v1_ref):
    @pl.when(pl.program_id(2) == 0)
    def _():
        k = _rms(k_ref[...].astype(F32)) * kw_ref[...]
        kr_ref[...] = _rope(k, ck_ref[...], slk_ref[...], shk_ref[...]).astype(BF16)
        v1_ref[:, :HEAD_DIM] = v_ref[...]
        v1_ref[:, HEAD_DIM:] = jnp.ones((SEQ, HEAD_DIM), BF16)

    scale = math.log2(math.e) / math.sqrt(HEAD_DIM)
    cq = cq_ref[...]
    slq = slq_ref[...]
    shq = shq_ref[...]
    heads = [slice(h * HEAD_DIM, (h + 1) * HEAD_DIM) for h in range(KV_GROUP)]
    qs = []
    for cols in heads:
        q = _rms(q_ref[:, cols].astype(F32)) * qw_ref[...]
        qs.append((_rope(q, cq, slq, shq) * scale).astype(BF16))
    scores = [lax.dot_general(q, kr_ref[...], (((1,), (1,)), ((), ())), preferred_element_type=F32)
              for q in qs]
    outs = []
    for s in scores:
        p = jnp.exp2((s - jnp.max(s, axis=-1, keepdims=True)).astype(BF16))
        ov = jnp.dot(p, v1_ref[...], preferred_element_type=F32)
        outs.append(ov[:, :HEAD_DIM] / ov[:, HEAD_DIM:HEAD_DIM + 1])
    for cols, o in zip(heads, outs):
        o_ref[:, cols] = (_rms(o) * aw_ref[:, cols]).astype(BF16)


def _attention(proj, cos, sin_lo, sin_hi, qw, kw, aw):
    nq = SEQ // ATTN_TQ
    gw = KV_GROUP * HEAD_DIM
    tab_q = pl.BlockSpec((ATTN_TQ, HEAD_DIM), lambda b, g, i: (i, 0))
    tab_k = pl.BlockSpec((SEQ, HEAD_DIM), lambda b, g, i: (0, 0))
    return pl.pallas_call(
        _attn_kernel,
        grid=(BATCH, N_KV_HEADS, nq),
        in_specs=[pl.BlockSpec((ATTN_TQ, gw), lambda b, g, i: (b * nq + i, g)),
                  pl.BlockSpec((SEQ, HEAD_DIM), lambda b, g, i: (b, COL_K + g)),
                  pl.BlockSpec((SEQ, HEAD_DIM), lambda b, g, i: (b, COL_V + g)),
                  tab_q, tab_q, tab_q, tab_k, tab_k, tab_k,
                  pl.BlockSpec((1, HEAD_DIM), lambda b, g, i: (0, 0)),
                  pl.BlockSpec((1, HEAD_DIM), lambda b, g, i: (0, 0)),
                  pl.BlockSpec((1, gw), lambda b, g, i: (0, g))],
        out_specs=pl.BlockSpec((ATTN_TQ, gw), lambda b, g, i: (b * nq + i, g)),
        out_shape=jax.ShapeDtypeStruct((N_TOK, ATTN_WIDTH), BF16),
        scratch_shapes=[pltpu.VMEM((SEQ, HEAD_DIM), BF16), pltpu.VMEM((SEQ, 2 * HEAD_DIM), BF16)],
        compiler_params=_cparams(("arbitrary", "arbitrary", "arbitrary")),
        name="attn",
    )(proj, proj, proj, cos, sin_lo, sin_hi, cos, sin_lo, sin_hi, qw, kw, aw)


def _hgrn_kernel(qr_ref, ff_ref, fb_ref, iv_ref, go_ref, lb_ref, nw_ref, o_ref, acc_ref, st_ref):
    C = HGRN_CHUNK
    nc = SEQ // C
    trips = nc // HGRN_UN
    row = lax.broadcasted_iota(jnp.int32, (C, C), 0)
    col = lax.broadcasted_iota(jnp.int32, (C, C), 1)
    keeps = (row >= col, row <= col)
    lasts = (C - 1, 0)
    f_refs = (ff_ref, fb_ref)

    nt_dims = (((1,), (1,)), ((), ()))
    tn_dims = (((0,), (0,)), ((), ()))
    st_ref[...] = jnp.zeros_like(st_ref)

    def trip(it, finish):
        chains = []
        for h in range(HGRN_HB):
            cols = slice(h * HEAD_DIM, (h + 1) * HEAD_DIM)
            for d in range(2):
                for u in range(HGRN_UN):
                    n = it * HGRN_UN + u
                    cidx = n if d == 0 else nc - 1 - n
                    chains.append(dict(h=h, d=d, cols=cols, rows=pl.ds(pl.multiple_of(cidx * C, C), C)))

        for ch in chains:
            d = ch["d"]
            lb = lb_ref[d, :, ch["cols"]]
            fg = lb + (1.0 - lb) * _sigmoid(f_refs[d][ch["rows"], ch["cols"]].astype(F32))
            ch["kk"] = 1.0 - fg
            lf = jnp.log(fg)
            lf_hi = lf.astype(BF16)
            lf_lo = (lf - lf_hi.astype(F32)).astype(BF16)
            tri = jnp.where(keeps[d], 1.0, 0.0).astype(BF16)
            ch["b"] = (jnp.dot(tri, lf_hi, preferred_element_type=F32)
                       + jnp.dot(tri, lf_lo, preferred_element_type=F32))
        for ch in chains:
            b = ch["b"]
            bl = b[lasts[ch["d"]]:lasts[ch["d"]] + 1, :]
            qx = qr_ref[ch["rows"], ch["cols"]].astype(F32)
            ch["qd"] = (qx * _sigmoid(qx) * jnp.exp(b)).astype(BF16)
            kd = (ch["kk"] * jnp.exp(-b)).astype(BF16)
            ku = (ch["kk"] * jnp.exp(bl - b)).astype(BF16)
            ch["v"] = iv_ref[ch["rows"], ch["cols"]]
            ch["decay"] = jnp.exp(bl)
            ch["sc"] = lax.dot_general(ch["qd"], kd, nt_dims, preferred_element_type=F32)
            ch["u_t"] = lax.dot_general(ch["v"], ku, tn_dims, preferred_element_type=F32)
        for h in range(HGRN_HB):
            for d in range(2):
                state = st_ref[2 * h + d]
                for ch in chains:
                    if ch["h"] == h and ch["d"] == d:
                        ch["state"] = state.astype(BF16)
                        state = state * ch["decay"] + ch["u_t"]
                st_ref[2 * h + d] = state
        for ch in chains:
            sc = jnp.where(keeps[ch["d"]], ch["sc"], 0.0).astype(BF16)
            ch["o"] = (jnp.dot(sc, ch["v"], preferred_element_type=F32)
                       + lax.dot_general(ch["qd"], ch["state"], nt_dims, preferred_element_type=F32))
        for ch in chains:
            rows, cols = ch["rows"], ch["cols"]
            if finish:
                o = _rms(acc_ref[rows, cols] + ch["o"]) * nw_ref[:, cols]
                g = go_ref[rows, cols].astype(F32)
                o_ref[rows, cols] = (o * (g * _sigmoid(g))).astype(BF16)
            else:
                acc_ref[rows, cols] = ch["o"]

    def first_half(it, carry):
        trip(it, False)
        return carry

    def second_half(it, carry):
        trip(it, True)
        return carry

    lax.fori_loop(0, trips // 2, first_half, 0)
    lax.fori_loop(trips // 2, trips, second_half, 0)


def _hgrn(proj, lb, nw):
    width = HGRN_HB * HEAD_DIM

    def col(c0):
        return pl.BlockSpec((SEQ, width), lambda b, h: (b, c0 // HGRN_HB + h))

    return pl.pallas_call(
        _hgrn_kernel,
        grid=(BATCH, N_HGRN_HEADS // HGRN_HB),
        in_specs=[col(COL_QR), col(COL_FF), col(COL_FB), col(COL_IN), col(COL_GO),
                  pl.BlockSpec((2, 1, width), lambda b, h: (0, 0, h)),
                  pl.BlockSpec((1, width), lambda b, h: (0, h))],
        out_specs=pl.BlockSpec((SEQ, width), lambda b, h: (b, h)),
        out_shape=jax.ShapeDtypeStruct((N_TOK, HGRN_WIDTH), BF16),
        scratch_shapes=[pltpu.VMEM((SEQ, width), F32),
                        pltpu.VMEM((2 * HGRN_HB, HEAD_DIM, HEAD_DIM), F32)],
        compiler_params=_cparams(("arbitrary", "arbitrary")),
        name="hgrn",
    )(proj, proj, proj, proj, proj, lb, nw)


def _mix_kernel(oa_ref, or_ref, wa_ref, wr_ref, x_ref, g1_ref, sc_ref, sh_ref, lg_ref, lbias_ref,
                wrt_ref, brt_ref, x1_ref, h2_ref, idx_ref, gate_ref, rank_ref, cnt_ref, carry_ref):
    i = pl.program_id(0)

    @pl.when(i == 0)
    def _():
        carry_ref[...] = jnp.zeros_like(carry_ref)

    tm = MIX_SUB
    subs = [slice(u * tm, (u + 1) * tm) for u in range(MIX_TM // MIX_SUB)]
    lane = lax.broadcasted_iota(jnp.int32, (tm, LANES), 1)
    neg = jnp.float32(-jnp.inf)

    ys = [jnp.dot(oa_ref[rs, :], wa_ref[...], preferred_element_type=F32)
          + jnp.dot(or_ref[rs, :], wr_ref[...], preferred_element_type=F32) for rs in subs]
    h2s = []
    for rs, y in zip(subs, ys):
        x1 = _layer_norm(DEEPNORM_ALPHA * x_ref[rs, :] + g1_ref[...] * y) * lg_ref[...] + lbias_ref[...]
        x1_ref[rs, :] = x1
        h2 = _layer_norm(x1) * (1.0 + sc_ref[...]) + sh_ref[...]
        h2_ref[rs, :] = h2
        h2s.append(h2.astype(BF16))
    logit_list = [jnp.dot(h2, wrt_ref[...], preferred_element_type=F32) + brt_ref[...] for h2 in h2s]

    picks = []
    for work in logit_list:
        vals, sels = [], []
        for _ in range(TOP_K):
            m = jnp.max(work, axis=-1, keepdims=True)
            sel = jnp.min(jnp.where(work == m, lane, LANES), axis=-1, keepdims=True)
            vals.append(m)
            sels.append(sel)
            work = jnp.where(lane == sel, neg, work)
        es = [jnp.exp(v - vals[0]) for v in vals]
        multi = jnp.zeros((tm, LANES), F32)
        for sel in sels:
            multi = multi + jnp.where(lane == sel, 1.0, 0.0)
        picks.append((sels, es, es[0] + es[1] + es[2] + es[3], multi))

    r = lax.broadcasted_iota(jnp.int32, (tm, tm), 0)
    c = lax.broadcasted_iota(jnp.int32, (tm, tm), 1)
    strict = jnp.where(r > c, 1.0, 0.0).astype(BF16)
    within = [jnp.dot(strict, multi.astype(BF16), preferred_element_type=F32) for _, _, _, multi in picks]
    carry = carry_ref[...]
    for rs, (sels, es, denom, multi), inside in zip(subs, picks, within):
        before = inside + carry
        carry = carry + jnp.sum(multi, axis=0, keepdims=True)
        idx_out = jnp.zeros((tm, LANES), jnp.int32)
        gate_out = jnp.zeros((tm, LANES), F32)
        rank_out = jnp.zeros((tm, LANES), F32)
        for k in range(TOP_K):
            rk = jnp.sum(jnp.where(lane == sels[k], before, 0.0), axis=-1, keepdims=True)
            idx_out = jnp.where(lane == k, sels[k], idx_out)
            gate_out = jnp.where(lane == k, es[k] / denom, gate_out)
            rank_out = jnp.where(lane == k, rk, rank_out)
        idx_ref[rs, :] = idx_out
        gate_ref[rs, :] = gate_out
        rank_ref[rs, :] = rank_out.astype(jnp.int32)
    carry_ref[...] = carry
    cnt_ref[...] = carry


def _mix(o_attn, o_r, wa, wr, x2, g1, sc2, sh2, ln_g, ln_b, w_rt, b_rt):
    tiles_per_batch = SEQ // MIX_TM
    rows = lambda w: pl.BlockSpec((MIX_TM, w), lambda i: (i, 0))
    full = lambda a, b: pl.BlockSpec((a, b), lambda i: (0, 0))
    per_batch = pl.BlockSpec((None, 1, D_MODEL), lambda i: (i // tiles_per_batch, 0, 0))
    return pl.pallas_call(
        _mix_kernel,
        grid=(N_TOK // MIX_TM,),
        in_specs=[rows(ATTN_WIDTH), rows(HGRN_WIDTH), full(ATTN_WIDTH, D_MODEL), full(HGRN_WIDTH, D_MODEL),
                  rows(D_MODEL), per_batch, per_batch, per_batch, full(1, D_MODEL), full(1, D_MODEL),
                  full(D_MODEL, LANES), full(1, LANES)],
        out_specs=[rows(D_MODEL), rows(D_MODEL), rows(LANES), rows(LANES), rows(LANES), full(1, LANES)],
        out_shape=[jax.ShapeDtypeStruct((N_TOK, D_MODEL), F32),
                   jax.ShapeDtypeStruct((N_TOK, D_MODEL), F32),
                   jax.ShapeDtypeStruct((N_TOK, LANES), jnp.int32),
                   jax.ShapeDtypeStruct((N_TOK, LANES), F32),
                   jax.ShapeDtypeStruct((N_TOK, LANES), jnp.int32),
                   jax.ShapeDtypeStruct((1, LANES), F32)],
        scratch_shapes=[pltpu.VMEM((1, LANES), F32)],
        compiler_params=_cparams(("arbitrary",)),
        name="mix",
    )(o_attn, o_r, wa, wr, x2, g1, sc2, sh2, ln_g, ln_b, w_rt, b_rt)


def _wait_rows(n, make_copy):
    for bit in range(MOE_SB.bit_length()):
        @pl.when(((n >> bit) & 1) == 1)
        def _(bit=bit):
            make_copy(1 << bit).wait()


def _for_rows(lo, hi, fn):
    groups = (hi - lo) // MOE_ISSUE_UNROLL

    def group(q, carry):
        base = lo + q * MOE_ISSUE_UNROLL
        for u in range(MOE_ISSUE_UNROLL):
            fn(base + u)
        return carry

    def single(r, carry):
        fn(r)
        return carry

    lax.fori_loop(0, groups, group, 0)
    lax.fori_loop(lo + groups * MOE_ISSUE_UNROLL, hi, single, 0)


def _moe_kernel(se_ref, so_ref, sn_ref, na_ref, src_ref, dst_ref, h2_hbm, w1_hbm, b1_ref, w2_hbm, b2_ref,
                y_hbm, xg_ref, xb_ref, acc_ref, wf1g_ref, wf1l_ref, wf2_ref, wb1g_ref, wb1l_ref, wb2_ref,
                gsem, ssem, wsem):
    g = pl.program_id(0)
    nj = MOE_NJ
    n_assign = N_TOK * TOP_K
    n_slices = na_ref[0] * nj

    def weight_copies(t, s):
        e = se_ref[jnp.minimum(t // nj, MOE_G - 1)]
        col = pl.multiple_of((t % nj) * MOE_TH, MOE_TH)
        return (pltpu.make_async_copy(w1_hbm.at[e, :, pl.ds(col, MOE_TH)], wf1g_ref.at[s], wsem.at[s]),
                pltpu.make_async_copy(w1_hbm.at[e, :, pl.ds(D_EXPERT + col, MOE_TH)], wf1l_ref.at[s], wsem.at[s]),
                pltpu.make_async_copy(w2_hbm.at[e, pl.ds(col, MOE_TH), :], wf2_ref.at[s], wsem.at[s]))

    def cast_weights(s):
        wb1g_ref[s] = wf1g_ref[s].astype(BF16)
        wb1l_ref[s] = wf1l_ref[s].astype(BF16)
        wb2_ref[s] = wf2_ref[s].astype(BF16)

    def tiles_of(rows):
        return (rows + MOE_TMI - 1) // MOE_TMI

    n = sn_ref[g]
    nt = tiles_of(n)
    slot = g % 2
    g_next = jnp.minimum(g + 1, MOE_G - 1)
    n_next = jnp.where(g + 1 < MOE_G, sn_ref[g_next], 0)
    off_next = so_ref[g_next]
    g_prev = jnp.maximum(g - 1, 0)
    n_prev = jnp.where(g >= 1, sn_ref[g_prev], 0)
    off_prev = so_ref[g_prev]
    step_rows = MOE_Q_STEP + nt * MOE_Q_TILE
    eager = jnp.where(nt > 0, nj * step_rows, 0)

    def gather_copy(tok, r):
        return pltpu.make_async_copy(h2_hbm.at[pl.ds(tok, 1)], xg_ref.at[pl.ds(r, 1)], gsem)

    def scatter_copy(s, r, a):
        return pltpu.make_async_copy(acc_ref.at[s, pl.ds(r, 1)], y_hbm.at[pl.ds(a, 1)], ssem.at[s])

    def gather_start(r):
        gather_copy(src_ref[off_next + r], r).start()

    def scatter_start(r):
        scatter_copy(1 - slot, r, dst_ref[off_prev + r]).start(priority=1)

    def eager_issue(first, count):
        for q in range(count):
            gather_start(first + q)
            scatter_start(first + q)

    def tile_rows(i):
        return pl.ds(pl.multiple_of(i * MOE_TMI, MOE_TMI), MOE_TMI)

    def begin():
        @pl.when(g == 0)
        def _():
            xg_ref[...] = jnp.zeros_like(xg_ref)
            acc_ref[...] = jnp.zeros_like(acc_ref)
            spare = pltpu.make_async_copy(acc_ref.at[0], y_hbm.at[pl.ds(n_assign, MOE_SB)], ssem.at[0])
            spare.start()
            spare.wait()
            _for_rows(0, n, lambda r: gather_copy(src_ref[so_ref[0] + r], r).start())
            for t in range(2):
                @pl.when(t < n_slices)
                def _(t=t):
                    for c in weight_copies(t, t):
                        c.start()

            @pl.when(n_slices > 0)
            def _():
                for c in weight_copies(0, 0):
                    c.wait()
                cast_weights(0)

        nt_prev = tiles_of(n_prev)
        eager_prev = jnp.where(nt_prev > 0, nj * (MOE_Q_STEP + nt_prev * MOE_Q_TILE), 0)
        n_prev2 = jnp.where(g >= 2, sn_ref[jnp.maximum(g - 2, 0)], 0)
        gathered = jnp.maximum(eager_prev, n)
        scattered = jnp.maximum(eager_prev, n_prev2)
        _wait_rows(gathered,
                   lambda k: pltpu.make_async_copy(h2_hbm.at[pl.ds(0, k)], xg_ref.at[pl.ds(0, k)], gsem))
        _wait_rows(scattered,
                   lambda k: pltpu.make_async_copy(acc_ref.at[slot, pl.ds(0, k)], y_hbm.at[pl.ds(0, k)],
                                                   ssem.at[slot]))

        def prep(i, carry):
            rows = tile_rows(i)
            xb_ref[rows, :] = xg_ref[rows, :].astype(BF16)
            acc_ref[slot, rows, :] = jnp.broadcast_to(b2_ref[...], (MOE_TMI, D_MODEL))
            return carry

        lax.fori_loop(0, nt, prep, 0)

    begin()

    def hidden_slice(j, ws):
        t = g * nj + j

        @pl.when(t + 1 < n_slices)
        def _():
            for c in weight_copies(t + 1, 1 - ws):
                c.wait()

        @pl.when(t + 2 < n_slices)
        def _():
            for c in weight_copies(t + 2, ws):
                c.start()

        b1g = b1_ref[pl.ds(j, 1), :]
        b1l = b1_ref[pl.ds(nj + j, 1), :]

        def tiles(first_tile, count, cast_next):
            first_row = pl.multiple_of(j * step_rows + first_tile * MOE_Q_TILE, MOE_Q_TILE)
            if cast_next:
                eager_issue(first_row, MOE_Q_STEP + count * MOE_Q_TILE)
            else:
                eager_issue(first_row + MOE_Q_STEP, count * MOE_Q_TILE)
            rows = [tile_rows(first_tile + i) for i in range(count)]
            xs = [xb_ref[r, :] for r in rows]
            hid = [(jnp.dot(x, wb1g_ref[ws], preferred_element_type=F32) + b1g,
                    jnp.dot(x, wb1l_ref[ws], preferred_element_type=F32) + b1l) for x in xs]
            if cast_next:
                cast_weights(1 - ws)
            parts = []
            for hg, hl in hid:
                hg = jnp.minimum(hg, SWIGLU_LIMIT)
                hl = jnp.clip(hl, -SWIGLU_LIMIT, SWIGLU_LIMIT)
                act = hg * _sigmoid(SWIGLU_ALPHA * hg) * (hl + 1.0)
                parts.append(jnp.dot(act.astype(BF16), wb2_ref[ws], preferred_element_type=F32))
            for r, part in zip(rows, parts):
                acc_ref[slot, r, :] += part

        @pl.when(nt >= 2)
        def _():
            tiles(0, 2, True)

        @pl.when(nt >= 4)
        def _():
            tiles(2, 2, False)

        @pl.when(nt == 1)
        def _():
            tiles(0, 1, True)

        @pl.when(nt == 3)
        def _():
            tiles(2, 1, False)

    def slice_pair(jj, carry):
        for ws in range(2):
            hidden_slice(2 * jj + ws, ws)
        return carry

    @pl.when(nt > 0)
    def _():
        lax.fori_loop(0, nj // 2, slice_pair, 0)

    _for_rows(jnp.minimum(eager, n_next), n_next, gather_start)
    _for_rows(jnp.minimum(eager, n_prev), n_prev, scatter_start)

    @pl.when(g == MOE_G - 1)
    def _():
        _wait_rows(n_prev, lambda k: pltpu.make_async_copy(acc_ref.at[1 - slot, pl.ds(0, k)],
                                                            y_hbm.at[pl.ds(0, k)], ssem.at[1 - slot]))


def _moe(sb_expert, sb_off, sb_n, n_active, src_tok, dst_row, h2, w1, b1, w2, b2):
    grid_spec = pltpu.PrefetchScalarGridSpec(
        num_scalar_prefetch=6,
        grid=(MOE_G,),
        in_specs=[
            pl.BlockSpec(memory_space=pl.ANY),
            pl.BlockSpec(memory_space=pl.ANY),
            pl.BlockSpec((None, 2 * MOE_NJ, MOE_TH), lambda g, se, *_: (se[g], 0, 0)),
            pl.BlockSpec(memory_space=pl.ANY),
            pl.BlockSpec((None, 1, D_MODEL), lambda g, se, *_: (se[g], 0, 0)),
        ],
        out_specs=pl.BlockSpec(memory_space=pl.ANY),
        scratch_shapes=[pltpu.VMEM((MOE_SB, D_MODEL), F32),
                        pltpu.VMEM((MOE_SB, D_MODEL), BF16),
                        pltpu.VMEM((2, MOE_SB, D_MODEL), F32),
                        pltpu.VMEM((2, D_MODEL, MOE_TH), F32),
                        pltpu.VMEM((2, D_MODEL, MOE_TH), F32),
                        pltpu.VMEM((2, MOE_TH, D_MODEL), F32),
                        pltpu.VMEM((2, D_MODEL, MOE_TH), BF16),
                        pltpu.VMEM((2, D_MODEL, MOE_TH), BF16),
                        pltpu.VMEM((2, MOE_TH, D_MODEL), BF16),
                        pltpu.SemaphoreType.DMA(()),
                        pltpu.SemaphoreType.DMA((2,)),
                        pltpu.SemaphoreType.DMA((2,))],
    )
    return pl.pallas_call(
        _moe_kernel,
        grid_spec=grid_spec,
        out_shape=jax.ShapeDtypeStruct((N_TOK * TOP_K + MOE_SB, D_MODEL), F32),
        compiler_params=_cparams(("arbitrary",)),
        name="moe",
    )(sb_expert, sb_off, sb_n, n_active, src_tok, dst_row, h2, w1, b1, w2, b2)


def _final_kernel(x1_ref, y0_ref, y1_ref, y2_ref, y3_ref, gate_ref, g2_ref, lg_ref, lb_ref, o_ref):
    y = gate_ref[:, 0:1] * y0_ref[...]
    for k, y_ref in enumerate((y1_ref, y2_ref, y3_ref), start=1):
        y = y + gate_ref[:, k:k + 1] * y_ref[...]
    z = DEEPNORM_ALPHA * x1_ref[...] + g2_ref[...] * y
    o_ref[...] = _layer_norm(z) * lg_ref[...] + lb_ref[...]


def _final(x1, y4, gates, g2, ln_g, ln_b):
    tiles_per_batch = SEQ // FIN_TM
    tiles = N_TOK // FIN_TM
    rows = lambda w: pl.BlockSpec((FIN_TM, w), lambda i: (i, 0))
    plane = lambda k: pl.BlockSpec((FIN_TM, D_MODEL), lambda i: (k * tiles + i, 0))
    vec = pl.BlockSpec((1, D_MODEL), lambda i: (0, 0))
    return pl.pallas_call(
        _final_kernel,
        grid=(tiles,),
        in_specs=[rows(D_MODEL), plane(0), plane(1), plane(2), plane(3), rows(LANES),
                  pl.BlockSpec((None, 1, D_MODEL), lambda i: (i // tiles_per_batch, 0, 0)), vec, vec],
        out_specs=rows(D_MODEL),
        out_shape=jax.ShapeDtypeStruct((N_TOK, D_MODEL), F32),
        compiler_params=_cparams(("arbitrary",)),
        name="final",
    )(x1, y4, y4, y4, y4, gates, g2, ln_g, ln_b)


def _rope_tables():
    rows = SEQ // GRID_W
    t = np.arange(SEQ)
    row = (t // GRID_W - rows // 2).astype(np.float32)
    col = (t % GRID_W - GRID_W // 2).astype(np.float32)
    inv_freq = jnp.asarray(ROPE_THETA, F32) ** (-jnp.arange(0, ROPE_AXIS_DIM, 2, dtype=F32) / ROPE_AXIS_DIM)
    ang_row = jnp.asarray(row)[:, None] * inv_freq[None, :]
    ang_col = jnp.asarray(col)[:, None] * inv_freq[None, :]
    zeros = jnp.zeros_like(ang_row)
    cos = jnp.concatenate([jnp.cos(ang_row)] * 2 + [jnp.cos(ang_col)] * 2, axis=-1)
    sin_lo = jnp.concatenate([-jnp.sin(ang_row), zeros, -jnp.sin(ang_col), zeros], axis=-1)
    sin_hi = jnp.concatenate([zeros, jnp.sin(ang_row), zeros, jnp.sin(ang_col)], axis=-1)
    return cos, sin_lo, sin_hi


def _routing(top_i, rank, counts):
    counts = counts.astype(jnp.int32)
    nsb = (counts + MOE_SB - 1) // MOE_SB
    sb_end = jnp.cumsum(nsb)
    sb_start = sb_end - nsb
    g = jnp.arange(MOE_G, dtype=jnp.int32)
    active = g < sb_end[-1]
    e_of_g = jnp.minimum(jnp.sum(g[:, None] >= sb_end[None, :], axis=1), N_EXPERTS - 1).astype(jnp.int32)
    first_row = (g - sb_start[e_of_g]) * MOE_SB
    n_of_g = jnp.where(active, jnp.clip(counts[e_of_g] - first_row, 0, MOE_SB), 0).astype(jnp.int32)
    order = jnp.argsort(-n_of_g, stable=True).astype(jnp.int32)
    place = jnp.zeros((MOE_G,), jnp.int32).at[order].set(g)
    sb_n = n_of_g[order]
    sb_off = (jnp.cumsum(sb_n) - sb_n).astype(jnp.int32)
    last_e = e_of_g[order[jnp.maximum(sb_end[-1] - 1, 0)]]
    sb_expert = jnp.where(sb_n > 0, e_of_g[order], last_e).astype(jnp.int32)
    assign = jnp.arange(N_TOK * TOP_K, dtype=jnp.int32)
    max_chunks = N_TOK // MOE_SB
    chunk_ids = jnp.arange(max_chunks, dtype=jnp.int32)
    base = sb_off[place[jnp.minimum(sb_start[:, None] + chunk_ids[None, :], MOE_G - 1)]]
    is_e = top_i[:, :, None] == jnp.arange(N_EXPERTS, dtype=jnp.int32)
    is_c = (rank // MOE_SB)[:, :, None] == chunk_ids
    base_e = jnp.sum(jnp.where(is_e[:, :, :, None], base[None, None], 0), axis=2)
    dest = (jnp.sum(jnp.where(is_c, base_e, 0), axis=-1) + rank % MOE_SB).reshape(-1)
    sorted_assign = lax.sort_key_val(dest, assign)[1]
    src_tok = sorted_assign // TOP_K
    dst_row = (sorted_assign % TOP_K) * N_TOK + src_tok
    src_tok = jnp.concatenate([src_tok, jnp.zeros((MOE_SB,), jnp.int32)])
    dst_row = jnp.concatenate([dst_row, N_TOK * TOP_K + jnp.arange(MOE_SB, dtype=jnp.int32)])
    n_active = sb_end[-1:].astype(jnp.int32)
    return sb_expert, sb_off, sb_n, n_active, src_tok, dst_row


def kernel(x, c, w_ada, b_ada, w_in, q_norm_w, k_norm_w, attn_norm_w, hgrn_lb, hgrn_norm_w, w_out, ln1_g, ln1_b, w_router, b_router, w_exp_in, b_exp_in, w_exp_out, b_exp_out, ln2_g, ln2_b):
    c_pad = jnp.zeros((8, D_MODEL), F32).at[:BATCH].set(c)
    cos, sin_lo, sin_hi = _rope_tables()
    x2 = x.reshape(N_TOK, D_MODEL)
    for l in range(DEPTH):
        mod = _ada(c_pad, w_ada[l], b_ada[l][None, :])[:BATCH]
        sh1, sc1, g1, sh2, sc2, g2 = [m.reshape(BATCH, 1, D_MODEL) for m in jnp.split(mod, 6, axis=-1)]

        proj = _proj(x2, sc1, sh1, w_in[l].astype(BF16))
        o_attn = _attention(proj, cos, sin_lo, sin_hi, q_norm_w[l][None, :], k_norm_w[l][None, :],
                            attn_norm_w[l][None, :])
        lb = jnp.cumsum(jax.nn.softmax(hgrn_lb.astype(F32), axis=1), axis=1)[:, l]
        o_r = _hgrn(proj, lb.reshape(2, 1, HGRN_WIDTH), hgrn_norm_w[l][None, :])

        w_o = w_out[l].astype(BF16)
        w_rt = jnp.zeros((D_MODEL, LANES), BF16).at[:, :N_EXPERTS].set(w_router[l].astype(BF16))
        b_rt = jnp.full((1, LANES), -1e30, F32).at[0, :N_EXPERTS].set(b_router[l])
        x1, h2, idx, gates, rank, counts = _mix(
            o_attn, o_r, w_o[:ATTN_WIDTH], w_o[ATTN_WIDTH:], x2, g1, sc2, sh2,
            ln1_g[l][None, :], ln1_b[l][None, :], w_rt, b_rt)
        sb_expert, sb_off, sb_n, n_active, src_tok, dst_row = _routing(idx[:, :TOP_K], rank[:, :TOP_K],
                                                                       counts[0, :N_EXPERTS])
        y4 = _moe(sb_expert, sb_off, sb_n, n_active, src_tok, dst_row, h2, w_exp_in[l],
                  b_exp_in[l].reshape(N_EXPERTS, 2 * MOE_NJ, MOE_TH), w_exp_out[l], b_exp_out[l][:, None, :])
        x2 = _final(x1, y4, gates, g2, ln2_g[l][None, :], ln2_b[l][None, :])
    return x2.reshape(BATCH, SEQ, D_MODEL)
```

```python
import functools
import math

import numpy as np
import jax
import jax.numpy as jnp
from jax import lax
from jax.experimental import pallas as pl
from jax.experimental.pallas import tpu as pltpu

F32 = jnp.float32
BF16 = jnp.bfloat16

D_MODEL = 2048
BATCH = 4
SEQ = 2048
DEPTH = 1
N_TOK = BATCH * SEQ
HEAD_DIM = 128
ATTN_WIDTH = 1024
N_Q_HEADS = 8
N_KV_HEADS = 2
KV_GROUP = 4
HGRN_WIDTH = 1024
N_HGRN_HEADS = 8
HGRN_CHUNK = 64
GRID_W = 64
ROPE_THETA = 10000.0
ROPE_AXIS_DIM = 64
N_EXPERTS = 32
TOP_K = 4
D_EXPERT = 2048
SWIGLU_LIMIT = 7.0
SWIGLU_ALPHA = 1.702
NORM_EPS = 1e-6
DEEPNORM_ALPHA = (2 * DEPTH) ** 0.25
PROJ_WIDTH = 6656
LANES = 128

COL_Q = 0
COL_K = 8
COL_V = 10
COL_QR = 12
COL_FF = 20
COL_FB = 28
COL_IN = 36
COL_GO = 44

VMEM_LIMIT = 56 * 1024 * 1024

ADA_TN = 1024
PROJ_TM = 512
PROJ_TN = 3328
ATTN_TQ = 256
HGRN_HB = 4
HGRN_UN = 2
MIX_TM = 512
MIX_SUB = 256
MOE_ISSUE_UNROLL = 8
MOE_SB = 1024
MOE_TMI = 256
MOE_TH = 256
MOE_NJ = D_EXPERT // MOE_TH
MOE_G = N_TOK * TOP_K // MOE_SB + N_EXPERTS + 1
MOE_Q_STEP = 0
MOE_Q_TILE = (MOE_SB // (D_EXPERT // MOE_TH) - MOE_Q_STEP) // (MOE_SB // MOE_TMI)
FIN_TM = 256


def _cparams(sem):
    return pltpu.CompilerParams(dimension_semantics=sem, vmem_limit_bytes=VMEM_LIMIT)


def _sigmoid(x):
    return 1.0 / (1.0 + jnp.exp(-x))


def _layer_norm(x):
    mu = jnp.mean(x, axis=-1, keepdims=True)
    xc = x - mu
    var = jnp.mean(xc * xc, axis=-1, keepdims=True)
    return xc * lax.rsqrt(var + NORM_EPS)


def _rms(x):
    return x * lax.rsqrt(jnp.mean(x * x, axis=-1, keepdims=True) + NORM_EPS)


def _ada_kernel(c_ref, w_ref, b_ref, o_ref):
    c = c_ref[...]
    ca = c * _sigmoid(c)
    o_ref[...] = jnp.dot(ca.astype(BF16), w_ref[...].astype(BF16),
                         preferred_element_type=F32) + b_ref[...]


def _ada(c_pad, w, b):
    n = w.shape[1]
    return pl.pallas_call(
        _ada_kernel,
        grid=(n // ADA_TN,),
        in_specs=[pl.BlockSpec((8, D_MODEL), lambda j: (0, 0)),
                  pl.BlockSpec((D_MODEL, ADA_TN), lambda j: (0, j)),
                  pl.BlockSpec((1, ADA_TN), lambda j: (0, j))],
        out_specs=pl.BlockSpec((8, ADA_TN), lambda j: (0, j)),
        out_shape=jax.ShapeDtypeStruct((8, n), F32),
        compiler_params=_cparams(("arbitrary",)),
        name="ada",
    )(c_pad, w, b)


def _proj_kernel(x_ref, sc_ref, sh_ref, w_ref, o_ref, h_ref):
    @pl.when(pl.program_id(1) == 0)
    def _():
        h = _layer_norm(x_ref[...]) * (1.0 + sc_ref[...]) + sh_ref[...]
        h_ref[...] = h.astype(BF16)

    o_ref[...] = jnp.dot(h_ref[...], w_ref[...], preferred_element_type=F32).astype(BF16)


def _proj(x2, sc, sh, w_bf):
    tiles_per_batch = SEQ // PROJ_TM
    return pl.pallas_call(
        _proj_kernel,
        grid=(N_TOK // PROJ_TM, PROJ_WIDTH // PROJ_TN),
        in_specs=[pl.BlockSpec((PROJ_TM, D_MODEL), lambda i, j: (i, 0)),
                  pl.BlockSpec((None, 1, D_MODEL), lambda i, j: (i // tiles_per_batch, 0, 0)),
                  pl.BlockSpec((None, 1, D_MODEL), lambda i, j: (i // tiles_per_batch, 0, 0)),
                  pl.BlockSpec((D_MODEL, PROJ_TN), lambda i, j: (0, j))],
        out_specs=pl.BlockSpec((PROJ_TM, PROJ_TN), lambda i, j: (i, j)),
        out_shape=jax.ShapeDtypeStruct((N_TOK, PROJ_WIDTH), BF16),
        scratch_shapes=[pltpu.VMEM((PROJ_TM, D_MODEL), BF16)],
        compiler_params=_cparams(("arbitrary", "arbitrary")),
        name="proj",
    )(x2, sc, sh, w_bf)


def _rope(x, cos, sin_lo, sin_hi):
    return (x * cos + pltpu.roll(x, 96, axis=1) * sin_lo + pltpu.roll(x, 32, axis=1) * sin_hi)


def _attn_kernel(q_ref, k_ref, v_ref, cq_ref, slq_ref, shq_ref, ck_ref, slk_ref, shk_ref,
                 qw_ref, kw_ref, aw_ref, o_ref, kr_ref, v1_ref):
    @pl.when(pl.program_id(2) == 0)
    def _():
        k = _rms(k_ref[...].astype(F32)) * kw_ref[...]
        kr_ref[...] = _rope(k, ck_ref[...], slk_ref[...], shk_ref[...]).astype(BF16)
        v1_ref[:, :HEAD_DIM] = v_ref[...]
        v1_ref[:, HEAD_DIM:] = jnp.ones((SEQ, HEAD_DIM), BF16)

    scale = math.log2(math.e) / math.sqrt(HEAD_DIM)
    cq = cq_ref[...]
    slq = slq_ref[...]
    shq = shq_ref[...]
    heads = [slice(h * HEAD_DIM, (h + 1) * HEAD_DIM) for h in range(KV_GROUP)]
    qs = []
    for cols in heads:
        q = _rms(q_ref[:, cols].astype(F32)) * qw_ref[...]
        qs.append((_rope(q, cq, slq, shq) * scale).astype(BF16))
    scores = [lax.dot_general(q, kr_ref[...], (((1,), (1,)), ((), ())), preferred_element_type=F32)
              for q in qs]
    outs = []
    for s in scores:
        p = jnp.exp2((s - jnp.max(s, axis=-1, keepdims=True)).astype(BF16))
        ov = jnp.dot(p, v1_ref[...], preferred_element_type=F32)
        outs.append(ov[:, :HEAD_DIM] / ov[:, HEAD_DIM:HEAD_DIM + 1])
    for cols, o in zip(heads, outs):
        o_ref[:, cols] = (_rms(o) * aw_ref[:, cols]).astype(BF16)


def _attention(proj, cos, sin_lo, sin_hi, qw, kw, aw):
    nq = SEQ // ATTN_TQ
    gw = KV_GROUP * HEAD_DIM
    tab_q = pl.BlockSpec((ATTN_TQ, HEAD_DIM), lambda b, g, i: (i, 0))
    tab_k = pl.BlockSpec((SEQ, HEAD_DIM), lambda b, g, i: (0, 0))
    return pl.pallas_call(
        _attn_kernel,
        grid=(BATCH, N_KV_HEADS, nq),
        in_specs=[pl.BlockSpec((ATTN_TQ, gw), lambda b, g, i: (b * nq + i, g)),
                  pl.BlockSpec((SEQ, HEAD_DIM), lambda b, g, i: (b, COL_K + g)),
                  pl.BlockSpec((SEQ, HEAD_DIM), lambda b, g, i: (b, COL_V + g)),
                  tab_q, tab_q, tab_q, tab_k, tab_k, tab_k,
                  pl.BlockSpec((1, HEAD_DIM), lambda b, g, i: (0, 0)),
                  pl.BlockSpec((1, HEAD_DIM), lambda b, g, i: (0, 0)),
                  pl.BlockSpec((1, gw), lambda b, g, i: (0, g))],
        out_specs=pl.BlockSpec((ATTN_TQ, gw), lambda b, g, i: (b * nq + i, g)),
        out_shape=jax.ShapeDtypeStruct((N_TOK, ATTN_WIDTH), BF16),
        scratch_shapes=[pltpu.VMEM((SEQ, HEAD_DIM), BF16), pltpu.VMEM((SEQ, 2 * HEAD_DIM), BF16)],
        compiler_params=_cparams(("arbitrary", "arbitrary", "arbitrary")),
        name="attn",
    )(proj, proj, proj, cos, sin_lo, sin_hi, cos, sin_lo, sin_hi, qw, kw, aw)


def _hgrn_kernel(qr_ref, ff_ref, fb_ref, iv_ref, go_ref, lb_ref, nw_ref, o_ref, acc_ref, st_ref):
    C = HGRN_CHUNK
    nc = SEQ // C
    trips = nc // HGRN_UN
    row = lax.broadcasted_iota(jnp.int32, (C, C), 0)
    col = lax.broadcasted_iota(jnp.int32, (C, C), 1)
    keeps = (row >= col, row <= col)
    lasts = (C - 1, 0)
    f_refs = (ff_ref, fb_ref)

    nt_dims = (((1,), (1,)), ((), ()))
    tn_dims = (((0,), (0,)), ((), ()))
    st_ref[...] = jnp.zeros_like(st_ref)

    def trip(it, finish):
        chains = []
        for h in range(HGRN_HB):
            cols = slice(h * HEAD_DIM, (h + 1) * HEAD_DIM)
            for d in range(2):
                for u in range(HGRN_UN):
                    n = it * HGRN_UN + u
                    cidx = n if d == 0 else nc - 1 - n
                    chains.append(dict(h=h, d=d, cols=cols, rows=pl.ds(pl.multiple_of(cidx * C, C), C)))

        for ch in chains:
            d = ch["d"]
            lb = lb_ref[d, :, ch["cols"]]
            fg = lb + (1.0 - lb) * _sigmoid(f_refs[d][ch["rows"], ch["cols"]].astype(F32))
            ch["kk"] = 1.0 - fg
            lf = jnp.log(fg)
            lf_hi = lf.astype(BF16)
            lf_lo = (lf - lf_hi.astype(F32)).astype(BF16)
            tri = jnp.where(keeps[d], 1.0, 0.0).astype(BF16)
            ch["b"] = (jnp.dot(tri, lf_hi, preferred_element_type=F32)
                       + jnp.dot(tri, lf_lo, preferred_element_type=F32))
        for ch in chains:
            b = ch["b"]
            bl = b[lasts[ch["d"]]:lasts[ch["d"]] + 1, :]
            qx = qr_ref[ch["rows"], ch["cols"]].astype(F32)
            ch["qd"] = (qx * _sigmoid(qx) * jnp.exp(b)).astype(BF16)
            kd = ch["kk"] * jnp.exp(-b)
            ch["decay"] = jnp.exp(bl)
            ku = (kd * ch["decay"]).astype(BF16)
            kd = kd.astype(BF16)
            ch["v"] = iv_ref[ch["rows"], ch["cols"]]
            ch["sc"] = lax.dot_general(ch["qd"], kd, nt_dims, preferred_element_type=F32)
            ch["u_t"] = lax.dot_general(ch["v"], ku, tn_dims, preferred_element_type=F32)
        for h in range(HGRN_HB):
            for d in range(2):
                state = st_ref[2 * h + d]
                for ch in chains:
                    if ch["h"] == h and ch["d"] == d:
                        ch["state"] = state.astype(BF16)
                        state = state * ch["decay"] + ch["u_t"]
                st_ref[2 * h + d] = state
        for ch in chains:
            sc = jnp.where(keeps[ch["d"]], ch["sc"], 0.0).astype(BF16)
            ch["o"] = (jnp.dot(sc, ch["v"], preferred_element_type=F32)
                       + lax.dot_general(ch["qd"], ch["state"], nt_dims, preferred_element_type=F32))
        for ch in chains:
            rows, cols = ch["rows"], ch["cols"]
            if finish:
                o = _rms(acc_ref[rows, cols] + ch["o"]) * nw_ref[:, cols]
                g = go_ref[rows, cols].astype(F32)
                o_ref[rows, cols] = (o * (g * _sigmoid(g))).astype(BF16)
            else:
                acc_ref[rows, cols] = ch["o"]

    def first_half(it, carry):
        trip(it, False)
        return carry

    def second_half(it, carry):
        trip(it, True)
        return carry

    lax.fori_loop(0, trips // 2, first_half, 0)
    lax.fori_loop(trips // 2, trips, second_half, 0)


def _hgrn(proj, lb, nw):
    width = HGRN_HB * HEAD_DIM

    def col(c0):
        return pl.BlockSpec((SEQ, width), lambda b, h: (b, c0 // HGRN_HB + h))

    return pl.pallas_call(
        _hgrn_kernel,
        grid=(BATCH, N_HGRN_HEADS // HGRN_HB),
        in_specs=[col(COL_QR), col(COL_FF), col(COL_FB), col(COL_IN), col(COL_GO),
                  pl.BlockSpec((2, 1, width), lambda b, h: (0, 0, h)),
                  pl.BlockSpec((1, width), lambda b, h: (0, h))],
        out_specs=pl.BlockSpec((SEQ, width), lambda b, h: (b, h)),
        out_shape=jax.ShapeDtypeStruct((N_TOK, HGRN_WIDTH), BF16),
        scratch_shapes=[pltpu.VMEM((SEQ, width), F32),
                        pltpu.VMEM((2 * HGRN_HB, HEAD_DIM, HEAD_DIM), F32)],
        compiler_params=_cparams(("arbitrary", "arbitrary")),
        name="hgrn",
    )(proj, proj, proj, proj, proj, lb, nw)


def _mix_kernel(oa_ref, or_ref, wa_ref, wr_ref, x_ref, g1_ref, sc_ref, sh_ref, lg_ref, lbias_ref,
                wrt_ref, brt_ref, x1_ref, h2_ref, idx_ref, gate_ref, rank_ref, cnt_ref, carry_ref):
    i = pl.program_id(0)

    @pl.when(i == 0)
    def _():
        carry_ref[...] = jnp.zeros_like(carry_ref)

    tm = MIX_SUB
    subs = [slice(u * tm, (u + 1) * tm) for u in range(MIX_TM // MIX_SUB)]
    lane = lax.broadcasted_iota(jnp.int32, (tm, LANES), 1)
    neg = jnp.float32(-jnp.inf)

    ys = [jnp.dot(oa_ref[rs, :], wa_ref[...], preferred_element_type=F32)
          + jnp.dot(or_ref[rs, :], wr_ref[...], preferred_element_type=F32) for rs in subs]
    h2s = []
    for rs, y in zip(subs, ys):
        x1 = _layer_norm(DEEPNORM_ALPHA * x_ref[rs, :] + g1_ref[...] * y) * lg_ref[...] + lbias_ref[...]
        x1_ref[rs, :] = x1
        h2 = _layer_norm(x1) * (1.0 + sc_ref[...]) + sh_ref[...]
        h2_ref[rs, :] = h2
        h2s.append(h2.astype(BF16))
    logit_list = [jnp.dot(h2, wrt_ref[...], preferred_element_type=F32) + brt_ref[...] for h2 in h2s]

    picks = []
    for work in logit_list:
        vals, sels = [], []
        for _ in range(TOP_K):
            m = jnp.max(work, axis=-1, keepdims=True)
            sel = jnp.min(jnp.where(work == m, lane, LANES), axis=-1, keepdims=True)
            vals.append(m)
            sels.append(sel)
            work = jnp.where(lane == sel, neg, work)
        es = [jnp.exp(v - vals[0]) for v in vals]
        multi = jnp.zeros((tm, LANES), F32)
        for sel in sels:
            multi = multi + jnp.where(lane == sel, 1.0, 0.0)
        picks.append((sels, es, es[0] + es[1] + es[2] + es[3], multi))

    r = lax.broadcasted_iota(jnp.int32, (tm, tm), 0)
    c = lax.broadcasted_iota(jnp.int32, (tm, tm), 1)
    strict = jnp.where(r > c, 1.0, 0.0).astype(BF16)
    within = [jnp.dot(strict, multi.astype(BF16), preferred_element_type=F32) for _, _, _, multi in picks]
    carry = carry_ref[...]
    for rs, (sels, es, denom, multi), inside in zip(subs, picks, within):
        before = inside + carry
        carry = carry + jnp.sum(multi, axis=0, keepdims=True)
        idx_out = jnp.zeros((tm, LANES), jnp.int32)
        gate_out = jnp.zeros((tm, LANES), F32)
        rank_out = jnp.zeros((tm, LANES), F32)
        for k in range(TOP_K):
            rk = jnp.sum(jnp.where(lane == sels[k], before, 0.0), axis=-1, keepdims=True)
            idx_out = jnp.where(lane == k, sels[k], idx_out)
            gate_out = jnp.where(lane == k, es[k] / denom, gate_out)
            rank_out = jnp.where(lane == k, rk, rank_out)
        idx_ref[rs, :] = idx_out
        gate_ref[rs, :] = gate_out
        rank_ref[rs, :] = rank_out.astype(jnp.int32)
    carry_ref[...] = carry
    cnt_ref[...] = carry


def _mix(o_attn, o_r, wa, wr, x2, g1, sc2, sh2, ln_g, ln_b, w_rt, b_rt):
    tiles_per_batch = SEQ // MIX_TM
    rows = lambda w: pl.BlockSpec((MIX_TM, w), lambda i: (i, 0))
    full = lambda a, b: pl.BlockSpec((a, b), lambda i: (0, 0))
    per_batch = pl.BlockSpec((None, 1, D_MODEL), lambda i: (i // tiles_per_batch, 0, 0))
    return pl.pallas_call(
        _mix_kernel,
        grid=(N_TOK // MIX_TM,),
        in_specs=[rows(ATTN_WIDTH), rows(HGRN_WIDTH), full(ATTN_WIDTH, D_MODEL), full(HGRN_WIDTH, D_MODEL),
                  rows(D_MODEL), per_batch, per_batch, per_batch, full(1, D_MODEL), full(1, D_MODEL),
                  full(D_MODEL, LANES), full(1, LANES)],
        out_specs=[rows(D_MODEL), rows(D_MODEL), rows(LANES), rows(LANES), rows(LANES), full(1, LANES)],
        out_shape=[jax.ShapeDtypeStruct((N_TOK, D_MODEL), F32),
                   jax.ShapeDtypeStruct((N_TOK, D_MODEL), F32),
                   jax.ShapeDtypeStruct((N_TOK, LANES), jnp.int32),
                   jax.ShapeDtypeStruct((N_TOK, LANES), F32),
                   jax.ShapeDtypeStruct((N_TOK, LANES), jnp.int32),
                   jax.ShapeDtypeStruct((1, LANES), F32)],
        scratch_shapes=[pltpu.VMEM((1, LANES), F32)],
        compiler_params=_cparams(("arbitrary",)),
        name="mix",
    )(o_attn, o_r, wa, wr, x2, g1, sc2, sh2, ln_g, ln_b, w_rt, b_rt)


def _wait_rows(n, make_copy):
    for bit in range(MOE_SB.bit_length()):
        @pl.when(((n >> bit) & 1) == 1)
        def _(bit=bit):
            make_copy(1 << bit).wait()


def _for_rows(lo, hi, fn):
    groups = (hi - lo) // MOE_ISSUE_UNROLL

    def group(q, carry):
        base = lo + q * MOE_ISSUE_UNROLL
        for u in range(MOE_ISSUE_UNROLL):
            fn(base + u)
        return carry

    def single(r, carry):
        fn(r)
        return carry

    lax.fori_loop(0, groups, group, 0)
    lax.fori_loop(lo + groups * MOE_ISSUE_UNROLL, hi, single, 0)


def _moe_kernel(se_ref, so_ref, sn_ref, na_ref, src_ref, dst_ref, h2_hbm, w1_hbm, b1_ref, w2_hbm, b2_ref,
                y_hbm, xg_ref, xb_ref, acc_ref, wf1g_ref, wf1l_ref, wf2_ref, wb1g_ref, wb1l_ref, wb2_ref,
                gsem, ssem, wsem):
    g = pl.program_id(0)
    nj = MOE_NJ
    n_assign = N_TOK * TOP_K
    n_slices = na_ref[0] * nj

    def weight_copies(t, s):
        e = se_ref[jnp.minimum(t // nj, MOE_G - 1)]
        col = pl.multiple_of((t % nj) * MOE_TH, MOE_TH)
        return (pltpu.make_async_copy(w1_hbm.at[e, :, pl.ds(col, MOE_TH)], wf1g_ref.at[s], wsem.at[s]),
                pltpu.make_async_copy(w1_hbm.at[e, :, pl.ds(D_EXPERT + col, MOE_TH)], wf1l_ref.at[s], wsem.at[s]),
                pltpu.make_async_copy(w2_hbm.at[e, pl.ds(col, MOE_TH), :], wf2_ref.at[s], wsem.at[s]))

    def cast_weights(s):
        wb1g_ref[s] = wf1g_ref[s].astype(BF16)
        wb1l_ref[s] = wf1l_ref[s].astype(BF16)
        wb2_ref[s] = wf2_ref[s].astype(BF16)

    def tiles_of(rows):
        return (rows + MOE_TMI - 1) // MOE_TMI

    n = sn_ref[g]
    nt = tiles_of(n)
    slot = g % 2
    g_next = jnp.minimum(g + 1, MOE_G - 1)
    n_next = jnp.where(g + 1 < MOE_G, sn_ref[g_next], 0)
    off_next = so_ref[g_next]
    g_prev = jnp.maximum(g - 1, 0)
    n_prev = jnp.where(g >= 1, sn_ref[g_prev], 0)
    off_prev = so_ref[g_prev]
    step_rows = MOE_Q_STEP + nt * MOE_Q_TILE
    eager = jnp.where(nt > 0, nj * step_rows, 0)

    def gather_copy(tok, r):
        return pltpu.make_async_copy(h2_hbm.at[pl.ds(tok, 1)], xg_ref.at[pl.ds(r, 1)], gsem)

    def scatter_copy(s, r, a):
        return pltpu.make_async_copy(acc_ref.at[s, pl.ds(r, 1)], y_hbm.at[pl.ds(a, 1)], ssem.at[s])

    def gather_start(r):
        gather_copy(src_ref[off_next + r], r).start()

    def scatter_start(r):
        scatter_copy(1 - slot, r, dst_ref[off_prev + r]).start()

    def eager_issue(first, count):
        for q in range(count):
            gather_start(first + q)
            scatter_start(first + q)

    def tile_rows(i):
        return pl.ds(pl.multiple_of(i * MOE_TMI, MOE_TMI), MOE_TMI)

    def begin():
        @pl.when(g == 0)
        def _():
            xg_ref[...] = jnp.zeros_like(xg_ref)
            acc_ref[...] = jnp.zeros_like(acc_ref)
            spare = pltpu.make_async_copy(acc_ref.at[0], y_hbm.at[pl.ds(n_assign, MOE_SB)], ssem.at[0])
            spare.start()
            spare.wait()
            _for_rows(0, n, lambda r: gather_copy(src_ref[so_ref[0] + r], r).start())
            for t in range(2):
                @pl.when(t < n_slices)
                def _(t=t):
                    for c in weight_copies(t, t):
                        c.start()

            @pl.when(n_slices > 0)
            def _():
                for c in weight_copies(0, 0):
                    c.wait()
                cast_weights(0)

        nt_prev = tiles_of(n_prev)
        eager_prev = jnp.where(nt_prev > 0, nj * (MOE_Q_STEP + nt_prev * MOE_Q_TILE), 0)
        n_prev2 = jnp.where(g >= 2, sn_ref[jnp.maximum(g - 2, 0)], 0)
        gathered = jnp.maximum(eager_prev, n)
        scattered = jnp.maximum(eager_prev, n_prev2)
        _wait_rows(gathered,
                   lambda k: pltpu.make_async_copy(h2_hbm.at[pl.ds(0, k)], xg_ref.at[pl.ds(0, k)], gsem))
        _wait_rows(scattered,
                   lambda k: pltpu.make_async_copy(acc_ref.at[slot, pl.ds(0, k)], y_hbm.at[pl.ds(0, k)],
                                                   ssem.at[slot]))

        def prep(i, carry):
            rows = tile_rows(i)
            xb_ref[rows, :] = xg_ref[rows, :].astype(BF16)
            acc_ref[slot, rows, :] = jnp.broadcast_to(b2_ref[...], (MOE_TMI, D_MODEL))
            return carry

        lax.fori_loop(0, nt, prep, 0)

    begin()

    def hidden_slice(j, ws):
        t = g * nj + j

        @pl.when(t + 1 < n_slices)
        def _():
            for c in weight_copies(t + 1, 1 - ws):
                c.wait()

        @pl.when(t + 2 < n_slices)
        def _():
            for c in weight_copies(t + 2, ws):
                c.start()

        b1g = b1_ref[pl.ds(j, 1), :]
        b1l = b1_ref[pl.ds(nj + j, 1), :]

        def tiles(first_tile, count, cast_next):
            first_row = pl.multiple_of(j * step_rows + first_tile * MOE_Q_TILE, MOE_Q_TILE)
            if cast_next:
                eager_issue(first_row, MOE_Q_STEP + count * MOE_Q_TILE)
            else:
                eager_issue(first_row + MOE_Q_STEP, count * MOE_Q_TILE)
            rows = [tile_rows(first_tile + i) for i in range(count)]
            xs = [xb_ref[r, :] for r in rows]
            hid = [(jnp.dot(x, wb1g_ref[ws], preferred_element_type=F32) + b1g,
                    jnp.dot(x, wb1l_ref[ws], preferred_element_type=F32) + b1l) for x in xs]
            if cast_next:
                cast_weights(1 - ws)
            parts = []
            for hg, hl in hid:
                hg = jnp.minimum(hg, SWIGLU_LIMIT)
                hl = jnp.clip(hl, -SWIGLU_LIMIT, SWIGLU_LIMIT)
                act = hg * _sigmoid(SWIGLU_ALPHA * hg) * (hl + 1.0)
                parts.append(jnp.dot(act.astype(BF16), wb2_ref[ws], preferred_element_type=F32))
            for r, part in zip(rows, parts):
                acc_ref[slot, r, :] += part

        @pl.when(nt >= 2)
        def _():
            tiles(0, 2, True)

        @pl.when(nt >= 4)
        def _():
            tiles(2, 2, False)

        @pl.when(nt == 1)
        def _():
            tiles(0, 1, True)

        @pl.when(nt == 3)
        def _():
            tiles(2, 1, False)

    def slice_pair(jj, carry):
        for ws in range(2):
            hidden_slice(2 * jj + ws, ws)
        return carry

    @pl.when(nt > 0)
    def _():
        lax.fori_loop(0, nj // 2, slice_pair, 0)

    _for_rows(jnp.minimum(eager, n_next), n_next, gather_start)
    _for_rows(jnp.minimum(eager, n_prev), n_prev, scatter_start)

    @pl.when(g == MOE_G - 1)
    def _():
        _wait_rows(n_prev, lambda k: pltpu.make_async_copy(acc_ref.at[1 - slot, pl.ds(0, k)],
                                                            y_hbm.at[pl.ds(0, k)], ssem.at[1 - slot]))


def _moe(sb_expert, sb_off, sb_n, n_active, src_tok, dst_row, h2, w1, b1, w2, b2):
    grid_spec = pltpu.PrefetchScalarGridSpec(
        num_scalar_prefetch=6,
        grid=(MOE_G,),
        in_specs=[
            pl.BlockSpec(memory_space=pl.ANY),
            pl.BlockSpec(memory_space=pl.ANY),
            pl.BlockSpec((None, 2 * MOE_NJ, MOE_TH), lambda g, se, *_: (se[g], 0, 0)),
            pl.BlockSpec(memory_space=pl.ANY),
            pl.BlockSpec((None, 1, D_MODEL), lambda g, se, *_: (se[g], 0, 0)),
        ],
        out_specs=pl.BlockSpec(memory_space=pl.ANY),
        scratch_shapes=[pltpu.VMEM((MOE_SB, D_MODEL), F32),
                        pltpu.VMEM((MOE_SB, D_MODEL), BF16),
                        pltpu.VMEM((2, MOE_SB, D_MODEL), F32),
                        pltpu.VMEM((2, D_MODEL, MOE_TH), F32),
                        pltpu.VMEM((2, D_MODEL, MOE_TH), F32),
                        pltpu.VMEM((2, MOE_TH, D_MODEL), F32),
                        pltpu.VMEM((2, D_MODEL, MOE_TH), BF16),
                        pltpu.VMEM((2, D_MODEL, MOE_TH), BF16),
                        pltpu.VMEM((2, MOE_TH, D_MODEL), BF16),
                        pltpu.SemaphoreType.DMA(()),
                        pltpu.SemaphoreType.DMA((2,)),
                        pltpu.SemaphoreType.DMA((2,))],
    )
    return pl.pallas_call(
        _moe_kernel,
        grid_spec=grid_spec,
        out_shape=jax.ShapeDtypeStruct((N_TOK * TOP_K + MOE_SB, D_MODEL), F32),
        compiler_params=_cparams(("arbitrary",)),
        name="moe",
    )(sb_expert, sb_off, sb_n, n_active, src_tok, dst_row, h2, w1, b1, w2, b2)


def _final_kernel(x1_ref, y0_ref, y1_ref, y2_ref, y3_ref, gate_ref, g2_ref, lg_ref, lb_ref, o_ref):
    y = gate_ref[:, 0:1] * y0_ref[...]
    for k, y_ref in enumerate((y1_ref, y2_ref, y3_ref), start=1):
        y = y + gate_ref[:, k:k + 1] * y_ref[...]
    z = DEEPNORM_ALPHA * x1_ref[...] + g2_ref[...] * y
    o_ref[...] = _layer_norm(z) * lg_ref[...] + lb_ref[...]


def _final(x1, y4, gates, g2, ln_g, ln_b):
    tiles_per_batch = SEQ // FIN_TM
    tiles = N_TOK // FIN_TM
    rows = lambda w: pl.BlockSpec((FIN_TM, w), lambda i: (i, 0))
    plane = lambda k: pl.BlockSpec((FIN_TM, D_MODEL), lambda i: (k * tiles + i, 0))
    vec = pl.BlockSpec((1, D_MODEL), lambda i: (0, 0))
    return pl.pallas_call(
        _final_kernel,
        grid=(tiles,),
        in_specs=[rows(D_MODEL), plane(0), plane(1), plane(2), plane(3), rows(LANES),
                  pl.BlockSpec((None, 1, D_MODEL), lambda i: (i // tiles_per_batch, 0, 0)), vec, vec],
        out_specs=rows(D_MODEL),
        out_shape=jax.ShapeDtypeStruct((N_TOK, D_MODEL), F32),
        compiler_params=_cparams(("arbitrary",)),
        name="final",
    )(x1, y4, y4, y4, y4, gates, g2, ln_g, ln_b)


def _rope_tables():
    rows = SEQ // GRID_W
    t = np.arange(SEQ)
    row = (t // GRID_W - rows // 2).astype(np.float32)
    col = (t % GRID_W - GRID_W // 2).astype(np.float32)
    inv_freq = jnp.asarray(ROPE_THETA, F32) ** (-jnp.arange(0, ROPE_AXIS_DIM, 2, dtype=F32) / ROPE_AXIS_DIM)
    ang_row = jnp.asarray(row)[:, None] * inv_freq[None, :]
    ang_col = jnp.asarray(col)[:, None] * inv_freq[None, :]
    zeros = jnp.zeros_like(ang_row)
    cos = jnp.concatenate([jnp.cos(ang_row)] * 2 + [jnp.cos(ang_col)] * 2, axis=-1)
    sin_lo = jnp.concatenate([-jnp.sin(ang_row), zeros, -jnp.sin(ang_col), zeros], axis=-1)
    sin_hi = jnp.concatenate([zeros, jnp.sin(ang_row), zeros, jnp.sin(ang_col)], axis=-1)
    return cos, sin_lo, sin_hi


def _routing(top_i, rank, counts):
    counts = counts.astype(jnp.int32)
    nsb = (counts + MOE_SB - 1) // MOE_SB
    sb_end = jnp.cumsum(nsb)
    sb_start = sb_end - nsb
    g = jnp.arange(MOE_G, dtype=jnp.int32)
    active = g < sb_end[-1]
    e_of_g = jnp.minimum(jnp.sum(g[:, None] >= sb_end[None, :], axis=1), N_EXPERTS - 1).astype(jnp.int32)
    first_row = (g - sb_start[e_of_g]) * MOE_SB
    n_of_g = jnp.where(active, jnp.clip(counts[e_of_g] - first_row, 0, MOE_SB), 0).astype(jnp.int32)
    order = jnp.argsort(-n_of_g, stable=True).astype(jnp.int32)
    place = jnp.zeros((MOE_G,), jnp.int32).at[order].set(g)
    sb_n = n_of_g[order]
    sb_off = (jnp.cumsum(sb_n) - sb_n).astype(jnp.int32)
    last_e = e_of_g[order[jnp.maximum(sb_end[-1] - 1, 0)]]
    sb_expert = jnp.where(sb_n > 0, e_of_g[order], last_e).astype(jnp.int32)
    assign = jnp.arange(N_TOK * TOP_K, dtype=jnp.int32)
    max_chunks = N_TOK // MOE_SB
    chunk_ids = jnp.arange(max_chunks, dtype=jnp.int32)
    base = sb_off[place[jnp.minimum(sb_start[:, None] + chunk_ids[None, :], MOE_G - 1)]]
    is_e = top_i[:, :, None] == jnp.arange(N_EXPERTS, dtype=jnp.int32)
    is_c = (rank // MOE_SB)[:, :, None] == chunk_ids
    base_e = jnp.sum(jnp.where(is_e[:, :, :, None], base[None, None], 0), axis=2)
    dest = (jnp.sum(jnp.where(is_c, base_e, 0), axis=-1) + rank % MOE_SB).reshape(-1)
    sorted_assign = lax.sort_key_val(dest, assign)[1]
    src_tok = sorted_assign // TOP_K
    dst_row = (sorted_assign % TOP_K) * N_TOK + src_tok
    src_tok = jnp.concatenate([src_tok, jnp.zeros((MOE_SB,), jnp.int32)])
    dst_row = jnp.concatenate([dst_row, N_TOK * TOP_K + jnp.arange(MOE_SB, dtype=jnp.int32)])
    n_active = sb_end[-1:].astype(jnp.int32)
    return sb_expert, sb_off, sb_n, n_active, src_tok, dst_row


def kernel(x, c, w_ada, b_ada, w_in, q_norm_w, k_norm_w, attn_norm_w, hgrn_lb, hgrn_norm_w, w_out, ln1_g, ln1_b, w_router, b_router, w_exp_in, b_exp_in, w_exp_out, b_exp_out, ln2_g, ln2_b):
    c_pad = jnp.zeros((8, D_MODEL), F32).at[:BATCH].set(c)
    cos, sin_lo, sin_hi = _rope_tables()
    x2 = x.reshape(N_TOK, D_MODEL)
    for l in range(DEPTH):
        mod = _ada(c_pad, w_ada[l], b_ada[l][None, :])[:BATCH]
        sh1, sc1, g1, sh2, sc2, g2 = [m.reshape(BATCH, 1, D_MODEL) for m in jnp.split(mod, 6, axis=-1)]

        proj = _proj(x2, sc1, sh1, w_in[l].astype(BF16))
        o_attn = _attention(proj, cos, sin_lo, sin_hi, q_norm_w[l][None, :], k_norm_w[l][None, :],
                            attn_norm_w[l][None, :])
        lb = jnp.cumsum(jax.nn.softmax(hgrn_lb.astype(F32), axis=1), axis=1)[:, l]
        o_r = _hgrn(proj, lb.reshape(2, 1, HGRN_WIDTH), hgrn_norm_w[l][None, :])

        w_o = w_out[l].astype(BF16)
        w_rt = jnp.zeros((D_MODEL, LANES), BF16).at[:, :N_EXPERTS].set(w_router[l].astype(BF16))
        b_rt = jnp.full((1, LANES), -1e30, F32).at[0, :N_EXPERTS].set(b_router[l])
        x1, h2, idx, gates, rank, counts = _mix(
            o_attn, o_r, w_o[:ATTN_WIDTH], w_o[ATTN_WIDTH:], x2, g1, sc2, sh2,
            ln1_g[l][None, :], ln1_b[l][None, :], w_rt, b_rt)
        sb_expert, sb_off, sb_n, n_active, src_tok, dst_row = _routing(idx[:, :TOP_K], rank[:, :TOP_K],
                                                                       counts[0, :N_EXPERTS])
        y4 = _moe(sb_expert, sb_off, sb_n, n_active, src_tok, dst_row, h2, w_exp_in[l],
                  b_exp_in[l].reshape(N_EXPERTS, 2 * MOE_NJ, MOE_TH), w_exp_out[l], b_exp_out[l][:, None, :])
        x2 = _final(x1, y4, gates, g2, ln2_g[l][None, :], ln2_b[l][None, :])
    return x2.reshape(BATCH, SEQ, D_MODEL)
```

```python
import functools
import math

import numpy as np
import jax
import jax.numpy as jnp
from jax import lax
from jax.experimental import pallas as pl
from jax.experimental.pallas import tpu as pltpu

F32 = jnp.float32
BF16 = jnp.bfloat16

D_MODEL = 2048
BATCH = 4
SEQ = 2048
DEPTH = 1
N_TOK = BATCH * SEQ
HEAD_DIM = 128
ATTN_WIDTH = 1024
N_Q_HEADS = 8
N_KV_HEADS = 2
KV_GROUP = 4
HGRN_WIDTH = 1024
N_HGRN_HEADS = 8
HGRN_CHUNK = 64
GRID_W = 64
ROPE_THETA = 10000.0
ROPE_AXIS_DIM = 64
N_EXPERTS = 32
TOP_K = 4
D_EXPERT = 2048
SWIGLU_LIMIT = 7.0
SWIGLU_ALPHA = 1.702
NORM_EPS = 1e-6
DEEPNORM_ALPHA = (2 * DEPTH) ** 0.25
PROJ_WIDTH = 6656
LANES = 128

COL_Q = 0
COL_K = 8
COL_V = 10
COL_QR = 12
COL_FF = 20
COL_FB = 28
COL_IN = 36
COL_GO = 44

VMEM_LIMIT = 56 * 1024 * 1024

ADA_TN = 1024
PROJ_TM = 512
PROJ_TN = 3328
ATTN_TQ = 256
HGRN_HB = 4
HGRN_UN = 2
MIX_TM = 512
MIX_SUB = 128
MOE_ISSUE_UNROLL = 8
MOE_SB = 1024
MOE_TMI = 256
MOE_TH = 256
MOE_NJ = D_EXPERT // MOE_TH
MOE_G = N_TOK * TOP_K // MOE_SB + N_EXPERTS + 1
MOE_Q_STEP = 0
MOE_Q_TILE = (MOE_SB // (D_EXPERT // MOE_TH) - MOE_Q_STEP) // (MOE_SB // MOE_TMI)
FIN_TM = 256


def _cparams(sem):
    return pltpu.CompilerParams(dimension_semantics=sem, vmem_limit_bytes=VMEM_LIMIT)


def _sigmoid(x):
    return 1.0 / (1.0 + jnp.exp(-x))


def _layer_norm(x):
    mu = jnp.mean(x, axis=-1, keepdims=True)
    xc = x - mu
    var = jnp.mean(xc * xc, axis=-1, keepdims=True)
    return xc * lax.rsqrt(var + NORM_EPS)


def _rms(x):
    return x * lax.rsqrt(jnp.mean(x * x, axis=-1, keepdims=True) + NORM_EPS)


def _ada_kernel(c_ref, w_ref, b_ref, o_ref):
    c = c_ref[...]
    ca = c * _sigmoid(c)
    o_ref[...] = jnp.dot(ca.astype(BF16), w_ref[...].astype(BF16),
                         preferred_element_type=F32) + b_ref[...]


def _ada(c_pad, w, b):
    n = w.shape[1]
    return pl.pallas_call(
        _ada_kernel,
        grid=(n // ADA_TN,),
        in_specs=[pl.BlockSpec((8, D_MODEL), lambda j: (0, 0)),
                  pl.BlockSpec((D_MODEL, ADA_TN), lambda j: (0, j)),
                  pl.BlockSpec((1, ADA_TN), lambda j: (0, j))],
        out_specs=pl.BlockSpec((8, ADA_TN), lambda j: (0, j)),
        out_shape=jax.ShapeDtypeStruct((8, n), F32),
        compiler_params=_cparams(("arbitrary",)),
        name="ada",
    )(c_pad, w, b)


def _proj_kernel(x_ref, sc_ref, sh_ref, w_ref, o_ref, h_ref):
    @pl.when(pl.program_id(1) == 0)
    def _():
        h = _layer_norm(x_ref[...]) * (1.0 + sc_ref[...]) + sh_ref[...]
        h_ref[...] = h.astype(BF16)

    o_ref[...] = jnp.dot(h_ref[...], w_ref[...], preferred_element_type=F32).astype(BF16)


def _proj(x2, sc, sh, w_bf):
    tiles_per_batch = SEQ // PROJ_TM
    return pl.pallas_call(
        _proj_kernel,
        grid=(N_TOK // PROJ_TM, PROJ_WIDTH // PROJ_TN),
        in_specs=[pl.BlockSpec((PROJ_TM, D_MODEL), lambda i, j: (i, 0)),
                  pl.BlockSpec((None, 1, D_MODEL), lambda i, j: (i // tiles_per_batch, 0, 0)),
                  pl.BlockSpec((None, 1, D_MODEL), lambda i, j: (i // tiles_per_batch, 0, 0)),
                  pl.BlockSpec((D_MODEL, PROJ_TN), lambda i, j: (0, j))],
        out_specs=pl.BlockSpec((PROJ_TM, PROJ_TN), lambda i, j: (i, j)),
        out_shape=jax.ShapeDtypeStruct((N_TOK, PROJ_WIDTH), BF16),
        scratch_shapes=[pltpu.VMEM((PROJ_TM, D_MODEL), BF16)],
        compiler_params=_cparams(("arbitrary", "arbitrary")),
        name="proj",
    )(x2, sc, sh, w_bf)


def _rope(x, cos, sin_lo, sin_hi):
    return (x * cos + pltpu.roll(x, 96, axis=1) * sin_lo + pltpu.roll(x, 32, axis=1) * sin_hi)


def _attn_kernel(q_ref, k_ref, v_ref, cq_ref, slq_ref, shq_ref, ck_ref, slk_ref, shk_ref,
                 qw_ref, kw_ref, aw_ref, o_ref, kr_ref, v1_ref):
    @pl.when(pl.program_id(2) == 0)
    def _():
        k = _rms(k_ref[...].astype(F32)) * kw_ref[...]
        kr_ref[...] = _rope(k, ck_ref[...], slk_ref[...], shk_ref[...]).astype(BF16)
        v1_ref[:, :HEAD_DIM] = v_ref[...]
        v1_ref[:, HEAD_DIM:] = jnp.ones((SEQ, HEAD_DIM), BF16)

    scale = math.log2(math.e) / math.sqrt(HEAD_DIM)
    cq = cq_ref[...]
    slq = slq_ref[...]
    shq = shq_ref[...]
    heads = [slice(h * HEAD_DIM, (h + 1) * HEAD_DIM) for h in range(KV_GROUP)]
    qs = []
    for cols in heads:
        q = _rms(q_ref[:, cols].astype(F32)) * qw_ref[...]
        qs.append((_rope(q, cq, slq, shq) * scale).astype(BF16))
    scores = [lax.dot_general(q, kr_ref[...], (((1,), (1,)), ((), ())), preferred_element_type=F32)
              for q in qs]
    outs = []
    for s in scores:
        p = jnp.exp2((s - jnp.max(s, axis=-1, keepdims=True)).astype(BF16))
        ov = jnp.dot(p, v1_ref[...], preferred_element_type=F32)
        outs.append(ov[:, :HEAD_DIM] / ov[:, HEAD_DIM:HEAD_DIM + 1])
    for cols, o in zip(heads, outs):
        o_ref[:, cols] = (_rms(o) * aw_ref[:, cols]).astype(BF16)


def _attention(proj, cos, sin_lo, sin_hi, qw, kw, aw):
    nq = SEQ // ATTN_TQ
    gw = KV_GROUP * HEAD_DIM
    tab_q = pl.BlockSpec((ATTN_TQ, HEAD_DIM), lambda b, g, i: (i, 0))
    tab_k = pl.BlockSpec((SEQ, HEAD_DIM), lambda b, g, i: (0, 0))
    return pl.pallas_call(
        _attn_kernel,
        grid=(BATCH, N_KV_HEADS, nq),
        in_specs=[pl.BlockSpec((ATTN_TQ, gw), lambda b, g, i: (b * nq + i, g)),
                  pl.BlockSpec((SEQ, HEAD_DIM), lambda b, g, i: (b, COL_K + g)),
                  pl.BlockSpec((SEQ, HEAD_DIM), lambda b, g, i: (b, COL_V + g)),
                  tab_q, tab_q, tab_q, tab_k, tab_k, tab_k,
                  pl.BlockSpec((1, HEAD_DIM), lambda b, g, i: (0, 0)),
                  pl.BlockSpec((1, HEAD_DIM), lambda b, g, i: (0, 0)),
                  pl.BlockSpec((1, gw), lambda b, g, i: (0, g))],
        out_specs=pl.BlockSpec((ATTN_TQ, gw), lambda b, g, i: (b * nq + i, g)),
        out_shape=jax.ShapeDtypeStruct((N_TOK, ATTN_WIDTH), BF16),
        scratch_shapes=[pltpu.VMEM((SEQ, HEAD_DIM), BF16), pltpu.VMEM((SEQ, 2 * HEAD_DIM), BF16)],
        compiler_params=_cparams(("arbitrary", "arbitrary", "arbitrary")),
        name="attn",
    )(proj, proj, proj, cos, sin_lo, sin_hi, cos, sin_lo, sin_hi, qw, kw, aw)


def _hgrn_kernel(qr_ref, ff_ref, fb_ref, iv_ref, go_ref, lb_ref, nw_ref, o_ref, acc_ref, st_ref):
    C = HGRN_CHUNK
    nc = SEQ // C
    trips = nc // HGRN_UN
    row = lax.broadcasted_iota(jnp.int32, (C, C), 0)
    col = lax.broadcasted_iota(jnp.int32, (C, C), 1)
    keeps = (row >= col, row <= col)
    lasts = (C - 1, 0)
    f_refs = (ff_ref, fb_ref)

    nt_dims = (((1,), (1,)), ((), ()))
    tn_dims = (((0,), (0,)), ((), ()))
    st_ref[...] = jnp.zeros_like(st_ref)

    def trip(it, finish):
        chains = []
        for h in range(HGRN_HB):
            cols = slice(h * HEAD_DIM, (h + 1) * HEAD_DIM)
            for d in range(2):
                for u in range(HGRN_UN):
                    n = it * HGRN_UN + u
                    cidx = n if d == 0 else nc - 1 - n
                    chains.append(dict(h=h, d=d, cols=cols, rows=pl.ds(pl.multiple_of(cidx * C, C), C)))

        for ch in chains:
            d = ch["d"]
            lb = lb_ref[d, :, ch["cols"]]
            fg = lb + (1.0 - lb) * _sigmoid(f_refs[d][ch["rows"], ch["cols"]].astype(F32))
            ch["kk"] = 1.0 - fg
            lf = jnp.log(fg)
            lf_hi = lf.astype(BF16)
            lf_lo = (lf - lf_hi.astype(F32)).astype(BF16)
            tri = jnp.where(keeps[d], 1.0, 0.0).astype(BF16)
            ch["b"] = (jnp.dot(tri, lf_hi, preferred_element_type=F32)
                       + jnp.dot(tri, lf_lo, preferred_element_type=F32))
        for ch in chains:
            b = ch["b"]
            bl = b[lasts[ch["d"]]:lasts[ch["d"]] + 1, :]
            qx = qr_ref[ch["rows"], ch["cols"]].astype(F32)
            ch["qd"] = (qx * _sigmoid(qx) * jnp.exp(b)).astype(BF16)
            kd = ch["kk"] * jnp.exp(-b)
            ch["decay"] = jnp.exp(bl)
            ku = (kd * ch["decay"]).astype(BF16)
            kd = kd.astype(BF16)
            ch["v"] = iv_ref[ch["rows"], ch["cols"]]
            ch["sc"] = lax.dot_general(ch["qd"], kd, nt_dims, preferred_element_type=F32)
            ch["u_t"] = lax.dot_general(ch["v"], ku, tn_dims, preferred_element_type=F32)
        for h in range(HGRN_HB):
            for d in range(2):
                state = st_ref[2 * h + d]
                for ch in chains:
                    if ch["h"] == h and ch["d"] == d:
                        ch["state"] = state.astype(BF16)
                        state = state * ch["decay"] + ch["u_t"]
                st_ref[2 * h + d] = state
        for ch in chains:
            sc = jnp.where(keeps[ch["d"]], ch["sc"], 0.0).astype(BF16)
            ch["o"] = (jnp.dot(sc, ch["v"], preferred_element_type=F32)
                       + lax.dot_general(ch["qd"], ch["state"], nt_dims, preferred_element_type=F32))
        for ch in chains:
            rows, cols = ch["rows"], ch["cols"]
            if finish:
                o = _rms(acc_ref[rows, cols] + ch["o"]) * nw_ref[:, cols]
                g = go_ref[rows, cols].astype(F32)
                o_ref[rows, cols] = (o * (g * _sigmoid(g))).astype(BF16)
            else:
                acc_ref[rows, cols] = ch["o"]

    def first_half(it, carry):
        trip(it, False)
        return carry

    def second_half(it, carry):
        trip(it, True)
        return carry

    lax.fori_loop(0, trips // 2, first_half, 0)
    lax.fori_loop(trips // 2, trips, second_half, 0)


def _hgrn(proj, lb, nw):
    width = HGRN_HB * HEAD_DIM

    def col(c0):
        return pl.BlockSpec((SEQ, width), lambda b, h: (b, c0 // HGRN_HB + h))

    return pl.pallas_call(
        _hgrn_kernel,
        grid=(BATCH, N_HGRN_HEADS // HGRN_HB),
        in_specs=[col(COL_QR), col(COL_FF), col(COL_FB), col(COL_IN), col(COL_GO),
                  pl.BlockSpec((2, 1, width), lambda b, h: (0, 0, h)),
                  pl.BlockSpec((1, width), lambda b, h: (0, h))],
        out_specs=pl.BlockSpec((SEQ, width), lambda b, h: (b, h)),
        out_shape=jax.ShapeDtypeStruct((N_TOK, HGRN_WIDTH), BF16),
        scratch_shapes=[pltpu.VMEM((SEQ, width), F32),
                        pltpu.VMEM((2 * HGRN_HB, HEAD_DIM, HEAD_DIM), F32)],
        compiler_params=_cparams(("arbitrary", "arbitrary")),
        name="hgrn",
    )(proj, proj, proj, proj, proj, lb, nw)


def _mix_kernel(oa_ref, or_ref, wa_ref, wr_ref, x_ref, g1_ref, sc_ref, sh_ref, lg_ref, lbias_ref,
                wrt_ref, brt_ref, x1_ref, h2_ref, idx_ref, gate_ref, rank_ref, cnt_ref, carry_ref):
    i = pl.program_id(0)

    @pl.when(i == 0)
    def _():
        carry_ref[...] = jnp.zeros_like(carry_ref)

    tm = MIX_SUB
    subs = [slice(u * tm, (u + 1) * tm) for u in range(MIX_TM // MIX_SUB)]
    lane = lax.broadcasted_iota(jnp.int32, (tm, LANES), 1)
    neg = jnp.float32(-jnp.inf)

    ys = [jnp.dot(oa_ref[rs, :], wa_ref[...], preferred_element_type=F32)
          + jnp.dot(or_ref[rs, :], wr_ref[...], preferred_element_type=F32) for rs in subs]
    h2s = []
    for rs, y in zip(subs, ys):
        x1 = _layer_norm(DEEPNORM_ALPHA * x_ref[rs, :] + g1_ref[...] * y) * lg_ref[...] + lbias_ref[...]
        x1_ref[rs, :] = x1
        h2 = _layer_norm(x1) * (1.0 + sc_ref[...]) + sh_ref[...]
        h2_ref[rs, :] = h2
        h2s.append(h2.astype(BF16))
    logit_list = [jnp.dot(h2, wrt_ref[...], preferred_element_type=F32) + brt_ref[...] for h2 in h2s]

    picks = []
    for work in logit_list:
        vals, sels = [], []
        for _ in range(TOP_K):
            m = jnp.max(work, axis=-1, keepdims=True)
            sel = jnp.min(jnp.where(work == m, lane, LANES), axis=-1, keepdims=True)
            vals.append(m)
            sels.append(sel)
            work = jnp.where(lane == sel, neg, work)
        es = [jnp.exp(v - vals[0]) for v in vals]
        multi = jnp.zeros((tm, LANES), F32)
        for sel in sels:
            multi = multi + jnp.where(lane == sel, 1.0, 0.0)
        picks.append((sels, es, es[0] + es[1] + es[2] + es[3], multi))

    r = lax.broadcasted_iota(jnp.int32, (tm, tm), 0)
    c = lax.broadcasted_iota(jnp.int32, (tm, tm), 1)
    strict = jnp.where(r > c, 1.0, 0.0).astype(BF16)
    within = [jnp.dot(strict, multi.astype(BF16), preferred_element_type=F32) for _, _, _, multi in picks]
    carry = carry_ref[...]
    for rs, (sels, es, denom, multi), inside in zip(subs, picks, within):
        before = inside + carry
        carry = carry + jnp.sum(multi, axis=0, keepdims=True)
        idx_out = jnp.zeros((tm, LANES), jnp.int32)
        gate_out = jnp.zeros((tm, LANES), F32)
        rank_out = jnp.zeros((tm, LANES), F32)
        for k in range(TOP_K):
            rk = jnp.sum(jnp.where(lane == sels[k], before, 0.0), axis=-1, keepdims=True)
            idx_out = jnp.where(lane == k, sels[k], idx_out)
            gate_out = jnp.where(lane == k, es[k] / denom, gate_out)
            rank_out = jnp.where(lane == k, rk, rank_out)
        idx_ref[rs, :] = idx_out
        gate_ref[rs, :] = gate_out
        rank_ref[rs, :] = rank_out.astype(jnp.int32)
    carry_ref[...] = carry
    cnt_ref[...] = carry


def _mix(o_attn, o_r, wa, wr, x2, g1, sc2, sh2, ln_g, ln_b, w_rt, b_rt):
    tiles_per_batch = SEQ // MIX_TM
    rows = lambda w: pl.BlockSpec((MIX_TM, w), lambda i: (i, 0))
    full = lambda a, b: pl.BlockSpec((a, b), lambda i: (0, 0))
    per_batch = pl.BlockSpec((None, 1, D_MODEL), lambda i: (i // tiles_per_batch, 0, 0))
    return pl.pallas_call(
        _mix_kernel,
        grid=(N_TOK // MIX_TM,),
        in_specs=[rows(ATTN_WIDTH), rows(HGRN_WIDTH), full(ATTN_WIDTH, D_MODEL), full(HGRN_WIDTH, D_MODEL),
                  rows(D_MODEL), per_batch, per_batch, per_batch, full(1, D_MODEL), full(1, D_MODEL),
                  full(D_MODEL, LANES), full(1, LANES)],
        out_specs=[rows(D_MODEL), rows(D_MODEL), rows(LANES), rows(LANES), rows(LANES), full(1, LANES)],
        out_shape=[jax.ShapeDtypeStruct((N_TOK, D_MODEL), F32),
                   jax.ShapeDtypeStruct((N_TOK, D_MODEL), F32),
                   jax.ShapeDtypeStruct((N_TOK, LANES), jnp.int32),
                   jax.ShapeDtypeStruct((N_TOK, LANES), F32),
                   jax.ShapeDtypeStruct((N_TOK, LANES), jnp.int32),
                   jax.ShapeDtypeStruct((1, LANES), F32)],
        scratch_shapes=[pltpu.VMEM((1, LANES), F32)],
        compiler_params=_cparams(("arbitrary",)),
        name="mix",
    )(o_attn, o_r, wa, wr, x2, g1, sc2, sh2, ln_g, ln_b, w_rt, b_rt)


def _wait_rows(n, make_copy):
    for bit in range(MOE_SB.bit_length()):
        @pl.when(((n >> bit) & 1) == 1)
        def _(bit=bit):
            make_copy(1 << bit).wait()


def _for_rows(lo, hi, fn):
    groups = (hi - lo) // MOE_ISSUE_UNROLL

    def group(q, carry):
        base = lo + q * MOE_ISSUE_UNROLL
        for u in range(MOE_ISSUE_UNROLL):
            fn(base + u)
        return carry

    def single(r, carry):
        fn(r)
        return carry

    lax.fori_loop(0, groups, group, 0)
    lax.fori_loop(lo + groups * MOE_ISSUE_UNROLL, hi, single, 0)


def _moe_kernel(se_ref, so_ref, sn_ref, na_ref, src_ref, dst_ref, h2_hbm, w1_hbm, b1_ref, w2_hbm, b2_ref,
                y_hbm, xg_ref, xb_ref, acc_ref, wf1g_ref, wf1l_ref, wf2_ref, wb1g_ref, wb1l_ref, wb2_ref,
                gsem, ssem, wsem):
    g = pl.program_id(0)
    nj = MOE_NJ
    n_assign = N_TOK * TOP_K
    n_slices = na_ref[0] * nj

    def weight_copies(t, s):
        e = se_ref[jnp.minimum(t // nj, MOE_G - 1)]
        col = pl.multiple_of((t % nj) * MOE_TH, MOE_TH)
        return (pltpu.make_async_copy(w1_hbm.at[e, :, pl.ds(col, MOE_TH)], wf1g_ref.at[s], wsem.at[s]),
                pltpu.make_async_copy(w1_hbm.at[e, :, pl.ds(D_EXPERT + col, MOE_TH)], wf1l_ref.at[s], wsem.at[s]),
                pltpu.make_async_copy(w2_hbm.at[e, pl.ds(col, MOE_TH), :], wf2_ref.at[s], wsem.at[s]))

    def cast_weights(s):
        wb1g_ref[s] = wf1g_ref[s].astype(BF16)
        wb1l_ref[s] = wf1l_ref[s].astype(BF16)
        wb2_ref[s] = wf2_ref[s].astype(BF16)

    def tiles_of(rows):
        return (rows + MOE_TMI - 1) // MOE_TMI

    n = sn_ref[g]
    nt = tiles_of(n)
    slot = g % 2
    g_next = jnp.minimum(g + 1, MOE_G - 1)
    n_next = jnp.where(g + 1 < MOE_G, sn_ref[g_next], 0)
    off_next = so_ref[g_next]
    g_prev = jnp.maximum(g - 1, 0)
    n_prev = jnp.where(g >= 1, sn_ref[g_prev], 0)
    off_prev = so_ref[g_prev]
    step_rows = MOE_Q_STEP + nt * MOE_Q_TILE
    eager = jnp.where(nt > 0, nj * step_rows, 0)

    def gather_copy(tok, r):
        return pltpu.make_async_copy(h2_hbm.at[pl.ds(tok, 1)], xg_ref.at[pl.ds(r, 1)], gsem)

    def scatter_copy(s, r, a):
        return pltpu.make_async_copy(acc_ref.at[s, pl.ds(r, 1)], y_hbm.at[pl.ds(a, 1)], ssem.at[s])

    def gather_start(r):
        gather_copy(src_ref[off_next + r], r).start()

    def scatter_start(r):
        scatter_copy(1 - slot, r, dst_ref[off_prev + r]).start()

    def eager_issue(first, count):
        for q in range(count):
            gather_start(first + q)
            scatter_start(first + q)

    def tile_rows(i):
        return pl.ds(pl.multiple_of(i * MOE_TMI, MOE_TMI), MOE_TMI)

    def begin():
        @pl.when(g == 0)
        def _():
            xg_ref[...] = jnp.zeros_like(xg_ref)
            acc_ref[...] = jnp.zeros_like(acc_ref)
            spare = pltpu.make_async_copy(acc_ref.at[0], y_hbm.at[pl.ds(n_assign, MOE_SB)], ssem.at[0])
            spare.start()
            spare.wait()
            _for_rows(0, n, lambda r: gather_copy(src_ref[so_ref[0] + r], r).start())
            for t in range(2):
                @pl.when(t < n_slices)
                def _(t=t):
                    for c in weight_copies(t, t):
                        c.start()

            @pl.when(n_slices > 0)
            def _():
                for c in weight_copies(0, 0):
                    c.wait()
                cast_weights(0)

        nt_prev = tiles_of(n_prev)
        eager_prev = jnp.where(nt_prev > 0, nj * (MOE_Q_STEP + nt_prev * MOE_Q_TILE), 0)
        n_prev2 = jnp.where(g >= 2, sn_ref[jnp.maximum(g - 2, 0)], 0)
        gathered = jnp.maximum(eager_prev, n)
        scattered = jnp.maximum(eager_prev, n_prev2)
        _wait_rows(gathered,
                   lambda k: pltpu.make_async_copy(h2_hbm.at[pl.ds(0, k)], xg_ref.at[pl.ds(0, k)], gsem))
        _wait_rows(scattered,
                   lambda k: pltpu.make_async_copy(acc_ref.at[slot, pl.ds(0, k)], y_hbm.at[pl.ds(0, k)],
                                                   ssem.at[slot]))

        def prep(i, carry):
            rows = tile_rows(i)
            xb_ref[rows, :] = xg_ref[rows, :].astype(BF16)
            acc_ref[slot, rows, :] = jnp.broadcast_to(b2_ref[...], (MOE_TMI, D_MODEL))
            return carry

        lax.fori_loop(0, nt, prep, 0)

    begin()

    def hidden_slice(j, ws):
        t = g * nj + j

        @pl.when(t + 1 < n_slices)
        def _():
            for c in weight_copies(t + 1, 1 - ws):
                c.wait()

        @pl.when(t + 2 < n_slices)
        def _():
            for c in weight_copies(t + 2, ws):
                c.start()

        b1g = b1_ref[pl.ds(j, 1), :]
        b1l = b1_ref[pl.ds(nj + j, 1), :]

        def tiles(first_tile, count, cast_next):
            first_row = pl.multiple_of(j * step_rows + first_tile * MOE_Q_TILE, MOE_Q_TILE)
            if cast_next:
                eager_issue(first_row, MOE_Q_STEP + count * MOE_Q_TILE)
            else:
                eager_issue(first_row + MOE_Q_STEP, count * MOE_Q_TILE)
            rows = [tile_rows(first_tile + i) for i in range(count)]
            xs = [xb_ref[r, :] for r in rows]
            hid = [(jnp.dot(x, wb1g_ref[ws], preferred_element_type=F32) + b1g,
                    jnp.dot(x, wb1l_ref[ws], preferred_element_type=F32) + b1l) for x in xs]
            if cast_next:
                cast_weights(1 - ws)
            parts = []
            for hg, hl in hid:
                hg = jnp.minimum(hg, SWIGLU_LIMIT)
                hl = jnp.clip(hl, -SWIGLU_LIMIT, SWIGLU_LIMIT)
                act = hg * _sigmoid(SWIGLU_ALPHA * hg) * (hl + 1.0)
                parts.append(jnp.dot(act.astype(BF16), wb2_ref[ws], preferred_element_type=F32))
            for r, part in zip(rows, parts):
                acc_ref[slot, r, :] += part

        @pl.when(nt == 4)
        def _():
            tiles(0, 4, True)

        @pl.when((nt == 2) | (nt == 3))
        def _():
            tiles(0, 2, True)

        @pl.when(nt == 1)
        def _():
            tiles(0, 1, True)

        @pl.when(nt == 3)
        def _():
            tiles(2, 1, False)

    def slice_pair(jj, carry):
        for ws in range(2):
            hidden_slice(2 * jj + ws, ws)
        return carry

    @pl.when(nt > 0)
    def _():
        lax.fori_loop(0, nj // 2, slice_pair, 0)

    _for_rows(jnp.minimum(eager, n_next), n_next, gather_start)
    _for_rows(jnp.minimum(eager, n_prev), n_prev, scatter_start)

    @pl.when(g == MOE_G - 1)
    def _():
        _wait_rows(n_prev, lambda k: pltpu.make_async_copy(acc_ref.at[1 - slot, pl.ds(0, k)],
                                                            y_hbm.at[pl.ds(0, k)], ssem.at[1 - slot]))


def _moe(sb_expert, sb_off, sb_n, n_active, src_tok, dst_row, h2, w1, b1, w2, b2):
    grid_spec = pltpu.PrefetchScalarGridSpec(
        num_scalar_prefetch=6,
        grid=(MOE_G,),
        in_specs=[
            pl.BlockSpec(memory_space=pl.ANY),
            pl.BlockSpec(memory_space=pl.ANY),
            pl.BlockSpec((None, 2 * MOE_NJ, MOE_TH), lambda g, se, *_: (se[g], 0, 0)),
            pl.BlockSpec(memory_space=pl.ANY),
            pl.BlockSpec((None, 1, D_MODEL), lambda g, se, *_: (se[g], 0, 0)),
        ],
        out_specs=pl.BlockSpec(memory_space=pl.ANY),
        scratch_shapes=[pltpu.VMEM((MOE_SB, D_MODEL), F32),
                        pltpu.VMEM((MOE_SB, D_MODEL), BF16),
                        pltpu.VMEM((2, MOE_SB, D_MODEL), F32),
                        pltpu.VMEM((2, D_MODEL, MOE_TH), F32),
                        pltpu.VMEM((2, D_MODEL, MOE_TH), F32),
                        pltpu.VMEM((2, MOE_TH, D_MODEL), F32),
                        pltpu.VMEM((2, D_MODEL, MOE_TH), BF16),
                        pltpu.VMEM((2, D_MODEL, MOE_TH), BF16),
                        pltpu.VMEM((2, MOE_TH, D_MODEL), BF16),
                        pltpu.SemaphoreType.DMA(()),
                        pltpu.SemaphoreType.DMA((2,)),
                        pltpu.SemaphoreType.DMA((2,))],
    )
    return pl.pallas_call(
        _moe_kernel,
        grid_spec=grid_spec,
        out_shape=jax.ShapeDtypeStruct((N_TOK * TOP_K + MOE_SB, D_MODEL), F32),
        compiler_params=_cparams(("arbitrary",)),
        name="moe",
    )(sb_expert, sb_off, sb_n, n_active, src_tok, dst_row, h2, w1, b1, w2, b2)


def _final_kernel(x1_ref, y0_ref, y1_ref, y2_ref, y3_ref, gate_ref, g2_ref, lg_ref, lb_ref, o_ref):
    y = gate_ref[:, 0:1] * y0_ref[...]
    for k, y_ref in enumerate((y1_ref, y2_ref, y3_ref), start=1):
        y = y + gate_ref[:, k:k + 1] * y_ref[...]
    z = DEEPNORM_ALPHA * x1_ref[...] + g2_ref[...] * y
    o_ref[...] = _layer_norm(z) * lg_ref[...] + lb_ref[...]


def _final(x1, y4, gates, g2, ln_g, ln_b):
    tiles_per_batch = SEQ // FIN_TM
    tiles = N_TOK // FIN_TM
    rows = lambda w: pl.BlockSpec((FIN_TM, w), lambda i: (i, 0))
    plane = lambda k: pl.BlockSpec((FIN_TM, D_MODEL), lambda i: (k * tiles + i, 0))
    vec = pl.BlockSpec((1, D_MODEL), lambda i: (0, 0))
    return pl.pallas_call(
        _final_kernel,
        grid=(tiles,),
        in_specs=[rows(D_MODEL), plane(0), plane(1), plane(2), plane(3), rows(LANES),
                  pl.BlockSpec((None, 1, D_MODEL), lambda i: (i // tiles_per_batch, 0, 0)), vec, vec],
        out_specs=rows(D_MODEL),
        out_shape=jax.ShapeDtypeStruct((N_TOK, D_MODEL), F32),
        compiler_params=_cparams(("arbitrary",)),
        name="final",
    )(x1, y4, y4, y4, y4, gates, g2, ln_g, ln_b)


def _rope_tables():
    rows = SEQ // GRID_W
    t = np.arange(SEQ)
    row = (t // GRID_W - rows // 2).astype(np.float32)
    col = (t % GRID_W - GRID_W // 2).astype(np.float32)
    inv_freq = jnp.asarray(ROPE_THETA, F32) ** (-jnp.arange(0, ROPE_AXIS_DIM, 2, dtype=F32) / ROPE_AXIS_DIM)
    ang_row = jnp.asarray(row)[:, None] * inv_freq[None, :]
    ang_col = jnp.asarray(col)[:, None] * inv_freq[None, :]
    zeros = jnp.zeros_like(ang_row)
    cos = jnp.concatenate([jnp.cos(ang_row)] * 2 + [jnp.cos(ang_col)] * 2, axis=-1)
    sin_lo = jnp.concatenate([-jnp.sin(ang_row), zeros, -jnp.sin(ang_col), zeros], axis=-1)
    sin_hi = jnp.concatenate([zeros, jnp.sin(ang_row), zeros, jnp.sin(ang_col)], axis=-1)
    return cos, sin_lo, sin_hi


def _routing(top_i, rank, counts):
    counts = counts.astype(jnp.int32)
    nsb = (counts + MOE_SB - 1) // MOE_SB
    sb_end = jnp.cumsum(nsb)
    sb_start = sb_end - nsb
    g = jnp.arange(MOE_G, dtype=jnp.int32)
    active = g < sb_end[-1]
    e_of_g = jnp.minimum(jnp.sum(g[:, None] >= sb_end[None, :], axis=1), N_EXPERTS - 1).astype(jnp.int32)
    first_row = (g - sb_start[e_of_g]) * MOE_SB
    n_of_g = jnp.where(active, jnp.clip(counts[e_of_g] - first_row, 0, MOE_SB), 0).astype(jnp.int32)
    order = jnp.argsort(-n_of_g, stable=True).astype(jnp.int32)
    place = jnp.zeros((MOE_G,), jnp.int32).at[order].set(g)
    sb_n = n_of_g[order]
    sb_off = (jnp.cumsum(sb_n) - sb_n).astype(jnp.int32)
    last_e = e_of_g[order[jnp.maximum(sb_end[-1] - 1, 0)]]
    sb_expert = jnp.where(sb_n > 0, e_of_g[order], last_e).astype(jnp.int32)
    assign = jnp.arange(N_TOK * TOP_K, dtype=jnp.int32)
    max_chunks = N_TOK // MOE_SB
    chunk_ids = jnp.arange(max_chunks, dtype=jnp.int32)
    base = sb_off[place[jnp.minimum(sb_start[:, None] + chunk_ids[None, :], MOE_G - 1)]]
    is_e = top_i[:, :, None] == jnp.arange(N_EXPERTS, dtype=jnp.int32)
    is_c = (rank // MOE_SB)[:, :, None] == chunk_ids
    base_e = jnp.sum(jnp.where(is_e[:, :, :, None], base[None, None], 0), axis=2)
    dest = (jnp.sum(jnp.where(is_c, base_e, 0), axis=-1) + rank % MOE_SB).reshape(-1)
    sorted_assign = lax.sort_key_val(dest, assign)[1]
    src_tok = sorted_assign // TOP_K
    dst_row = (sorted_assign % TOP_K) * N_TOK + src_tok
    src_tok = jnp.concatenate([src_tok, jnp.zeros((MOE_SB,), jnp.int32)])
    dst_row = jnp.concatenate([dst_row, N_TOK * TOP_K + jnp.arange(MOE_SB, dtype=jnp.int32)])
    n_active = sb_end[-1:].astype(jnp.int32)
    return sb_expert, sb_off, sb_n, n_active, src_tok, dst_row


def kernel(x, c, w_ada, b_ada, w_in, q_norm_w, k_norm_w, attn_norm_w, hgrn_lb, hgrn_norm_w, w_out, ln1_g, ln1_b, w_router, b_router, w_exp_in, b_exp_in, w_exp_out, b_exp_out, ln2_g, ln2_b):
    c_pad = jnp.zeros((8, D_MODEL), F32).at[:BATCH].set(c)
    cos, sin_lo, sin_hi = _rope_tables()
    x2 = x.reshape(N_TOK, D_MODEL)
    for l in range(DEPTH):
        mod = _ada(c_pad, w_ada[l], b_ada[l][None, :])[:BATCH]
        sh1, sc1, g1, sh2, sc2, g2 = [m.reshape(BATCH, 1, D_MODEL) for m in jnp.split(mod, 6, axis=-1)]

        proj = _proj(x2, sc1, sh1, w_in[l].astype(BF16))
        o_attn = _attention(proj, cos, sin_lo, sin_hi, q_norm_w[l][None, :], k_norm_w[l][None, :],
                            attn_norm_w[l][None, :])
        lb = jnp.cumsum(jax.nn.softmax(hgrn_lb.astype(F32), axis=1), axis=1)[:, l]
        o_r = _hgrn(proj, lb.reshape(2, 1, HGRN_WIDTH), hgrn_norm_w[l][None, :])

        w_o = w_out[l].astype(BF16)
        w_rt = jnp.zeros((D_MODEL, LANES), BF16).at[:, :N_EXPERTS].set(w_router[l].astype(BF16))
        b_rt = jnp.full((1, LANES), -1e30, F32).at[0, :N_EXPERTS].set(b_router[l])
        x1, h2, idx, gates, rank, counts = _mix(
            o_attn, o_r, w_o[:ATTN_WIDTH], w_o[ATTN_WIDTH:], x2, g1, sc2, sh2,
            ln1_g[l][None, :], ln1_b[l][None, :], w_rt, b_rt)
        sb_expert, sb_off, sb_n, n_active, src_tok, dst_row = _routing(idx[:, :TOP_K], rank[:, :TOP_K],
                                                                       counts[0, :N_EXPERTS])
        y4 = _moe(sb_expert, sb_off, sb_n, n_active, src_tok, dst_row, h2, w_exp_in[l],
                  b_exp_in[l].reshape(N_EXPERTS, 2 * MOE_NJ, MOE_TH), w_exp_out[l], b_exp_out[l][:, None, :])
        x2 = _final(x1, y4, gates, g2, ln2_g[l][None, :], ln2_b[l][None, :])
    return x2.reshape(BATCH, SEQ, D_MODEL)
```

```python
import functools
import math

import numpy as np
import jax
import jax.numpy as jnp
from jax import lax
from jax.experimental import pallas as pl
from jax.experimental.pallas import tpu as pltpu

F32 = jnp.float32
BF16 = jnp.bfloat16

D_MODEL = 2048
BATCH = 4
SEQ = 2048
DEPTH = 1
N_TOK = BATCH * SEQ
HEAD_DIM = 128
ATTN_WIDTH = 1024
N_Q_HEADS = 8
N_KV_HEADS = 2
KV_GROUP = 4
HGRN_WIDTH = 1024
N_HGRN_HEADS = 8
HGRN_CHUNK = 64
GRID_W = 64
ROPE_THETA = 10000.0
ROPE_AXIS_DIM = 64
N_EXPERTS = 32
TOP_K = 4
D_EXPERT = 2048
SWIGLU_LIMIT = 7.0
SWIGLU_ALPHA = 1.702
NORM_EPS = 1e-6
DEEPNORM_ALPHA = (2 * DEPTH) ** 0.25
PROJ_WIDTH = 6656
LANES = 128

COL_Q = 0
COL_K = 8
COL_V = 10
COL_QR = 12
COL_FF = 20
COL_FB = 28
COL_IN = 36
COL_GO = 44

VMEM_LIMIT = 56 * 1024 * 1024

ADA_TN = 1024
PROJ_TM = 512
PROJ_TN = 3328
ATTN_TQ = 256
HGRN_HB = 4
HGRN_UN = 2
MIX_TM = 512
MIX_SUB = 128
MOE_ISSUE_UNROLL = 8
MOE_SB = 1024
MOE_TMI = 256
MOE_TH = 256
MOE_NJ = D_EXPERT // MOE_TH
MOE_G = N_TOK * TOP_K // MOE_SB + N_EXPERTS + 1
MOE_Q_TILE = MOE_SB // (MOE_NJ // 2) // (MOE_SB // MOE_TMI)
FIN_TM = 256


def _cparams(sem):
    return pltpu.CompilerParams(dimension_semantics=sem, vmem_limit_bytes=VMEM_LIMIT)


def _sigmoid(x):
    return 1.0 / (1.0 + jnp.exp(-x))


def _layer_norm(x):
    mu = jnp.mean(x, axis=-1, keepdims=True)
    xc = x - mu
    var = jnp.mean(xc * xc, axis=-1, keepdims=True)
    return xc * lax.rsqrt(var + NORM_EPS)


def _rms(x):
    return x * lax.rsqrt(jnp.mean(x * x, axis=-1, keepdims=True) + NORM_EPS)


def _ada_kernel(c_ref, w_ref, b_ref, o_ref):
    c = c_ref[...]
    ca = c * _sigmoid(c)
    o_ref[...] = jnp.dot(ca.astype(BF16), w_ref[...].astype(BF16),
                         preferred_element_type=F32) + b_ref[...]


def _ada(c_pad, w, b):
    n = w.shape[1]
    return pl.pallas_call(
        _ada_kernel,
        grid=(n // ADA_TN,),
        in_specs=[pl.BlockSpec((8, D_MODEL), lambda j: (0, 0)),
                  pl.BlockSpec((D_MODEL, ADA_TN), lambda j: (0, j)),
                  pl.BlockSpec((1, ADA_TN), lambda j: (0, j))],
        out_specs=pl.BlockSpec((8, ADA_TN), lambda j: (0, j)),
        out_shape=jax.ShapeDtypeStruct((8, n), F32),
        compiler_params=_cparams(("arbitrary",)),
        name="ada",
    )(c_pad, w, b)


def _proj_kernel(x_ref, sc_ref, sh_ref, w_ref, o_ref, h_ref):
    @pl.when(pl.program_id(1) == 0)
    def _():
        h = _layer_norm(x_ref[...]) * (1.0 + sc_ref[...]) + sh_ref[...]
        h_ref[...] = h.astype(BF16)

    o_ref[...] = jnp.dot(h_ref[...], w_ref[...], preferred_element_type=F32).astype(BF16)


def _proj(x2, sc, sh, w_bf):
    tiles_per_batch = SEQ // PROJ_TM
    return pl.pallas_call(
        _proj_kernel,
        grid=(N_TOK // PROJ_TM, PROJ_WIDTH // PROJ_TN),
        in_specs=[pl.BlockSpec((PROJ_TM, D_MODEL), lambda i, j: (i, 0)),
                  pl.BlockSpec((None, 1, D_MODEL), lambda i, j: (i // tiles_per_batch, 0, 0)),
                  pl.BlockSpec((None, 1, D_MODEL), lambda i, j: (i // tiles_per_batch, 0, 0)),
                  pl.BlockSpec((D_MODEL, PROJ_TN), lambda i, j: (0, j))],
        out_specs=pl.BlockSpec((PROJ_TM, PROJ_TN), lambda i, j: (i, j)),
        out_shape=jax.ShapeDtypeStruct((N_TOK, PROJ_WIDTH), BF16),
        scratch_shapes=[pltpu.VMEM((PROJ_TM, D_MODEL), BF16)],
        compiler_params=_cparams(("arbitrary", "arbitrary")),
        name="proj",
    )(x2, sc, sh, w_bf)


def _rope(x, cos, sin_lo, sin_hi):
    return (x * cos + pltpu.roll(x, 96, axis=1) * sin_lo + pltpu.roll(x, 32, axis=1) * sin_hi)


def _attn_kernel(q_ref, k_ref, v_ref, cq_ref, slq_ref, shq_ref, ck_ref, slk_ref, shk_ref,
                 qw_ref, kw_ref, aw_ref, o_ref, kr_ref, v1_ref):
    @pl.when(pl.program_id(2) == 0)
    def _():
        k = _rms(k_ref[...].astype(F32)) * kw_ref[...]
        kr_ref[...] = _rope(k, ck_ref[...], slk_ref[...], shk_ref[...]).astype(BF16)
        v1_ref[:, :HEAD_DIM] = v_ref[...]
        v1_ref[:, HEAD_DIM:] = jnp.ones((SEQ, HEAD_DIM), BF16)

    scale = math.log2(math.e) / math.sqrt(HEAD_DIM)
    cq = cq_ref[...]
    slq = slq_ref[...]
    shq = shq_ref[...]
    heads = [slice(h * HEAD_DIM, (h + 1) * HEAD_DIM) for h in range(KV_GROUP)]
    qs = []
    for cols in heads:
        q = _rms(q_ref[:, cols].astype(F32)) * qw_ref[...]
        qs.append((_rope(q, cq, slq, shq) * scale).astype(BF16))
    scores = [lax.dot_general(q, kr_ref[...], (((1,), (1,)), ((), ())), preferred_element_type=F32)
              for q in qs]
    outs = []
    for s in scores:
        p = jnp.exp2((s - jnp.max(s, axis=-1, keepdims=True)).astype(BF16))
        ov = jnp.dot(p, v1_ref[...], preferred_element_type=F32)
        outs.append(ov[:, :HEAD_DIM] / ov[:, HEAD_DIM:HEAD_DIM + 1])
    for cols, o in zip(heads, outs):
        o_ref[:, cols] = (_rms(o) * aw_ref[:, cols]).astype(BF16)


def _attention(proj, cos, sin_lo, sin_hi, qw, kw, aw):
    nq = SEQ // ATTN_TQ
    gw = KV_GROUP * HEAD_DIM
    tab_q = pl.BlockSpec((ATTN_TQ, HEAD_DIM), lambda b, g, i: (i, 0))
    tab_k = pl.BlockSpec((SEQ, HEAD_DIM), lambda b, g, i: (0, 0))
    return pl.pallas_call(
        _attn_kernel,
        grid=(BATCH, N_KV_HEADS, nq),
        in_specs=[pl.BlockSpec((ATTN_TQ, gw), lambda b, g, i: (b * nq + i, g)),
                  pl.BlockSpec((SEQ, HEAD_DIM), lambda b, g, i: (b, COL_K + g)),
                  pl.BlockSpec((SEQ, HEAD_DIM), lambda b, g, i: (b, COL_V + g)),
                  tab_q, tab_q, tab_q, tab_k, tab_k, tab_k,
                  pl.BlockSpec((1, HEAD_DIM), lambda b, g, i: (0, 0)),
                  pl.BlockSpec((1, HEAD_DIM), lambda b, g, i: (0, 0)),
                  pl.BlockSpec((1, gw), lambda b, g, i: (0, g))],
        out_specs=pl.BlockSpec((ATTN_TQ, gw), lambda b, g, i: (b * nq + i, g)),
        out_shape=jax.ShapeDtypeStruct((N_TOK, ATTN_WIDTH), BF16),
        scratch_shapes=[pltpu.VMEM((SEQ, HEAD_DIM), BF16), pltpu.VMEM((SEQ, 2 * HEAD_DIM), BF16)],
        compiler_params=_cparams(("arbitrary", "arbitrary", "arbitrary")),
        name="attn",
    )(proj, proj, proj, cos, sin_lo, sin_hi, cos, sin_lo, sin_hi, qw, kw, aw)


def _hgrn_kernel(qr_ref, ff_ref, fb_ref, iv_ref, go_ref, lb_ref, nw_ref, o_ref, acc_ref, st_ref):
    C = HGRN_CHUNK
    nc = SEQ // C
    trips = nc // HGRN_UN
    row = lax.broadcasted_iota(jnp.int32, (C, C), 0)
    col = lax.broadcasted_iota(jnp.int32, (C, C), 1)
    keeps = (row >= col, row <= col)
    lasts = (C - 1, 0)
    f_refs = (ff_ref, fb_ref)

    nt_dims = (((1,), (1,)), ((), ()))
    tn_dims = (((0,), (0,)), ((), ()))
    st_ref[...] = jnp.zeros_like(st_ref)

    def trip(it, finish):
        chains = []
        for h in range(HGRN_HB):
            cols = slice(h * HEAD_DIM, (h + 1) * HEAD_DIM)
            for d in range(2):
                for u in range(HGRN_UN):
                    n = it * HGRN_UN + u
                    cidx = n if d == 0 else nc - 1 - n
                    chains.append(dict(h=h, d=d, cols=cols, rows=pl.ds(pl.multiple_of(cidx * C, C), C)))

        for ch in chains:
            d = ch["d"]
            lb = lb_ref[d, :, ch["cols"]]
            fg = lb + (1.0 - lb) * _sigmoid(f_refs[d][ch["rows"], ch["cols"]].astype(F32))
            ch["kk"] = 1.0 - fg
            lf = jnp.log(fg)
            lf_hi = lf.astype(BF16)
            lf_lo = (lf - lf_hi.astype(F32)).astype(BF16)
            tri = jnp.where(keeps[d], 1.0, 0.0).astype(BF16)
            ch["b"] = (jnp.dot(tri, lf_hi, preferred_element_type=F32)
                       + jnp.dot(tri, lf_lo, preferred_element_type=F32))
        for ch in chains:
            b = ch["b"]
            bl = b[lasts[ch["d"]]:lasts[ch["d"]] + 1, :]
            qx = qr_ref[ch["rows"], ch["cols"]].astype(F32)
            ch["qd"] = (qx * _sigmoid(qx) * jnp.exp(b)).astype(BF16)
            kd = ch["kk"] * jnp.exp(-b)
            ch["decay"] = jnp.exp(bl)
            ku = (kd * ch["decay"]).astype(BF16)
            kd = kd.astype(BF16)
            ch["v"] = iv_ref[ch["rows"], ch["cols"]]
            ch["sc"] = lax.dot_general(ch["qd"], kd, nt_dims, preferred_element_type=F32)
            ch["u_t"] = lax.dot_general(ch["v"], ku, tn_dims, preferred_element_type=F32)
        for h in range(HGRN_HB):
            for d in range(2):
                state = st_ref[2 * h + d]
                for ch in chains:
                    if ch["h"] == h and ch["d"] == d:
                        ch["state"] = state.astype(BF16)
                        state = state * ch["decay"] + ch["u_t"]
                st_ref[2 * h + d] = state
        for ch in chains:
            sc = jnp.where(keeps[ch["d"]], ch["sc"], 0.0).astype(BF16)
            ch["o"] = (jnp.dot(sc, ch["v"], preferred_element_type=F32)
                       + lax.dot_general(ch["qd"], ch["state"], nt_dims, preferred_element_type=F32))
        for ch in chains:
            rows, cols = ch["rows"], ch["cols"]
            if finish:
                o = _rms(acc_ref[rows, cols] + ch["o"]) * nw_ref[:, cols]
                g = go_ref[rows, cols].astype(F32)
                o_ref[rows, cols] = (o * (g * _sigmoid(g))).astype(BF16)
            else:
                acc_ref[rows, cols] = ch["o"]

    def first_half(it, carry):
        trip(it, False)
        return carry

    def second_half(it, carry):
        trip(it, True)
        return carry

    lax.fori_loop(0, trips // 2, first_half, 0)
    lax.fori_loop(trips // 2, trips, second_half, 0)


def _hgrn(proj, lb, nw):
    width = HGRN_HB * HEAD_DIM

    def col(c0):
        return pl.BlockSpec((SEQ, width), lambda b, h: (b, c0 // HGRN_HB + h))

    return pl.pallas_call(
        _hgrn_kernel,
        grid=(BATCH, N_HGRN_HEADS // HGRN_HB),
        in_specs=[col(COL_QR), col(COL_FF), col(COL_FB), col(COL_IN), col(COL_GO),
                  pl.BlockSpec((2, 1, width), lambda b, h: (0, 0, h)),
                  pl.BlockSpec((1, width), lambda b, h: (0, h))],
        out_specs=pl.BlockSpec((SEQ, width), lambda b, h: (b, h)),
        out_shape=jax.ShapeDtypeStruct((N_TOK, HGRN_WIDTH), BF16),
        scratch_shapes=[pltpu.VMEM((SEQ, width), F32),
                        pltpu.VMEM((2 * HGRN_HB, HEAD_DIM, HEAD_DIM), F32)],
        compiler_params=_cparams(("arbitrary", "arbitrary")),
        name="hgrn",
    )(proj, proj, proj, proj, proj, lb, nw)


def _mix_kernel(oa_ref, or_ref, wa_ref, wr_ref, x_ref, g1_ref, sc_ref, sh_ref, lg_ref, lbias_ref,
                wrt_ref, brt_ref, x1_ref, h2_ref, idx_ref, gate_ref, rank_ref, cnt_ref, carry_ref):
    i = pl.program_id(0)

    @pl.when(i == 0)
    def _():
        carry_ref[...] = jnp.zeros_like(carry_ref)

    tm = MIX_SUB
    subs = [slice(u * tm, (u + 1) * tm) for u in range(MIX_TM // MIX_SUB)]
    lane = lax.broadcasted_iota(jnp.int32, (tm, LANES), 1)
    neg = jnp.float32(-jnp.inf)

    ys = [jnp.dot(oa_ref[rs, :], wa_ref[...], preferred_element_type=F32)
          + jnp.dot(or_ref[rs, :], wr_ref[...], preferred_element_type=F32) for rs in subs]
    h2s = []
    for rs, y in zip(subs, ys):
        x1 = _layer_norm(DEEPNORM_ALPHA * x_ref[rs, :] + g1_ref[...] * y) * lg_ref[...] + lbias_ref[...]
        x1_ref[rs, :] = x1
        h2 = _layer_norm(x1) * (1.0 + sc_ref[...]) + sh_ref[...]
        h2_ref[rs, :] = h2
        h2s.append(h2.astype(BF16))
    logit_list = [jnp.dot(h2, wrt_ref[...], preferred_element_type=F32) + brt_ref[...] for h2 in h2s]

    picks = []
    for work in logit_list:
        vals, sels = [], []
        for _ in range(TOP_K):
            m = jnp.max(work, axis=-1, keepdims=True)
            sel = jnp.min(jnp.where(work == m, lane, LANES), axis=-1, keepdims=True)
            vals.append(m)
            sels.append(sel)
            work = jnp.where(lane == sel, neg, work)
        es = [jnp.exp(v - vals[0]) for v in vals]
        multi = jnp.zeros((tm, LANES), F32)
        for sel in sels:
            multi = multi + jnp.where(lane == sel, 1.0, 0.0)
        picks.append((sels, es, es[0] + es[1] + es[2] + es[3], multi))

    r = lax.broadcasted_iota(jnp.int32, (tm, tm), 0)
    c = lax.broadcasted_iota(jnp.int32, (tm, tm), 1)
    strict = jnp.where(r > c, 1.0, 0.0).astype(BF16)
    within = [jnp.dot(strict, multi.astype(BF16), preferred_element_type=F32) for _, _, _, multi in picks]
    carry = carry_ref[...]
    for rs, (sels, es, denom, multi), inside in zip(subs, picks, within):
        before = inside + carry
        carry = carry + jnp.sum(multi, axis=0, keepdims=True)
        idx_out = jnp.zeros((tm, LANES), jnp.int32)
        gate_out = jnp.zeros((tm, LANES), F32)
        rank_out = jnp.zeros((tm, LANES), F32)
        for k in range(TOP_K):
            rk = jnp.sum(jnp.where(lane == sels[k], before, 0.0), axis=-1, keepdims=True)
            idx_out = jnp.where(lane == k, sels[k], idx_out)
            gate_out = jnp.where(lane == k, es[k] / denom, gate_out)
            rank_out = jnp.where(lane == k, rk, rank_out)
        idx_ref[rs, :] = idx_out
        gate_ref[rs, :] = gate_out
        rank_ref[rs, :] = rank_out.astype(jnp.int32)
    carry_ref[...] = carry
    cnt_ref[...] = carry


def _mix(o_attn, o_r, wa, wr, x2, g1, sc2, sh2, ln_g, ln_b, w_rt, b_rt):
    tiles_per_batch = SEQ // MIX_TM
    rows = lambda w: pl.BlockSpec((MIX_TM, w), lambda i: (i, 0))
    full = lambda a, b: pl.BlockSpec((a, b), lambda i: (0, 0))
    per_batch = pl.BlockSpec((None, 1, D_MODEL), lambda i: (i // tiles_per_batch, 0, 0))
    return pl.pallas_call(
        _mix_kernel,
        grid=(N_TOK // MIX_TM,),
        in_specs=[rows(ATTN_WIDTH), rows(HGRN_WIDTH), full(ATTN_WIDTH, D_MODEL), full(HGRN_WIDTH, D_MODEL),
                  rows(D_MODEL), per_batch, per_batch, per_batch, full(1, D_MODEL), full(1, D_MODEL),
                  full(D_MODEL, LANES), full(1, LANES)],
        out_specs=[rows(D_MODEL), rows(D_MODEL), rows(LANES), rows(LANES), rows(LANES), full(1, LANES)],
        out_shape=[jax.ShapeDtypeStruct((N_TOK, D_MODEL), F32),
                   jax.ShapeDtypeStruct((N_TOK, D_MODEL), F32),
                   jax.ShapeDtypeStruct((N_TOK, LANES), jnp.int32),
                   jax.ShapeDtypeStruct((N_TOK, LANES), F32),
                   jax.ShapeDtypeStruct((N_TOK, LANES), jnp.int32),
                   jax.ShapeDtypeStruct((1, LANES), F32)],
        scratch_shapes=[pltpu.VMEM((1, LANES), F32)],
        compiler_params=_cparams(("arbitrary",)),
        name="mix",
    )(o_attn, o_r, wa, wr, x2, g1, sc2, sh2, ln_g, ln_b, w_rt, b_rt)


def _wait_rows(n, make_copy):
    for bit in range(MOE_SB.bit_length()):
        @pl.when(((n >> bit) & 1) == 1)
        def _(bit=bit):
            make_copy(1 << bit).wait()


def _for_rows(lo, hi, fn):
    groups = (hi - lo) // MOE_ISSUE_UNROLL

    def group(q, carry):
        base = lo + q * MOE_ISSUE_UNROLL
        for u in range(MOE_ISSUE_UNROLL):
            fn(base + u)
        return carry

    def single(r, carry):
        fn(r)
        return carry

    lax.fori_loop(0, groups, group, 0)
    lax.fori_loop(lo + groups * MOE_ISSUE_UNROLL, hi, single, 0)


def _moe_kernel(se_ref, so_ref, sn_ref, na_ref, src_ref, dst_ref, h2_hbm, w1_hbm, b1_ref, w2_hbm, b2_ref,
                y_hbm, xg_ref, xb_ref, acc_ref, wf1g_ref, wf1l_ref, wf2_ref, wb1g_ref, wb1l_ref, wb2_ref,
                gsem, ssem, wsem):
    g = pl.program_id(0)
    nj = MOE_NJ
    n_assign = N_TOK * TOP_K
    n_slices = na_ref[0] * nj

    def weight_copies(t, s):
        e = se_ref[jnp.minimum(t // nj, MOE_G - 1)]
        col = pl.multiple_of((t % nj) * MOE_TH, MOE_TH)
        return (pltpu.make_async_copy(w1_hbm.at[e, :, pl.ds(col, MOE_TH)], wf1g_ref.at[s], wsem.at[s]),
                pltpu.make_async_copy(w1_hbm.at[e, :, pl.ds(D_EXPERT + col, MOE_TH)], wf1l_ref.at[s], wsem.at[s]),
                pltpu.make_async_copy(w2_hbm.at[e, pl.ds(col, MOE_TH), :], wf2_ref.at[s], wsem.at[s]))

    def cast_weights(s):
        wb1g_ref[s] = wf1g_ref[s].astype(BF16)
        wb1l_ref[s] = wf1l_ref[s].astype(BF16)
        wb2_ref[s] = wf2_ref[s].astype(BF16)

    def tiles_of(rows):
        return (rows + MOE_TMI - 1) // MOE_TMI

    n = sn_ref[g]
    nt = tiles_of(n)
    slot = g % 2
    g_next = jnp.minimum(g + 1, MOE_G - 1)
    n_next = jnp.where(g + 1 < MOE_G, sn_ref[g_next], 0)
    off_next = so_ref[g_next]
    g_prev = jnp.maximum(g - 1, 0)
    n_prev = jnp.where(g >= 1, sn_ref[g_prev], 0)
    off_prev = so_ref[g_prev]
    step_rows = nt * MOE_Q_TILE
    eager = (nj // 2) * step_rows

    def gather_copy(tok, r):
        return pltpu.make_async_copy(h2_hbm.at[pl.ds(tok, 1)], xg_ref.at[pl.ds(r, 1)], gsem)

    def scatter_copy(s, r, a):
        return pltpu.make_async_copy(acc_ref.at[s, pl.ds(r, 1)], y_hbm.at[pl.ds(a, 1)], ssem.at[s])

    def gather_start(r):
        gather_copy(src_ref[off_next + r], r).start()

    def scatter_start(r):
        scatter_copy(1 - slot, r, dst_ref[off_prev + r]).start()

    def eager_issue(first, count):
        for q in range(count):
            gather_start(first + q)
            scatter_start(first + q)

    def tile_rows(i):
        return pl.ds(pl.multiple_of(i * MOE_TMI, MOE_TMI), MOE_TMI)

    def begin():
        @pl.when(g == 0)
        def _():
            xg_ref[...] = jnp.zeros_like(xg_ref)
            acc_ref[...] = jnp.zeros_like(acc_ref)
            spare = pltpu.make_async_copy(acc_ref.at[0], y_hbm.at[pl.ds(n_assign, MOE_SB)], ssem.at[0])
            spare.start()
            spare.wait()
            _for_rows(0, n, lambda r: gather_copy(src_ref[so_ref[0] + r], r).start())
            for t in range(2):
                @pl.when(t < n_slices)
                def _(t=t):
                    for c in weight_copies(t, t):
                        c.start()

            @pl.when(n_slices > 0)
            def _():
                for c in weight_copies(0, 0):
                    c.wait()
                cast_weights(0)

        nt_prev = tiles_of(n_prev)
        eager_prev = (nj // 2) * nt_prev * MOE_Q_TILE
        n_prev2 = jnp.where(g >= 2, sn_ref[jnp.maximum(g - 2, 0)], 0)
        gathered = jnp.maximum(eager_prev, n)
        scattered = jnp.maximum(eager_prev, n_prev2)
        _wait_rows(gathered,
                   lambda k: pltpu.make_async_copy(h2_hbm.at[pl.ds(0, k)], xg_ref.at[pl.ds(0, k)], gsem))
        _wait_rows(scattered,
                   lambda k: pltpu.make_async_copy(acc_ref.at[slot, pl.ds(0, k)], y_hbm.at[pl.ds(0, k)],
                                                   ssem.at[slot]))

        def prep(i, carry):
            rows = tile_rows(i)
            xb_ref[rows, :] = xg_ref[rows, :].astype(BF16)
            acc_ref[slot, rows, :] = jnp.broadcast_to(b2_ref[...], (MOE_TMI, D_MODEL))
            return carry

        lax.fori_loop(0, nt, prep, 0)

    begin()

    def hidden_slice(j, ws):
        t = g * nj + j

        @pl.when(t + 1 < n_slices)
        def _():
            for c in weight_copies(t + 1, 1 - ws):
                c.wait()

        @pl.when(t + 2 < n_slices)
        def _():
            for c in weight_copies(t + 2, ws):
                c.start()

        b1g = b1_ref[pl.ds(j, 1), :]
        b1l = b1_ref[pl.ds(nj + j, 1), :]

        def tiles(first_tile, count, cast_next):
            if ws == 0:
                eager_issue(pl.multiple_of((j // 2) * step_rows + first_tile * MOE_Q_TILE, MOE_Q_TILE),
                            count * MOE_Q_TILE)
            rows = [tile_rows(first_tile + i) for i in range(count)]
            xs = [xb_ref[r, :] for r in rows]
            hid = [(jnp.dot(x, wb1g_ref[ws], preferred_element_type=F32) + b1g,
                    jnp.dot(x, wb1l_ref[ws], preferred_element_type=F32) + b1l) for x in xs]
            if cast_next:
                cast_weights(1 - ws)
            parts = []
            for hg, hl in hid:
                hg = jnp.minimum(hg, SWIGLU_LIMIT)
                hl = jnp.clip(hl, -SWIGLU_LIMIT, SWIGLU_LIMIT)
                act = hg * _sigmoid(SWIGLU_ALPHA * hg) * (hl + 1.0)
                parts.append(jnp.dot(act.astype(BF16), wb2_ref[ws], preferred_element_type=F32))
            for r, part in zip(rows, parts):
                acc_ref[slot, r, :] += part

        @pl.when(nt == 4)
        def _():
            tiles(0, 4, True)

        @pl.when((nt == 2) | (nt == 3))
        def _():
            tiles(0, 2, True)

        @pl.when(nt == 1)
        def _():
            tiles(0, 1, True)

        @pl.when(nt == 3)
        def _():
            tiles(2, 1, False)

    def slice_pair(jj, carry):
        for ws in range(2):
            hidden_slice(2 * jj + ws, ws)
        return carry

    @pl.when(nt > 0)
    def _():
        lax.fori_loop(0, nj // 2, slice_pair, 0)

    _for_rows(jnp.minimum(eager, n_next), n_next, gather_start)
    _for_rows(jnp.minimum(eager, n_prev), n_prev, scatter_start)

    @pl.when(g == MOE_G - 1)
    def _():
        _wait_rows(n_prev, lambda k: pltpu.make_async_copy(acc_ref.at[1 - slot, pl.ds(0, k)],
                                                            y_hbm.at[pl.ds(0, k)], ssem.at[1 - slot]))


def _moe(sb_expert, sb_off, sb_n, n_active, src_tok, dst_row, h2, w1, b1, w2, b2):
    grid_spec = pltpu.PrefetchScalarGridSpec(
        num_scalar_prefetch=6,
        grid=(MOE_G,),
        in_specs=[
            pl.BlockSpec(memory_space=pl.ANY),
            pl.BlockSpec(memory_space=pl.ANY),
            pl.BlockSpec((None, 2 * MOE_NJ, MOE_TH), lambda g, se, *_: (se[g], 0, 0)),
            pl.BlockSpec(memory_space=pl.ANY),
            pl.BlockSpec((None, 1, D_MODEL), lambda g, se, *_: (se[g], 0, 0)),
        ],
        out_specs=pl.BlockSpec(memory_space=pl.ANY),
        scratch_shapes=[pltpu.VMEM((MOE_SB, D_MODEL), F32),
                        pltpu.VMEM((MOE_SB, D_MODEL), BF16),
                        pltpu.VMEM((2, MOE_SB, D_MODEL), F32),
                        pltpu.VMEM((2, D_MODEL, MOE_TH), F32),
                        pltpu.VMEM((2, D_MODEL, MOE_TH), F32),
                        pltpu.VMEM((2, MOE_TH, D_MODEL), F32),
                        pltpu.VMEM((2, D_MODEL, MOE_TH), BF16),
                        pltpu.VMEM((2, D_MODEL, MOE_TH), BF16),
                        pltpu.VMEM((2, MOE_TH, D_MODEL), BF16),
                        pltpu.SemaphoreType.DMA(()),
                        pltpu.SemaphoreType.DMA((2,)),
                        pltpu.SemaphoreType.DMA((2,))],
    )
    return pl.pallas_call(
        _moe_kernel,
        grid_spec=grid_spec,
        out_shape=jax.ShapeDtypeStruct((N_TOK * TOP_K + MOE_SB, D_MODEL), F32),
        compiler_params=_cparams(("arbitrary",)),
        name="moe",
    )(sb_expert, sb_off, sb_n, n_active, src_tok, dst_row, h2, w1, b1, w2, b2)


def _final_kernel(x1_ref, y0_ref, y1_ref, y2_ref, y3_ref, gate_ref, g2_ref, lg_ref, lb_ref, o_ref):
    y = gate_ref[:, 0:1] * y0_ref[...]
    for k, y_ref in enumerate((y1_ref, y2_ref, y3_ref), start=1):
        y = y + gate_ref[:, k:k + 1] * y_ref[...]
    z = DEEPNORM_ALPHA * x1_ref[...] + g2_ref[...] * y
    o_ref[...] = _layer_norm(z) * lg_ref[...] + lb_ref[...]


def _final(x1, y4, gates, g2, ln_g, ln_b):
    tiles_per_batch = SEQ // FIN_TM
    tiles = N_TOK // FIN_TM
    rows = lambda w: pl.BlockSpec((FIN_TM, w), lambda i: (i, 0))
    plane = lambda k: pl.BlockSpec((FIN_TM, D_MODEL), lambda i: (k * tiles + i, 0))
    vec = pl.BlockSpec((1, D_MODEL), lambda i: (0, 0))
    return pl.pallas_call(
        _final_kernel,
        grid=(tiles,),
        in_specs=[rows(D_MODEL), plane(0), plane(1), plane(2), plane(3), rows(LANES),
                  pl.BlockSpec((None, 1, D_MODEL), lambda i: (i // tiles_per_batch, 0, 0)), vec, vec],
        out_specs=rows(D_MODEL),
        out_shape=jax.ShapeDtypeStruct((N_TOK, D_MODEL), F32),
        compiler_params=_cparams(("arbitrary",)),
        name="final",
    )(x1, y4, y4, y4, y4, gates, g2, ln_g, ln_b)


def _rope_tables():
    rows = SEQ // GRID_W
    t = np.arange(SEQ)
    row = (t // GRID_W - rows // 2).astype(np.float32)
    col = (t % GRID_W - GRID_W // 2).astype(np.float32)
    inv_freq = jnp.asarray(ROPE_THETA, F32) ** (-jnp.arange(0, ROPE_AXIS_DIM, 2, dtype=F32) / ROPE_AXIS_DIM)
    ang_row = jnp.asarray(row)[:, None] * inv_freq[None, :]
    ang_col = jnp.asarray(col)[:, None] * inv_freq[None, :]
    zeros = jnp.zeros_like(ang_row)
    cos = jnp.concatenate([jnp.cos(ang_row)] * 2 + [jnp.cos(ang_col)] * 2, axis=-1)
    sin_lo = jnp.concatenate([-jnp.sin(ang_row), zeros, -jnp.sin(ang_col), zeros], axis=-1)
    sin_hi = jnp.concatenate([zeros, jnp.sin(ang_row), zeros, jnp.sin(ang_col)], axis=-1)
    return cos, sin_lo, sin_hi


def _routing(top_i, rank, counts):
    counts = counts.astype(jnp.int32)
    nsb = (counts + MOE_SB - 1) // MOE_SB
    sb_end = jnp.cumsum(nsb)
    sb_start = sb_end - nsb
    g = jnp.arange(MOE_G, dtype=jnp.int32)
    active = g < sb_end[-1]
    e_of_g = jnp.minimum(jnp.sum(g[:, None] >= sb_end[None, :], axis=1), N_EXPERTS - 1).astype(jnp.int32)
    first_row = (g - sb_start[e_of_g]) * MOE_SB
    n_of_g = jnp.where(active, jnp.clip(counts[e_of_g] - first_row, 0, MOE_SB), 0).astype(jnp.int32)
    order = jnp.argsort(-n_of_g, stable=True).astype(jnp.int32)
    place = jnp.zeros((MOE_G,), jnp.int32).at[order].set(g)
    sb_n = n_of_g[order]
    sb_off = (jnp.cumsum(sb_n) - sb_n).astype(jnp.int32)
    last_e = e_of_g[order[jnp.maximum(sb_end[-1] - 1, 0)]]
    sb_expert = jnp.where(sb_n > 0, e_of_g[order], last_e).astype(jnp.int32)
    assign = jnp.arange(N_TOK * TOP_K, dtype=jnp.int32)
    max_chunks = N_TOK // MOE_SB
    chunk_ids = jnp.arange(max_chunks, dtype=jnp.int32)
    base = sb_off[place[jnp.minimum(sb_start[:, None] + chunk_ids[None, :], MOE_G - 1)]]
    is_e = top_i[:, :, None] == jnp.arange(N_EXPERTS, dtype=jnp.int32)
    is_c = (rank // MOE_SB)[:, :, None] == chunk_ids
    base_e = jnp.sum(jnp.where(is_e[:, :, :, None], base[None, None], 0), axis=2)
    dest = (jnp.sum(jnp.where(is_c, base_e, 0), axis=-1) + rank % MOE_SB).reshape(-1)
    sorted_assign = lax.sort_key_val(dest, assign)[1]
    src_tok = sorted_assign // TOP_K
    dst_row = (sorted_assign % TOP_K) * N_TOK + src_tok
    src_tok = jnp.concatenate([src_tok, jnp.zeros((MOE_SB,), jnp.int32)])
    dst_row = jnp.concatenate([dst_row, N_TOK * TOP_K + jnp.arange(MOE_SB, dtype=jnp.int32)])
    n_active = sb_end[-1:].astype(jnp.int32)
    return sb_expert, sb_off, sb_n, n_active, src_tok, dst_row


def kernel(x, c, w_ada, b_ada, w_in, q_norm_w, k_norm_w, attn_norm_w, hgrn_lb, hgrn_norm_w, w_out, ln1_g, ln1_b, w_router, b_router, w_exp_in, b_exp_in, w_exp_out, b_exp_out, ln2_g, ln2_b):
    c_pad = jnp.zeros((8, D_MODEL), F32).at[:BATCH].set(c)
    cos, sin_lo, sin_hi = _rope_tables()
    x2 = x.reshape(N_TOK, D_MODEL)
    for l in range(DEPTH):
        mod = _ada(c_pad, w_ada[l], b_ada[l][None, :])[:BATCH]
        sh1, sc1, g1, sh2, sc2, g2 = [m.reshape(BATCH, 1, D_MODEL) for m in jnp.split(mod, 6, axis=-1)]

        proj = _proj(x2, sc1, sh1, w_in[l].astype(BF16))
        o_attn = _attention(proj, cos, sin_lo, sin_hi, q_norm_w[l][None, :], k_norm_w[l][None, :],
                            attn_norm_w[l][None, :])
        lb = jnp.cumsum(jax.nn.softmax(hgrn_lb.astype(F32), axis=1), axis=1)[:, l]
        o_r = _hgrn(proj, lb.reshape(2, 1, HGRN_WIDTH), hgrn_norm_w[l][None, :])

        w_o = w_out[l].astype(BF16)
        w_rt = jnp.zeros((D_MODEL, LANES), BF16).at[:, :N_EXPERTS].set(w_router[l].astype(BF16))
        b_rt = jnp.full((1, LANES), -1e30, F32).at[0, :N_EXPERTS].set(b_router[l])
        x1, h2, idx, gates, rank, counts = _mix(
            o_attn, o_r, w_o[:ATTN_WIDTH], w_o[ATTN_WIDTH:], x2, g1, sc2, sh2,
            ln1_g[l][None, :], ln1_b[l][None, :], w_rt, b_rt)
        sb_expert, sb_off, sb_n, n_active, src_tok, dst_row = _routing(idx[:, :TOP_K], rank[:, :TOP_K],
                                                                       counts[0, :N_EXPERTS])
        y4 = _moe(sb_expert, sb_off, sb_n, n_active, src_tok, dst_row, h2, w_exp_in[l],
                  b_exp_in[l].reshape(N_EXPERTS, 2 * MOE_NJ, MOE_TH), w_exp_out[l], b_exp_out[l][:, None, :])
        x2 = _final(x1, y4, gates, g2, ln2_g[l][None, :], ln2_b[l][None, :])
    return x2.reshape(BATCH, SEQ, D_MODEL)
```

```python
import math

import numpy as np
import jax
import jax.numpy as jnp
from jax import lax
from jax.experimental import pallas as pl
from jax.experimental.pallas import tpu as pltpu

F32 = jnp.float32
BF16 = jnp.bfloat16

D_MODEL = 2048
BATCH = 4
SEQ = 2048
DEPTH = 1
N_TOK = BATCH * SEQ
HEAD_DIM = 128
ATTN_WIDTH = 1024
N_Q_HEADS = 8
N_KV_HEADS = 2
KV_GROUP = 4
HGRN_WIDTH = 1024
N_HGRN_HEADS = 8
HGRN_CHUNK = 64
GRID_W = 64
ROPE_THETA = 10000.0
ROPE_AXIS_DIM = 64
N_EXPERTS = 32
TOP_K = 4
D_EXPERT = 2048
SWIGLU_LIMIT = 7.0
SWIGLU_ALPHA = 1.702
NORM_EPS = 1e-6
DEEPNORM_ALPHA = (2 * DEPTH) ** 0.25
PROJ_WIDTH = 6656
LANES = 128

COL_Q = 0
COL_K = 8
COL_V = 10
COL_QR = 12
COL_FF = 20
COL_FB = 28
COL_IN = 36
COL_GO = 44

V7X_VMEM_BYTES = 64 * 1024 * 1024
VMEM_LIMIT = V7X_VMEM_BYTES * 7 // 8

ADA_TN = 2048
PROJ_TM = 512
PROJ_TN = 3328
ATTN_TQ = 256
HGRN_HB = 4
HGRN_UN = 2
MIX_TM = 512
MIX_SUB = 128
MOE_ISSUE_UNROLL = 8
MOE_SB = 1024
MOE_TMI = 256
MOE_TH = 256
MOE_NJ = D_EXPERT // MOE_TH
MOE_G = N_TOK * TOP_K // MOE_SB + N_EXPERTS + 1
MOE_Q_TILE = MOE_SB // (MOE_NJ // 2) // (MOE_SB // MOE_TMI)
FIN_TM = 512


def _cparams(sem):
    return pltpu.CompilerParams(dimension_semantics=sem, vmem_limit_bytes=VMEM_LIMIT)


def _sigmoid(x):
    return 1.0 / (1.0 + jnp.exp(-x))


def _layer_norm(x):
    mu = jnp.mean(x, axis=-1, keepdims=True)
    xc = x - mu
    var = jnp.mean(xc * xc, axis=-1, keepdims=True)
    return xc * lax.rsqrt(var + NORM_EPS)


def _rms(x):
    return x * lax.rsqrt(jnp.mean(x * x, axis=-1, keepdims=True) + NORM_EPS)


def _ada_kernel(c_ref, w_ref, b_ref, o_ref):
    c = c_ref[...]
    ca = c * _sigmoid(c)
    o_ref[...] = jnp.dot(ca.astype(BF16), w_ref[...].astype(BF16),
                         preferred_element_type=F32) + b_ref[...]


def _ada(c_pad, w, b):
    n = w.shape[1]
    return pl.pallas_call(
        _ada_kernel,
        grid=(n // ADA_TN,),
        in_specs=[pl.BlockSpec((8, D_MODEL), lambda j: (0, 0)),
                  pl.BlockSpec((D_MODEL, ADA_TN), lambda j: (0, j)),
                  pl.BlockSpec((1, ADA_TN), lambda j: (0, j))],
        out_specs=pl.BlockSpec((8, ADA_TN), lambda j: (0, j)),
        out_shape=jax.ShapeDtypeStruct((8, n), F32),
        compiler_params=_cparams(("arbitrary",)),
        name="ada",
    )(c_pad, w, b)


def _proj_kernel(x_ref, sc_ref, sh_ref, w_ref, o_ref, h_ref):
    @pl.when(pl.program_id(1) == 0)
    def _():
        h = _layer_norm(x_ref[...]) * (1.0 + sc_ref[...]) + sh_ref[...]
        h_ref[...] = h.astype(BF16)

    o_ref[...] = jnp.dot(h_ref[...], w_ref[...], preferred_element_type=F32).astype(BF16)


def _proj(x2, sc, sh, w_bf):
    tiles_per_batch = SEQ // PROJ_TM
    return pl.pallas_call(
        _proj_kernel,
        grid=(N_TOK // PROJ_TM, PROJ_WIDTH // PROJ_TN),
        in_specs=[pl.BlockSpec((PROJ_TM, D_MODEL), lambda i, j: (i, 0)),
                  pl.BlockSpec((None, 1, D_MODEL), lambda i, j: (i // tiles_per_batch, 0, 0)),
                  pl.BlockSpec((None, 1, D_MODEL), lambda i, j: (i // tiles_per_batch, 0, 0)),
                  pl.BlockSpec((D_MODEL, PROJ_TN), lambda i, j: (0, j))],
        out_specs=pl.BlockSpec((PROJ_TM, PROJ_TN), lambda i, j: (i, j)),
        out_shape=jax.ShapeDtypeStruct((N_TOK, PROJ_WIDTH), BF16),
        scratch_shapes=[pltpu.VMEM((PROJ_TM, D_MODEL), BF16)],
        compiler_params=_cparams(("arbitrary", "arbitrary")),
        name="proj",
    )(x2, sc, sh, w_bf)


def _rope(x, cos, sin_lo, sin_hi):
    return (x * cos + pltpu.roll(x, 96, axis=1) * sin_lo + pltpu.roll(x, 32, axis=1) * sin_hi)


def _attn_kernel(q_ref, k_ref, v_ref, cq_ref, slq_ref, shq_ref, ck_ref, slk_ref, shk_ref,
                 qw_ref, kw_ref, aw_ref, o_ref, kr_ref, v1_ref):
    @pl.when(pl.program_id(2) == 0)
    def _():
        k = _rms(k_ref[...].astype(F32)) * kw_ref[...]
        kr_ref[...] = _rope(k, ck_ref[...], slk_ref[...], shk_ref[...]).astype(BF16)
        v1_ref[:, :HEAD_DIM] = v_ref[...]
        v1_ref[:, HEAD_DIM:] = jnp.ones((SEQ, HEAD_DIM), BF16)

    scale = math.log2(math.e) / math.sqrt(HEAD_DIM)
    cq = cq_ref[...]
    slq = slq_ref[...]
    shq = shq_ref[...]
    heads = [slice(h * HEAD_DIM, (h + 1) * HEAD_DIM) for h in range(KV_GROUP)]
    qs = []
    for cols in heads:
        q = _rms(q_ref[:, cols].astype(F32)) * qw_ref[...]
        qs.append((_rope(q, cq, slq, shq) * scale).astype(BF16))
    scores = [lax.dot_general(q, kr_ref[...], (((1,), (1,)), ((), ())), preferred_element_type=F32)
              for q in qs]
    outs = []
    for s in scores:
        p = jnp.exp2((s - jnp.max(s, axis=-1, keepdims=True)).astype(BF16))
        ov = jnp.dot(p, v1_ref[...], preferred_element_type=F32)
        outs.append(ov[:, :HEAD_DIM] / ov[:, HEAD_DIM:HEAD_DIM + 1])
    for cols, o in zip(heads, outs):
        o_ref[:, cols] = (_rms(o) * aw_ref[:, cols]).astype(BF16)


def _attention(proj, cos, sin_lo, sin_hi, qw, kw, aw):
    nq = SEQ // ATTN_TQ
    gw = KV_GROUP * HEAD_DIM
    tab_q = pl.BlockSpec((ATTN_TQ, HEAD_DIM), lambda b, g, i: (i, 0))
    tab_k = pl.BlockSpec((SEQ, HEAD_DIM), lambda b, g, i: (0, 0))
    return pl.pallas_call(
        _attn_kernel,
        grid=(BATCH, N_KV_HEADS, nq),
        in_specs=[pl.BlockSpec((ATTN_TQ, gw), lambda b, g, i: (b * nq + i, g)),
                  pl.BlockSpec((SEQ, HEAD_DIM), lambda b, g, i: (b, COL_K + g)),
                  pl.BlockSpec((SEQ, HEAD_DIM), lambda b, g, i: (b, COL_V + g)),
                  tab_q, tab_q, tab_q, tab_k, tab_k, tab_k,
                  pl.BlockSpec((1, HEAD_DIM), lambda b, g, i: (0, 0)),
                  pl.BlockSpec((1, HEAD_DIM), lambda b, g, i: (0, 0)),
                  pl.BlockSpec((1, gw), lambda b, g, i: (0, g))],
        out_specs=pl.BlockSpec((ATTN_TQ, gw), lambda b, g, i: (b * nq + i, g)),
        out_shape=jax.ShapeDtypeStruct((N_TOK, ATTN_WIDTH), BF16),
        scratch_shapes=[pltpu.VMEM((SEQ, HEAD_DIM), BF16), pltpu.VMEM((SEQ, 2 * HEAD_DIM), BF16)],
        compiler_params=_cparams(("arbitrary", "arbitrary", "arbitrary")),
        name="attn",
    )(proj, proj, proj, cos, sin_lo, sin_hi, cos, sin_lo, sin_hi, qw, kw, aw)


def _hgrn_kernel(qr_ref, ff_ref, fb_ref, iv_ref, go_ref, lb_ref, nw_ref, o_ref, acc_ref, st_ref):
    C = HGRN_CHUNK
    nc = SEQ // C
    trips = nc // HGRN_UN
    row = lax.broadcasted_iota(jnp.int32, (C, C), 0)
    col = lax.broadcasted_iota(jnp.int32, (C, C), 1)
    keeps = (row >= col, row <= col)
    lasts = (C - 1, 0)
    f_refs = (ff_ref, fb_ref)

    nt_dims = (((1,), (1,)), ((), ()))
    tn_dims = (((0,), (0,)), ((), ()))
    st_ref[...] = jnp.zeros_like(st_ref)

    def trip(it, finish):
        chains = []
        for h in range(HGRN_HB):
            cols = slice(h * HEAD_DIM, (h + 1) * HEAD_DIM)
            for d in range(2):
                for u in range(HGRN_UN):
                    n = it * HGRN_UN + u
                    cidx = n if d == 0 else nc - 1 - n
                    chains.append(dict(h=h, d=d, cols=cols, rows=pl.ds(pl.multiple_of(cidx * C, C), C)))

        for ch in chains:
            d = ch["d"]
            lb = lb_ref[d, :, ch["cols"]]
            fg = lb + (1.0 - lb) * _sigmoid(f_refs[d][ch["rows"], ch["cols"]].astype(F32))
            ch["kk"] = 1.0 - fg
            lf = jnp.log(fg)
            lf_hi = lf.astype(BF16)
            lf_lo = (lf - lf_hi.astype(F32)).astype(BF16)
            tri = jnp.where(keeps[d], 1.0, 0.0).astype(BF16)
            ch["b"] = (jnp.dot(tri, lf_hi, preferred_element_type=F32)
                       + jnp.dot(tri, lf_lo, preferred_element_type=F32))
        for ch in chains:
            b = ch["b"]
            bl = b[lasts[ch["d"]]:lasts[ch["d"]] + 1, :]
            qx = qr_ref[ch["rows"], ch["cols"]].astype(F32)
            ch["qd"] = (qx * _sigmoid(qx) * jnp.exp(b)).astype(BF16)
            kd = ch["kk"] * jnp.exp(-b)
            ch["decay"] = jnp.exp(bl)
            ku = (kd * ch["decay"]).astype(BF16)
            kd = kd.astype(BF16)
            ch["v"] = iv_ref[ch["rows"], ch["cols"]]
            ch["sc"] = lax.dot_general(ch["qd"], kd, nt_dims, preferred_element_type=F32)
            ch["u_t"] = lax.dot_general(ch["v"], ku, tn_dims, preferred_element_type=F32)
        for h in range(HGRN_HB):
            for d in range(2):
                state = st_ref[2 * h + d]
                for ch in chains:
                    if ch["h"] == h and ch["d"] == d:
                        ch["state"] = state.astype(BF16)
                        state = state * ch["decay"] + ch["u_t"]
                st_ref[2 * h + d] = state
        for ch in chains:
            sc = jnp.where(keeps[ch["d"]], ch["sc"], 0.0).astype(BF16)
            ch["o"] = (jnp.dot(sc, ch["v"], preferred_element_type=F32)
                       + lax.dot_general(ch["qd"], ch["state"], nt_dims, preferred_element_type=F32))
        for ch in chains:
            rows, cols = ch["rows"], ch["cols"]
            if finish:
                o = _rms(acc_ref[rows, cols] + ch["o"]) * nw_ref[:, cols]
                g = go_ref[rows, cols].astype(F32)
                o_ref[rows, cols] = (o * (g * _sigmoid(g))).astype(BF16)
            else:
                acc_ref[rows, cols] = ch["o"]

    def first_half(it, carry):
        trip(it, False)
        return carry

    def second_half(it, carry):
        trip(it, True)
        return carry

    lax.fori_loop(0, trips // 2, first_half, 0)
    lax.fori_loop(trips // 2, trips, second_half, 0)


def _hgrn(proj, lb, nw):
    width = HGRN_HB * HEAD_DIM

    def col(c0):
        return pl.BlockSpec((SEQ, width), lambda b, h: (b, c0 // HGRN_HB + h))

    return pl.pallas_call(
        _hgrn_kernel,
        grid=(BATCH, N_HGRN_HEADS // HGRN_HB),
        in_specs=[col(COL_QR), col(COL_FF), col(COL_FB), col(COL_IN), col(COL_GO),
                  pl.BlockSpec((2, 1, width), lambda b, h: (0, 0, h)),
                  pl.BlockSpec((1, width), lambda b, h: (0, h))],
        out_specs=pl.BlockSpec((SEQ, width), lambda b, h: (b, h)),
        out_shape=jax.ShapeDtypeStruct((N_TOK, HGRN_WIDTH), BF16),
        scratch_shapes=[pltpu.VMEM((SEQ, width), F32),
                        pltpu.VMEM((2 * HGRN_HB, HEAD_DIM, HEAD_DIM), F32)],
        compiler_params=_cparams(("arbitrary", "arbitrary")),
        name="hgrn",
    )(proj, proj, proj, proj, proj, lb, nw)


def _mix_kernel(oa_ref, or_ref, wa_ref, wr_ref, x_ref, g1_ref, sc_ref, sh_ref, lg_ref, lbias_ref,
                wrt_ref, brt_ref, x1_ref, h2_ref, idx_ref, gate_ref, rank_ref, cnt_ref, carry_ref):
    i = pl.program_id(0)

    @pl.when(i == 0)
    def _():
        carry_ref[...] = jnp.zeros_like(carry_ref)

    tm = MIX_SUB
    subs = [slice(u * tm, (u + 1) * tm) for u in range(MIX_TM // MIX_SUB)]
    lane = lax.broadcasted_iota(jnp.int32, (tm, LANES), 1)
    neg = jnp.float32(-jnp.inf)

    ys = [jnp.dot(oa_ref[rs, :], wa_ref[...], preferred_element_type=F32)
          + jnp.dot(or_ref[rs, :], wr_ref[...], preferred_element_type=F32) for rs in subs]
    h2s = []
    for rs, y in zip(subs, ys):
        x1 = _layer_norm(DEEPNORM_ALPHA * x_ref[rs, :] + g1_ref[...] * y) * lg_ref[...] + lbias_ref[...]
        x1_ref[rs, :] = x1
        h2 = _layer_norm(x1) * (1.0 + sc_ref[...]) + sh_ref[...]
        h2_ref[rs, :] = h2
        h2s.append(h2.astype(BF16))
    logit_list = [jnp.dot(h2, wrt_ref[...], preferred_element_type=F32) + brt_ref[...] for h2 in h2s]

    picks = []
    for work in logit_list:
        vals, sels = [], []
        for _ in range(TOP_K):
            m = jnp.max(work, axis=-1, keepdims=True)
            sel = jnp.min(jnp.where(work == m, lane, LANES), axis=-1, keepdims=True)
            vals.append(m)
            sels.append(sel)
            work = jnp.where(lane == sel, neg, work)
        es = [jnp.exp(v - vals[0]) for v in vals]
        multi = jnp.zeros((tm, LANES), F32)
        for sel in sels:
            multi = multi + jnp.where(lane == sel, 1.0, 0.0)
        picks.append((sels, es, es[0] + es[1] + es[2] + es[3], multi))

    r = lax.broadcasted_iota(jnp.int32, (tm, tm), 0)
    c = lax.broadcasted_iota(jnp.int32, (tm, tm), 1)
    strict = jnp.where(r > c, 1.0, 0.0).astype(BF16)
    within = [jnp.dot(strict, multi.astype(BF16), preferred_element_type=F32) for _, _, _, multi in picks]
    carry = carry_ref[...]
    for rs, (sels, es, denom, multi), inside in zip(subs, picks, within):
        before = inside + carry
        carry = carry + jnp.sum(multi, axis=0, keepdims=True)
        idx_out = jnp.zeros((tm, LANES), jnp.int32)
        gate_out = jnp.zeros((tm, LANES), F32)
        rank_out = jnp.zeros((tm, LANES), F32)
        for k in range(TOP_K):
            rk = jnp.sum(jnp.where(lane == sels[k], before, 0.0), axis=-1, keepdims=True)
            idx_out = jnp.where(lane == k, sels[k], idx_out)
            gate_out = jnp.where(lane == k, es[k] / denom, gate_out)
            rank_out = jnp.where(lane == k, rk, rank_out)
        idx_ref[rs, :] = idx_out
        gate_ref[rs, :] = gate_out
        rank_ref[rs, :] = rank_out.astype(jnp.int32)
    carry_ref[...] = carry
    cnt_ref[...] = carry


def _mix(o_attn, o_r, wa, wr, x2, g1, sc2, sh2, ln_g, ln_b, w_rt, b_rt):
    tiles_per_batch = SEQ // MIX_TM
    rows = lambda w: pl.BlockSpec((MIX_TM, w), lambda i: (i, 0))
    full = lambda a, b: pl.BlockSpec((a, b), lambda i: (0, 0))
    per_batch = pl.BlockSpec((None, 1, D_MODEL), lambda i: (i // tiles_per_batch, 0, 0))
    return pl.pallas_call(
        _mix_kernel,
        grid=(N_TOK // MIX_TM,),
        in_specs=[rows(ATTN_WIDTH), rows(HGRN_WIDTH), full(ATTN_WIDTH, D_MODEL), full(HGRN_WIDTH, D_MODEL),
                  rows(D_MODEL), per_batch, per_batch, per_batch, full(1, D_MODEL), full(1, D_MODEL),
                  full(D_MODEL, LANES), full(1, LANES)],
        out_specs=[rows(D_MODEL), rows(D_MODEL), rows(LANES), rows(LANES), rows(LANES), full(1, LANES)],
        out_shape=[jax.ShapeDtypeStruct((N_TOK, D_MODEL), F32),
                   jax.ShapeDtypeStruct((N_TOK, D_MODEL), F32),
                   jax.ShapeDtypeStruct((N_TOK, LANES), jnp.int32),
                   jax.ShapeDtypeStruct((N_TOK, LANES), F32),
                   jax.ShapeDtypeStruct((N_TOK, LANES), jnp.int32),
                   jax.ShapeDtypeStruct((1, LANES), F32)],
        scratch_shapes=[pltpu.VMEM((1, LANES), F32)],
        compiler_params=_cparams(("arbitrary",)),
        name="mix",
    )(o_attn, o_r, wa, wr, x2, g1, sc2, sh2, ln_g, ln_b, w_rt, b_rt)


def _wait_rows(n, make_copy):
    for bit in range(MOE_SB.bit_length()):
        @pl.when(((n >> bit) & 1) == 1)
        def _(bit=bit):
            make_copy(1 << bit).wait()


def _for_rows(lo, hi, fn):
    groups = (hi - lo) // MOE_ISSUE_UNROLL

    def group(q, carry):
        base = lo + q * MOE_ISSUE_UNROLL
        for u in range(MOE_ISSUE_UNROLL):
            fn(base + u)
        return carry

    def single(r, carry):
        fn(r)
        return carry

    lax.fori_loop(0, groups, group, 0)
    lax.fori_loop(lo + groups * MOE_ISSUE_UNROLL, hi, single, 0)


def _moe_kernel(se_ref, so_ref, sn_ref, na_ref, src_ref, dst_ref, h2_hbm, w1_hbm, b1_ref, w2_hbm, b2_ref,
                y_hbm, xg_ref, xb_ref, acc_ref, wf1g_ref, wf1l_ref, wf2_ref, wb1g_ref, wb1l_ref, wb2_ref,
                gsem, ssem, wsem):
    g = pl.program_id(0)
    nj = MOE_NJ
    n_assign = N_TOK * TOP_K
    n_slices = na_ref[0] * nj

    def weight_copies(t, s):
        e = se_ref[jnp.minimum(t // nj, MOE_G - 1)]
        col = pl.multiple_of((t % nj) * MOE_TH, MOE_TH)
        return (pltpu.make_async_copy(w1_hbm.at[e, :, pl.ds(col, MOE_TH)], wf1g_ref.at[s], wsem.at[s]),
                pltpu.make_async_copy(w1_hbm.at[e, :, pl.ds(D_EXPERT + col, MOE_TH)], wf1l_ref.at[s], wsem.at[s]),
                pltpu.make_async_copy(w2_hbm.at[e, pl.ds(col, MOE_TH), :], wf2_ref.at[s], wsem.at[s]))

    def cast_weights(s):
        wb1g_ref[s] = wf1g_ref[s].astype(BF16)
        wb1l_ref[s] = wf1l_ref[s].astype(BF16)
        wb2_ref[s] = wf2_ref[s].astype(BF16)

    def tiles_of(rows):
        return (rows + MOE_TMI - 1) // MOE_TMI

    n = sn_ref[g]
    nt = tiles_of(n)
    slot = g % 2
    g_next = jnp.minimum(g + 1, MOE_G - 1)
    n_next = jnp.where(g + 1 < MOE_G, sn_ref[g_next], 0)
    off_next = so_ref[g_next]
    g_prev = jnp.maximum(g - 1, 0)
    n_prev = jnp.where(g >= 1, sn_ref[g_prev], 0)
    off_prev = so_ref[g_prev]
    step_rows = nt * MOE_Q_TILE
    eager = (nj // 2) * step_rows

    def gather_copy(tok, r):
        return pltpu.make_async_copy(h2_hbm.at[pl.ds(tok, 1)], xg_ref.at[pl.ds(r, 1)], gsem)

    def scatter_copy(s, r, a):
        return pltpu.make_async_copy(acc_ref.at[s, pl.ds(r, 1)], y_hbm.at[pl.ds(a, 1)], ssem.at[s])

    def gather_start(r):
        gather_copy(src_ref[off_next + r], r).start()

    def scatter_start(r):
        scatter_copy(1 - slot, r, dst_ref[off_prev + r]).start()

    def eager_issue(first, count):
        for q in range(count):
            gather_start(first + q)
            scatter_start(first + q)

    def tile_rows(i):
        return pl.ds(pl.multiple_of(i * MOE_TMI, MOE_TMI), MOE_TMI)

    def begin():
        @pl.when(g == 0)
        def _():
            xg_ref[...] = jnp.zeros_like(xg_ref)
            acc_ref[...] = jnp.zeros_like(acc_ref)
            spare = pltpu.make_async_copy(acc_ref.at[0], y_hbm.at[pl.ds(n_assign, MOE_SB)], ssem.at[0])
            spare.start()
            spare.wait()
            _for_rows(0, n, lambda r: gather_copy(src_ref[so_ref[0] + r], r).start())
            for t in range(2):
                @pl.when(t < n_slices)
                def _(t=t):
                    for c in weight_copies(t, t):
                        c.start()

            @pl.when(n_slices > 0)
            def _():
                for c in weight_copies(0, 0):
                    c.wait()
                cast_weights(0)

        nt_prev = tiles_of(n_prev)
        eager_prev = (nj // 2) * nt_prev * MOE_Q_TILE
        n_prev2 = jnp.where(g >= 2, sn_ref[jnp.maximum(g - 2, 0)], 0)
        gathered = jnp.maximum(eager_prev, n)
        scattered = jnp.maximum(eager_prev, n_prev2)
        _wait_rows(gathered,
                   lambda k: pltpu.make_async_copy(h2_hbm.at[pl.ds(0, k)], xg_ref.at[pl.ds(0, k)], gsem))
        _wait_rows(scattered,
                   lambda k: pltpu.make_async_copy(acc_ref.at[slot, pl.ds(0, k)], y_hbm.at[pl.ds(0, k)],
                                                   ssem.at[slot]))

        def prep(i, carry):
            rows = tile_rows(i)
            xb_ref[rows, :] = xg_ref[rows, :].astype(BF16)
            acc_ref[slot, rows, :] = jnp.broadcast_to(b2_ref[...], (MOE_TMI, D_MODEL))
            return carry

        lax.fori_loop(0, nt, prep, 0)

    begin()

    def hidden_slice(j, ws):
        t = g * nj + j

        @pl.when(t + 1 < n_slices)
        def _():
            for c in weight_copies(t + 1, 1 - ws):
                c.wait()

        @pl.when(t + 2 < n_slices)
        def _():
            for c in weight_copies(t + 2, ws):
                c.start()

        b1g = b1_ref[pl.ds(j, 1), :]
        b1l = b1_ref[pl.ds(nj + j, 1), :]

        def tiles(first_tile, count, cast_next):
            if ws == 0:
                eager_issue(pl.multiple_of((j // 2) * step_rows + first_tile * MOE_Q_TILE, MOE_Q_TILE),
                            count * MOE_Q_TILE)
            rows = [tile_rows(first_tile + i) for i in range(count)]
            xs = [xb_ref[r, :] for r in rows]
            hid = [(jnp.dot(x, wb1g_ref[ws], preferred_element_type=F32) + b1g,
                    jnp.dot(x, wb1l_ref[ws], preferred_element_type=F32) + b1l) for x in xs]
            if cast_next:
                cast_weights(1 - ws)
            parts = []
            for hg, hl in hid:
                hg = jnp.minimum(hg, SWIGLU_LIMIT)
                hl = jnp.clip(hl, -SWIGLU_LIMIT, SWIGLU_LIMIT)
                act = hg * _sigmoid(SWIGLU_ALPHA * hg) * (hl + 1.0)
                parts.append(jnp.dot(act.astype(BF16), wb2_ref[ws], preferred_element_type=F32))
            for r, part in zip(rows, parts):
                acc_ref[slot, r, :] += part

        @pl.when(nt == 4)
        def _():
            tiles(0, 4, True)

        @pl.when((nt == 2) | (nt == 3))
        def _():
            tiles(0, 2, True)

        @pl.when(nt == 1)
        def _():
            tiles(0, 1, True)

        @pl.when(nt == 3)
        def _():
            tiles(2, 1, False)

    def slice_pair(jj, carry):
        for ws in range(2):
            hidden_slice(2 * jj + ws, ws)
        return carry

    @pl.when(nt > 0)
    def _():
        lax.fori_loop(0, nj // 2, slice_pair, 0)

    _for_rows(jnp.minimum(eager, n_next), n_next, gather_start)
    _for_rows(jnp.minimum(eager, n_prev), n_prev, scatter_start)

    @pl.when(g == MOE_G - 1)
    def _():
        _wait_rows(n_prev, lambda k: pltpu.make_async_copy(acc_ref.at[1 - slot, pl.ds(0, k)],
                                                            y_hbm.at[pl.ds(0, k)], ssem.at[1 - slot]))


def _moe(sb_expert, sb_off, sb_n, n_active, src_tok, dst_row, h2, w1, b1, w2, b2):
    grid_spec = pltpu.PrefetchScalarGridSpec(
        num_scalar_prefetch=6,
        grid=(MOE_G,),
        in_specs=[
            pl.BlockSpec(memory_space=pl.ANY),
            pl.BlockSpec(memory_space=pl.ANY),
            pl.BlockSpec((None, 2 * MOE_NJ, MOE_TH), lambda g, se, *_: (se[g], 0, 0)),
            pl.BlockSpec(memory_space=pl.ANY),
            pl.BlockSpec((None, 1, D_MODEL), lambda g, se, *_: (se[g], 0, 0)),
        ],
        out_specs=pl.BlockSpec(memory_space=pl.ANY),
        scratch_shapes=[pltpu.VMEM((MOE_SB, D_MODEL), F32),
                        pltpu.VMEM((MOE_SB, D_MODEL), BF16),
                        pltpu.VMEM((2, MOE_SB, D_MODEL), F32),
                        pltpu.VMEM((2, D_MODEL, MOE_TH), F32),
                        pltpu.VMEM((2, D_MODEL, MOE_TH), F32),
                        pltpu.VMEM((2, MOE_TH, D_MODEL), F32),
                        pltpu.VMEM((2, D_MODEL, MOE_TH), BF16),
                        pltpu.VMEM((2, D_MODEL, MOE_TH), BF16),
                        pltpu.VMEM((2, MOE_TH, D_MODEL), BF16),
                        pltpu.SemaphoreType.DMA(()),
                        pltpu.SemaphoreType.DMA((2,)),
                        pltpu.SemaphoreType.DMA((2,))],
    )
    return pl.pallas_call(
        _moe_kernel,
        grid_spec=grid_spec,
        out_shape=jax.ShapeDtypeStruct((N_TOK * TOP_K + MOE_SB, D_MODEL), F32),
        compiler_params=_cparams(("arbitrary",)),
        name="moe",
    )(sb_expert, sb_off, sb_n, n_active, src_tok, dst_row, h2, w1, b1, w2, b2)


def _final_kernel(x1_ref, y0_ref, y1_ref, y2_ref, y3_ref, gate_ref, g2_ref, lg_ref, lb_ref, o_ref):
    y = gate_ref[:, 0:1] * y0_ref[...]
    for k, y_ref in enumerate((y1_ref, y2_ref, y3_ref), start=1):
        y = y + gate_ref[:, k:k + 1] * y_ref[...]
    z = DEEPNORM_ALPHA * x1_ref[...] + g2_ref[...] * y
    o_ref[...] = _layer_norm(z) * lg_ref[...] + lb_ref[...]


def _final(x1, y4, gates, g2, ln_g, ln_b):
    tiles_per_batch = SEQ // FIN_TM
    tiles = N_TOK // FIN_TM
    rows = lambda w: pl.BlockSpec((FIN_TM, w), lambda i: (i, 0))
    plane = lambda k: pl.BlockSpec((FIN_TM, D_MODEL), lambda i: (k * tiles + i, 0))
    vec = pl.BlockSpec((1, D_MODEL), lambda i: (0, 0))
    return pl.pallas_call(
        _final_kernel,
        grid=(tiles,),
        in_specs=[rows(D_MODEL), plane(0), plane(1), plane(2), plane(3), rows(LANES),
                  pl.BlockSpec((None, 1, D_MODEL), lambda i: (i // tiles_per_batch, 0, 0)), vec, vec],
        out_specs=rows(D_MODEL),
        out_shape=jax.ShapeDtypeStruct((N_TOK, D_MODEL), F32),
        compiler_params=_cparams(("arbitrary",)),
        name="final",
    )(x1, y4, y4, y4, y4, gates, g2, ln_g, ln_b)


def _rope_tables():
    rows = SEQ // GRID_W
    t = np.arange(SEQ)
    row = (t // GRID_W - rows // 2).astype(np.float32)
    col = (t % GRID_W - GRID_W // 2).astype(np.float32)
    inv_freq = jnp.asarray(ROPE_THETA, F32) ** (-jnp.arange(0, ROPE_AXIS_DIM, 2, dtype=F32) / ROPE_AXIS_DIM)
    ang_row = jnp.asarray(row)[:, None] * inv_freq[None, :]
    ang_col = jnp.asarray(col)[:, None] * inv_freq[None, :]
    zeros = jnp.zeros_like(ang_row)
    cos = jnp.concatenate([jnp.cos(ang_row)] * 2 + [jnp.cos(ang_col)] * 2, axis=-1)
    sin_lo = jnp.concatenate([-jnp.sin(ang_row), zeros, -jnp.sin(ang_col), zeros], axis=-1)
    sin_hi = jnp.concatenate([zeros, jnp.sin(ang_row), zeros, jnp.sin(ang_col)], axis=-1)
    return cos, sin_lo, sin_hi


def _routing(top_i, rank, counts):
    counts = counts.astype(jnp.int32)
    nsb = (counts + MOE_SB - 1) // MOE_SB
    sb_end = jnp.cumsum(nsb)
    sb_start = sb_end - nsb
    g = jnp.arange(MOE_G, dtype=jnp.int32)
    active = g < sb_end[-1]
    e_of_g = jnp.minimum(jnp.sum(g[:, None] >= sb_end[None, :], axis=1), N_EXPERTS - 1).astype(jnp.int32)
    first_row = (g - sb_start[e_of_g]) * MOE_SB
    n_of_g = jnp.where(active, jnp.clip(counts[e_of_g] - first_row, 0, MOE_SB), 0).astype(jnp.int32)
    order = jnp.argsort(-n_of_g, stable=True).astype(jnp.int32)
    place = jnp.zeros((MOE_G,), jnp.int32).at[order].set(g)
    sb_n = n_of_g[order]
    sb_off = (jnp.cumsum(sb_n) - sb_n).astype(jnp.int32)
    last_e = e_of_g[order[jnp.maximum(sb_end[-1] - 1, 0)]]
    sb_expert = jnp.where(sb_n > 0, e_of_g[order], last_e).astype(jnp.int32)
    assign = jnp.arange(N_TOK * TOP_K, dtype=jnp.int32)
    max_chunks = N_TOK // MOE_SB
    chunk_ids = jnp.arange(max_chunks, dtype=jnp.int32)
    base = sb_off[place[jnp.minimum(sb_start[:, None] + chunk_ids[None, :], MOE_G - 1)]]
    is_e = top_i[:, :, None] == jnp.arange(N_EXPERTS, dtype=jnp.int32)
    is_c = (rank // MOE_SB)[:, :, None] == chunk_ids
    base_e = jnp.sum(jnp.where(is_e[:, :, :, None], base[None, None], 0), axis=2)
    dest = (jnp.sum(jnp.where(is_c, base_e, 0), axis=-1) + rank % MOE_SB).reshape(-1)
    sorted_assign = lax.sort_key_val(dest, assign)[1]
    src_tok = sorted_assign // TOP_K
    dst_row = (sorted_assign % TOP_K) * N_TOK + src_tok
    src_tok = jnp.concatenate([src_tok, jnp.zeros((MOE_SB,), jnp.int32)])
    dst_row = jnp.concatenate([dst_row, N_TOK * TOP_K + jnp.arange(MOE_SB, dtype=jnp.int32)])
    n_active = sb_end[-1:].astype(jnp.int32)
    return sb_expert, sb_off, sb_n, n_active, src_tok, dst_row


def kernel(x, c, w_ada, b_ada, w_in, q_norm_w, k_norm_w, attn_norm_w, hgrn_lb, hgrn_norm_w, w_out, ln1_g, ln1_b, w_router, b_router, w_exp_in, b_exp_in, w_exp_out, b_exp_out, ln2_g, ln2_b):
    c_pad = jnp.zeros((8, D_MODEL), F32).at[:BATCH].set(c)
    cos, sin_lo, sin_hi = _rope_tables()
    x2 = x.reshape(N_TOK, D_MODEL)
    for l in range(DEPTH):
        mod = _ada(c_pad, w_ada[l], b_ada[l][None, :])[:BATCH]
        sh1, sc1, g1, sh2, sc2, g2 = [m.reshape(BATCH, 1, D_MODEL) for m in jnp.split(mod, 6, axis=-1)]

        proj = _proj(x2, sc1, sh1, w_in[l].astype(BF16))
        o_attn = _attention(proj, cos, sin_lo, sin_hi, q_norm_w[l][None, :], k_norm_w[l][None, :],
                            attn_norm_w[l][None, :])
        lb = jnp.cumsum(jax.nn.softmax(hgrn_lb.astype(F32), axis=1), axis=1)[:, l]
        o_r = _hgrn(proj, lb.reshape(2, 1, HGRN_WIDTH), hgrn_norm_w[l][None, :])

        w_o = w_out[l].astype(BF16)
        w_rt = jnp.zeros((D_MODEL, LANES), BF16).at[:, :N_EXPERTS].set(w_router[l].astype(BF16))
        b_rt = jnp.full((1, LANES), -1e30, F32).at[0, :N_EXPERTS].set(b_router[l])
        x1, h2, idx, gates, rank, counts = _mix(
            o_attn, o_r, w_o[:ATTN_WIDTH], w_o[ATTN_WIDTH:], x2, g1, sc2, sh2,
            ln1_g[l][None, :], ln1_b[l][None, :], w_rt, b_rt)
        sb_expert, sb_off, sb_n, n_active, src_tok, dst_row = _routing(idx[:, :TOP_K], rank[:, :TOP_K],
                                                                       counts[0, :N_EXPERTS])
        y4 = _moe(sb_expert, sb_off, sb_n, n_active, src_tok, dst_row, h2, w_exp_in[l],
                  b_exp_in[l].reshape(N_EXPERTS, 2 * MOE_NJ, MOE_TH), w_exp_out[l], b_exp_out[l][:, None, :])
        x2 = _final(x1, y4, gates, g2, ln2_g[l][None, :], ln2_b[l][None, :])
    return x2.reshape(BATCH, SEQ, D_MODEL)
```

```python
import math

import numpy as np
import jax
import jax.numpy as jnp
from jax import lax
from jax.experimental import pallas as pl
from jax.experimental.pallas import tpu as pltpu

F32 = jnp.float32
BF16 = jnp.bfloat16

D_MODEL = 2048
BATCH = 4
SEQ = 2048
DEPTH = 1
N_TOK = BATCH * SEQ
HEAD_DIM = 128
ATTN_WIDTH = 1024
N_Q_HEADS = 8
N_KV_HEADS = 2
KV_GROUP = 4
HGRN_WIDTH = 1024
N_HGRN_HEADS = 8
HGRN_CHUNK = 64
GRID_W = 64
ROPE_THETA = 10000.0
ROPE_AXIS_DIM = 64
N_EXPERTS = 32
TOP_K = 4
D_EXPERT = 2048
SWIGLU_LIMIT = 7.0
SWIGLU_ALPHA = 1.702
NORM_EPS = 1e-6
DEEPNORM_ALPHA = (2 * DEPTH) ** 0.25
PROJ_WIDTH = 6656
LANES = 128

COL_Q = 0
COL_K = 8
COL_V = 10
COL_QR = 12
COL_FF = 20
COL_FB = 28
COL_IN = 36
COL_GO = 44

V7X_VMEM_BYTES = 64 * 1024 * 1024
VMEM_LIMIT = V7X_VMEM_BYTES * 7 // 8

ADA_TN = 2048
PROJ_TM = 512
PROJ_TN = 3328
ATTN_TQ = 256
HGRN_HB = 4
HGRN_UN = 2
MIX_TM = 512
MIX_SUB = 128
MOE_ISSUE_UNROLL = 8
MOE_SB = 1024
MOE_TMI = 256
MOE_TH = 256
MOE_NJ = D_EXPERT // MOE_TH
MOE_G = N_TOK * TOP_K // MOE_SB + N_EXPERTS + 1
MOE_Q_TILE = MOE_SB // (MOE_NJ // 2) // (MOE_SB // MOE_TMI)
FIN_TM = 512


def _cparams(sem):
    return pltpu.CompilerParams(dimension_semantics=sem, vmem_limit_bytes=VMEM_LIMIT)


def _sigmoid(x):
    return 1.0 / (1.0 + jnp.exp(-x))


def _layer_norm(x):
    mu = jnp.mean(x, axis=-1, keepdims=True)
    xc = x - mu
    var = jnp.mean(xc * xc, axis=-1, keepdims=True)
    return xc * lax.rsqrt(var + NORM_EPS)


def _rms(x):
    return x * lax.rsqrt(jnp.mean(x * x, axis=-1, keepdims=True) + NORM_EPS)


def _ada_kernel(c_ref, w_ref, b_ref, o_ref):
    c = c_ref[...]
    ca = c * _sigmoid(c)
    o_ref[...] = jnp.dot(ca.astype(BF16), w_ref[...].astype(BF16),
                         preferred_element_type=F32) + b_ref[...]


def _ada(c_pad, w, b):
    n = w.shape[1]
    return pl.pallas_call(
        _ada_kernel,
        grid=(n // ADA_TN,),
        in_specs=[pl.BlockSpec((8, D_MODEL), lambda j: (0, 0)),
                  pl.BlockSpec((D_MODEL, ADA_TN), lambda j: (0, j)),
                  pl.BlockSpec((1, ADA_TN), lambda j: (0, j))],
        out_specs=pl.BlockSpec((8, ADA_TN), lambda j: (0, j)),
        out_shape=jax.ShapeDtypeStruct((8, n), F32),
        compiler_params=_cparams(("arbitrary",)),
        name="ada",
    )(c_pad, w, b)


def _proj_kernel(x_ref, sc_ref, sh_ref, w_ref, o_ref, h_ref):
    @pl.when(pl.program_id(1) == 0)
    def _():
        h = _layer_norm(x_ref[...]) * (1.0 + sc_ref[...]) + sh_ref[...]
        h_ref[...] = h.astype(BF16)

    o_ref[...] = jnp.dot(h_ref[...], w_ref[...], preferred_element_type=F32).astype(BF16)


def _proj(x2, sc, sh, w_bf):
    tiles_per_batch = SEQ // PROJ_TM
    return pl.pallas_call(
        _proj_kernel,
        grid=(N_TOK // PROJ_TM, PROJ_WIDTH // PROJ_TN),
        in_specs=[pl.BlockSpec((PROJ_TM, D_MODEL), lambda i, j: (i, 0)),
                  pl.BlockSpec((None, 1, D_MODEL), lambda i, j: (i // tiles_per_batch, 0, 0)),
                  pl.BlockSpec((None, 1, D_MODEL), lambda i, j: (i // tiles_per_batch, 0, 0)),
                  pl.BlockSpec((D_MODEL, PROJ_TN), lambda i, j: (0, j))],
        out_specs=pl.BlockSpec((PROJ_TM, PROJ_TN), lambda i, j: (i, j)),
        out_shape=jax.ShapeDtypeStruct((N_TOK, PROJ_WIDTH), BF16),
        scratch_shapes=[pltpu.VMEM((PROJ_TM, D_MODEL), BF16)],
        compiler_params=_cparams(("arbitrary", "arbitrary")),
        name="proj",
    )(x2, sc, sh, w_bf)


def _rope(x, cos, sin_lo, sin_hi):
    return (x * cos + pltpu.roll(x, 96, axis=1) * sin_lo + pltpu.roll(x, 32, axis=1) * sin_hi)


def _attn_kernel(q_ref, k_ref, v_ref, cq_ref, slq_ref, shq_ref, ck_ref, slk_ref, shk_ref,
                 qw_ref, kw_ref, aw_ref, o_ref, kr_ref, v1_ref):
    @pl.when(pl.program_id(2) == 0)
    def _():
        k = _rms(k_ref[...].astype(F32)) * kw_ref[...]
        kr_ref[...] = _rope(k, ck_ref[...], slk_ref[...], shk_ref[...]).astype(BF16)
        v1_ref[:, :HEAD_DIM] = v_ref[...]
        v1_ref[:, HEAD_DIM:] = jnp.ones((SEQ, HEAD_DIM), BF16)

    scale = math.log2(math.e) / math.sqrt(HEAD_DIM)
    cq = cq_ref[...]
    slq = slq_ref[...]
    shq = shq_ref[...]
    heads = [slice(h * HEAD_DIM, (h + 1) * HEAD_DIM) for h in range(KV_GROUP)]
    qs = []
    for cols in heads:
        q = _rms(q_ref[:, cols].astype(F32)) * qw_ref[...]
        qs.append((_rope(q, cq, slq, shq) * scale).astype(BF16))
    scores = [lax.dot_general(q, kr_ref[...], (((1,), (1,)), ((), ())), preferred_element_type=F32)
              for q in qs]
    outs = []
    for s in scores:
        p = jnp.exp2((s - jnp.max(s, axis=-1, keepdims=True)).astype(BF16))
        ov = jnp.dot(p, v1_ref[...], preferred_element_type=F32)
        outs.append(ov[:, :HEAD_DIM] / ov[:, HEAD_DIM:HEAD_DIM + 1])
    for cols, o in zip(heads, outs):
        o_ref[:, cols] = (_rms(o) * aw_ref[:, cols]).astype(BF16)


def _attention(proj, cos, sin_lo, sin_hi, qw, kw, aw):
    nq = SEQ // ATTN_TQ
    gw = KV_GROUP * HEAD_DIM
    tab_q = pl.BlockSpec((ATTN_TQ, HEAD_DIM), lambda b, g, i: (i, 0))
    tab_k = pl.BlockSpec((SEQ, HEAD_DIM), lambda b, g, i: (0, 0))
    return pl.pallas_call(
        _attn_kernel,
        grid=(BATCH, N_KV_HEADS, nq),
        in_specs=[pl.BlockSpec((ATTN_TQ, gw), lambda b, g, i: (b * nq + i, g)),
                  pl.BlockSpec((SEQ, HEAD_DIM), lambda b, g, i: (b, COL_K + g)),
                  pl.BlockSpec((SEQ, HEAD_DIM), lambda b, g, i: (b, COL_V + g)),
                  tab_q, tab_q, tab_q, tab_k, tab_k, tab_k,
                  pl.BlockSpec((1, HEAD_DIM), lambda b, g, i: (0, 0)),
                  pl.BlockSpec((1, HEAD_DIM), lambda b, g, i: (0, 0)),
                  pl.BlockSpec((1, gw), lambda b, g, i: (0, g))],
        out_specs=pl.BlockSpec((ATTN_TQ, gw), lambda b, g, i: (b * nq + i, g)),
        out_shape=jax.ShapeDtypeStruct((N_TOK, ATTN_WIDTH), BF16),
        scratch_shapes=[pltpu.VMEM((SEQ, HEAD_DIM), BF16), pltpu.VMEM((SEQ, 2 * HEAD_DIM), BF16)],
        compiler_params=_cparams(("arbitrary", "arbitrary", "arbitrary")),
        name="attn",
    )(proj, proj, proj, cos, sin_lo, sin_hi, cos, sin_lo, sin_hi, qw, kw, aw)


def _hgrn_kernel(qr_ref, ff_ref, fb_ref, iv_ref, go_ref, lb_ref, nw_ref, o_ref, acc_ref, st_ref):
    C = HGRN_CHUNK
    nc = SEQ // C
    trips = nc // HGRN_UN
    row = lax.broadcasted_iota(jnp.int32, (C, C), 0)
    col = lax.broadcasted_iota(jnp.int32, (C, C), 1)
    keeps = (row >= col, row <= col)
    lasts = (C - 1, 0)
    f_refs = (ff_ref, fb_ref)

    nt_dims = (((1,), (1,)), ((), ()))
    tn_dims = (((0,), (0,)), ((), ()))
    st_ref[...] = jnp.zeros_like(st_ref)

    def trip(it, finish):
        chains = []
        for h in range(HGRN_HB):
            cols = slice(h * HEAD_DIM, (h + 1) * HEAD_DIM)
            for d in range(2):
                for u in range(HGRN_UN):
                    n = it * HGRN_UN + u
                    cidx = n if d == 0 else nc - 1 - n
                    chains.append(dict(h=h, d=d, cols=cols, rows=pl.ds(pl.multiple_of(cidx * C, C), C)))

        for ch in chains:
            d = ch["d"]
            lb = lb_ref[d, :, ch["cols"]]
            fg = lb + (1.0 - lb) * _sigmoid(f_refs[d][ch["rows"], ch["cols"]].astype(F32))
            ch["kk"] = 1.0 - fg
            lf = jnp.log(fg)
            lf_hi = lf.astype(BF16)
            lf_lo = (lf - lf_hi.astype(F32)).astype(BF16)
            tri = jnp.where(keeps[d], 1.0, 0.0).astype(BF16)
            ch["b"] = (jnp.dot(tri, lf_hi, preferred_element_type=F32)
                       + jnp.dot(tri, lf_lo, preferred_element_type=F32))
        for ch in chains:
            b = ch["b"]
            bl = b[lasts[ch["d"]]:lasts[ch["d"]] + 1, :]
            qx = qr_ref[ch["rows"], ch["cols"]].astype(F32)
            ch["qd"] = (qx * _sigmoid(qx) * jnp.exp(b)).astype(BF16)
            kd = ch["kk"] * jnp.exp(-b)
            ch["decay"] = jnp.exp(bl)
            ku = (kd * ch["decay"]).astype(BF16)
            kd = kd.astype(BF16)
            ch["v"] = iv_ref[ch["rows"], ch["cols"]]
            ch["sc"] = lax.dot_general(ch["qd"], kd, nt_dims, preferred_element_type=F32)
            ch["u_t"] = lax.dot_general(ch["v"], ku, tn_dims, preferred_element_type=F32)
        for h in range(HGRN_HB):
            for d in range(2):
                state = st_ref[2 * h + d]
                for ch in chains:
                    if ch["h"] == h and ch["d"] == d:
                        ch["state"] = state.astype(BF16)
                        state = state * ch["decay"] + ch["u_t"]
                st_ref[2 * h + d] = state
        for ch in chains:
            sc = jnp.where(keeps[ch["d"]], ch["sc"], 0.0).astype(BF16)
            ch["o"] = (jnp.dot(sc, ch["v"], preferred_element_type=F32)
                       + lax.dot_general(ch["qd"], ch["state"], nt_dims, preferred_element_type=F32))
        for ch in chains:
            rows, cols = ch["rows"], ch["cols"]
            if finish:
                o = _rms(acc_ref[rows, cols] + ch["o"]) * nw_ref[:, cols]
                g = go_ref[rows, cols].astype(F32)
                o_ref[rows, cols] = (o * (g * _sigmoid(g))).astype(BF16)
            else:
                acc_ref[rows, cols] = ch["o"]

    def first_half(it, carry):
        trip(it, False)
        return carry

    def second_half(it, carry):
        trip(it, True)
        return carry

    lax.fori_loop(0, trips // 2, first_half, 0)
    lax.fori_loop(trips // 2, trips, second_half, 0)


def _hgrn(proj, lb, nw):
    width = HGRN_HB * HEAD_DIM

    def col(c0):
        return pl.BlockSpec((SEQ, width), lambda b, h: (b, c0 // HGRN_HB + h))

    return pl.pallas_call(
        _hgrn_kernel,
        grid=(BATCH, N_HGRN_HEADS // HGRN_HB),
        in_specs=[col(COL_QR), col(COL_FF), col(COL_FB), col(COL_IN), col(COL_GO),
                  pl.BlockSpec((2, 1, width), lambda b, h: (0, 0, h)),
                  pl.BlockSpec((1, width), lambda b, h: (0, h))],
        out_specs=pl.BlockSpec((SEQ, width), lambda b, h: (b, h)),
        out_shape=jax.ShapeDtypeStruct((N_TOK, HGRN_WIDTH), BF16),
        scratch_shapes=[pltpu.VMEM((SEQ, width), F32),
                        pltpu.VMEM((2 * HGRN_HB, HEAD_DIM, HEAD_DIM), F32)],
        compiler_params=_cparams(("arbitrary", "arbitrary")),
        name="hgrn",
    )(proj, proj, proj, proj, proj, lb, nw)


def _mix_kernel(oa_ref, or_ref, wa_ref, wr_ref, x_ref, g1_ref, sc_ref, sh_ref, lg_ref, lbias_ref,
                wrt_ref, brt_ref, x1_ref, h2_ref, idx_ref, gate_ref, rank_ref, cnt_ref, carry_ref):
    i = pl.program_id(0)

    @pl.when(i == 0)
    def _():
        carry_ref[...] = jnp.zeros_like(carry_ref)

    tm = MIX_SUB
    subs = [slice(u * tm, (u + 1) * tm) for u in range(MIX_TM // MIX_SUB)]
    lane = lax.broadcasted_iota(jnp.int32, (tm, LANES), 1)
    neg = jnp.float32(-jnp.inf)

    ys = [jnp.dot(oa_ref[rs, :], wa_ref[...], preferred_element_type=F32)
          + jnp.dot(or_ref[rs, :], wr_ref[...], preferred_element_type=F32) for rs in subs]
    h2s = []
    for rs, y in zip(subs, ys):
        x1 = _layer_norm(DEEPNORM_ALPHA * x_ref[rs, :] + g1_ref[...] * y) * lg_ref[...] + lbias_ref[...]
        x1_ref[rs, :] = x1
        h2 = _layer_norm(x1) * (1.0 + sc_ref[...]) + sh_ref[...]
        h2_ref[rs, :] = h2
        h2s.append(h2.astype(BF16))
    logit_list = [jnp.dot(h2, wrt_ref[...], preferred_element_type=F32) + brt_ref[...] for h2 in h2s]

    picks = []
    for work in logit_list:
        vals, sels = [], []
        for _ in range(TOP_K):
            m = jnp.max(work, axis=-1, keepdims=True)
            sel = jnp.min(jnp.where(work == m, lane, LANES), axis=-1, keepdims=True)
            vals.append(m)
            sels.append(sel)
            work = jnp.where(lane == sel, neg, work)
        es = [jnp.exp(v - vals[0]) for v in vals]
        multi = jnp.zeros((tm, LANES), F32)
        for sel in sels:
            multi = multi + jnp.where(lane == sel, 1.0, 0.0)
        picks.append((sels, es, es[0] + es[1] + es[2] + es[3], multi))

    r = lax.broadcasted_iota(jnp.int32, (tm, tm), 0)
    c = lax.broadcasted_iota(jnp.int32, (tm, tm), 1)
    strict = jnp.where(r > c, 1.0, 0.0).astype(BF16)
    within = [jnp.dot(strict, multi.astype(BF16), preferred_element_type=F32) for _, _, _, multi in picks]
    carry = carry_ref[...]
    for rs, (sels, es, denom, multi), inside in zip(subs, picks, within):
        before = inside + carry
        carry = carry + jnp.sum(multi, axis=0, keepdims=True)
        idx_out = jnp.zeros((tm, LANES), jnp.int32)
        gate_out = jnp.zeros((tm, LANES), F32)
        rank_out = jnp.zeros((tm, LANES), F32)
        for k in range(TOP_K):
            rk = jnp.sum(jnp.where(lane == sels[k], before, 0.0), axis=-1, keepdims=True)
            idx_out = jnp.where(lane == k, sels[k], idx_out)
            gate_out = jnp.where(lane == k, es[k] / denom, gate_out)
            rank_out = jnp.where(lane == k, rk, rank_out)
        idx_ref[rs, :] = idx_out
        gate_ref[rs, :] = gate_out
        rank_ref[rs, :] = rank_out.astype(jnp.int32)
    carry_ref[...] = carry
    cnt_ref[...] = carry


def _mix(o_attn, o_r, wa, wr, x2, g1, sc2, sh2, ln_g, ln_b, w_rt, b_rt):
    tiles_per_batch = SEQ // MIX_TM
    rows = lambda w: pl.BlockSpec((MIX_TM, w), lambda i: (i, 0))
    full = lambda a, b: pl.BlockSpec((a, b), lambda i: (0, 0))
    per_batch = pl.BlockSpec((None, 1, D_MODEL), lambda i: (i // tiles_per_batch, 0, 0))
    return pl.pallas_call(
        _mix_kernel,
        grid=(N_TOK // MIX_TM,),
        in_specs=[rows(ATTN_WIDTH), rows(HGRN_WIDTH), full(ATTN_WIDTH, D_MODEL), full(HGRN_WIDTH, D_MODEL),
                  rows(D_MODEL), per_batch, per_batch, per_batch, full(1, D_MODEL), full(1, D_MODEL),
                  full(D_MODEL, LANES), full(1, LANES)],
        out_specs=[rows(D_MODEL), rows(D_MODEL), rows(LANES), rows(LANES), rows(LANES), full(1, LANES)],
        out_shape=[jax.ShapeDtypeStruct((N_TOK, D_MODEL), F32),
                   jax.ShapeDtypeStruct((N_TOK, D_MODEL), F32),
                   jax.ShapeDtypeStruct((N_TOK, LANES), jnp.int32),
                   jax.ShapeDtypeStruct((N_TOK, LANES), F32),
                   jax.ShapeDtypeStruct((N_TOK, LANES), jnp.int32),
                   jax.ShapeDtypeStruct((1, LANES), F32)],
        scratch_shapes=[pltpu.VMEM((1, LANES), F32)],
        compiler_params=_cparams(("arbitrary",)),
        name="mix",
    )(o_attn, o_r, wa, wr, x2, g1, sc2, sh2, ln_g, ln_b, w_rt, b_rt)


def _wait_rows(n, make_copy):
    for bit in range(MOE_SB.bit_length()):
        @pl.when(((n >> bit) & 1) == 1)
        def _(bit=bit):
            make_copy(1 << bit).wait()


def _for_rows(lo, hi, fn):
    groups = (hi - lo) // MOE_ISSUE_UNROLL

    def group(q, carry):
        base = lo + q * MOE_ISSUE_UNROLL
        for u in range(MOE_ISSUE_UNROLL):
            fn(base + u, u % 2)
        return carry

    def single(r, carry):
        fn(r, 0)
        return carry

    lax.fori_loop(0, groups, group, 0)
    lax.fori_loop(lo + groups * MOE_ISSUE_UNROLL, hi, single, 0)


def _moe_kernel(se_ref, so_ref, sn_ref, na_ref, src_ref, dst_ref, h2_hbm, w1_hbm, b1_ref, w2_hbm, b2_ref,
                y_hbm, xg_ref, xb_ref, acc_ref, wf1g_ref, wf1l_ref, wf2_ref, wb1g_ref, wb1l_ref, wb2_ref,
                gsem, ssem, wsem):
    g = pl.program_id(0)
    nj = MOE_NJ
    n_assign = N_TOK * TOP_K
    n_slices = na_ref[0] * nj

    def weight_copies(t, s):
        e = se_ref[jnp.minimum(t // nj, MOE_G - 1)]
        col = pl.multiple_of((t % nj) * MOE_TH, MOE_TH)
        return (pltpu.make_async_copy(w1_hbm.at[e, :, pl.ds(col, MOE_TH)], wf1g_ref.at[s], wsem.at[s]),
                pltpu.make_async_copy(w1_hbm.at[e, :, pl.ds(D_EXPERT + col, MOE_TH)], wf1l_ref.at[s], wsem.at[s]),
                pltpu.make_async_copy(w2_hbm.at[e, pl.ds(col, MOE_TH), :], wf2_ref.at[s], wsem.at[s]))

    def cast_weights(s):
        wb1g_ref[s] = wf1g_ref[s].astype(BF16)
        wb1l_ref[s] = wf1l_ref[s].astype(BF16)
        wb2_ref[s] = wf2_ref[s].astype(BF16)

    def tiles_of(rows):
        return (rows + MOE_TMI - 1) // MOE_TMI

    n = sn_ref[g]
    nt = tiles_of(n)
    slot = g % 2
    g_next = jnp.minimum(g + 1, MOE_G - 1)
    n_next = jnp.where(g + 1 < MOE_G, sn_ref[g_next], 0)
    off_next = so_ref[g_next]
    g_prev = jnp.maximum(g - 1, 0)
    n_prev = jnp.where(g >= 1, sn_ref[g_prev], 0)
    off_prev = so_ref[g_prev]
    step_rows = nt * MOE_Q_TILE
    eager = (nj // 2) * step_rows

    def gather_copy(tok, r):
        return pltpu.make_async_copy(h2_hbm.at[pl.ds(tok, 1)], xg_ref.at[pl.ds(r, 1)], gsem)

    def scatter_copy(s, r, a):
        return pltpu.make_async_copy(acc_ref.at[s, pl.ds(r, 1)], y_hbm.at[pl.ds(a, 1)], ssem.at[s])

    def gather_start(r, parity=0):
        del parity
        gather_copy(src_ref[off_next + r], r).start(priority=0)

    def scatter_start(r, parity=0):
        scatter_copy(1 - slot, r, dst_ref[off_prev + r]).start(priority=parity)

    def eager_issue(first, count):
        for q in range(count):
            gather_start(first + q)
            scatter_start(first + q, q % 2)

    def tile_rows(i):
        return pl.ds(pl.multiple_of(i * MOE_TMI, MOE_TMI), MOE_TMI)

    def begin():
        @pl.when(g == 0)
        def _():
            xg_ref[...] = jnp.zeros_like(xg_ref)
            acc_ref[...] = jnp.zeros_like(acc_ref)
            spare = pltpu.make_async_copy(acc_ref.at[0], y_hbm.at[pl.ds(n_assign, MOE_SB)], ssem.at[0])
            spare.start()
            spare.wait()
            _for_rows(0, n, lambda r, parity: gather_copy(src_ref[so_ref[0] + r], r).start())
            for t in range(2):
                @pl.when(t < n_slices)
                def _(t=t):
                    for c in weight_copies(t, t):
                        c.start(priority=1)

            @pl.when(n_slices > 0)
            def _():
                for c in weight_copies(0, 0):
                    c.wait()
                cast_weights(0)

        nt_prev = tiles_of(n_prev)
        eager_prev = (nj // 2) * nt_prev * MOE_Q_TILE
        n_prev2 = jnp.where(g >= 2, sn_ref[jnp.maximum(g - 2, 0)], 0)
        gathered = jnp.maximum(eager_prev, n)
        scattered = jnp.maximum(eager_prev, n_prev2)
        _wait_rows(gathered,
                   lambda k: pltpu.make_async_copy(h2_hbm.at[pl.ds(0, k)], xg_ref.at[pl.ds(0, k)], gsem))
        _wait_rows(scattered,
                   lambda k: pltpu.make_async_copy(acc_ref.at[slot, pl.ds(0, k)], y_hbm.at[pl.ds(0, k)],
                                                   ssem.at[slot]))

        def prep(i, carry):
            rows = tile_rows(i)
            xb_ref[rows, :] = xg_ref[rows, :].astype(BF16)
            acc_ref[slot, rows, :] = jnp.broadcast_to(b2_ref[...], (MOE_TMI, D_MODEL))
            return carry

        lax.fori_loop(0, nt, prep, 0)

    begin()

    def hidden_slice(j, ws):
        t = g * nj + j

        @pl.when(t + 1 < n_slices)
        def _():
            for c in weight_copies(t + 1, 1 - ws):
                c.wait()

        @pl.when(t + 2 < n_slices)
        def _():
            for c in weight_copies(t + 2, ws):
                c.start(priority=1)

        b1g = b1_ref[pl.ds(j, 1), :]
        b1l = b1_ref[pl.ds(nj + j, 1), :]

        def tiles(first_tile, count, cast_next):
            if ws == 0:
                eager_issue(pl.multiple_of((j // 2) * step_rows + first_tile * MOE_Q_TILE, MOE_Q_TILE),
                            count * MOE_Q_TILE)
            rows = [tile_rows(first_tile + i) for i in range(count)]
            xs = [xb_ref[r, :] for r in rows]
            hid = [(jnp.dot(x, wb1g_ref[ws], preferred_element_type=F32) + b1g,
                    jnp.dot(x, wb1l_ref[ws], preferred_element_type=F32) + b1l) for x in xs]
            if cast_next:
                cast_weights(1 - ws)
            parts = []
            for hg, hl in hid:
                hg = jnp.minimum(hg, SWIGLU_LIMIT)
                hl = jnp.clip(hl, -SWIGLU_LIMIT, SWIGLU_LIMIT)
                act = hg * _sigmoid(SWIGLU_ALPHA * hg) * (hl + 1.0)
                parts.append(jnp.dot(act.astype(BF16), wb2_ref[ws], preferred_element_type=F32))
            for r, part in zip(rows, parts):
                acc_ref[slot, r, :] += part

        @pl.when(nt == 4)
        def _():
            tiles(0, 4, True)

        @pl.when((nt == 2) | (nt == 3))
        def _():
            tiles(0, 2, True)

        @pl.when(nt == 1)
        def _():
            tiles(0, 1, True)

        @pl.when(nt == 3)
        def _():
            tiles(2, 1, False)

    def slice_pair(jj, carry):
        for ws in range(2):
            hidden_slice(2 * jj + ws, ws)
        return carry

    @pl.when(nt > 0)
    def _():
        lax.fori_loop(0, nj // 2, slice_pair, 0)

    _for_rows(jnp.minimum(eager, n_next), n_next, gather_start)
    _for_rows(jnp.minimum(eager, n_prev), n_prev, scatter_start)

    @pl.when(g == MOE_G - 1)
    def _():
        _wait_rows(n_prev, lambda k: pltpu.make_async_copy(acc_ref.at[1 - slot, pl.ds(0, k)],
                                                            y_hbm.at[pl.ds(0, k)], ssem.at[1 - slot]))


def _moe(sb_expert, sb_off, sb_n, n_active, src_tok, dst_row, h2, w1, b1, w2, b2):
    grid_spec = pltpu.PrefetchScalarGridSpec(
        num_scalar_prefetch=6,
        grid=(MOE_G,),
        in_specs=[
            pl.BlockSpec(memory_space=pl.ANY),
            pl.BlockSpec(memory_space=pl.ANY),
            pl.BlockSpec((None, 2 * MOE_NJ, MOE_TH), lambda g, se, *_: (se[g], 0, 0)),
            pl.BlockSpec(memory_space=pl.ANY),
            pl.BlockSpec((None, 1, D_MODEL), lambda g, se, *_: (se[g], 0, 0)),
        ],
        out_specs=pl.BlockSpec(memory_space=pl.ANY),
        scratch_shapes=[pltpu.VMEM((MOE_SB, D_MODEL), F32),
                        pltpu.VMEM((MOE_SB, D_MODEL), BF16),
                        pltpu.VMEM((2, MOE_SB, D_MODEL), F32),
                        pltpu.VMEM((2, D_MODEL, MOE_TH), F32),
                        pltpu.VMEM((2, D_MODEL, MOE_TH), F32),
                        pltpu.VMEM((2, MOE_TH, D_MODEL), F32),
                        pltpu.VMEM((2, D_MODEL, MOE_TH), BF16),
                        pltpu.VMEM((2, D_MODEL, MOE_TH), BF16),
                        pltpu.VMEM((2, MOE_TH, D_MODEL), BF16),
                        pltpu.SemaphoreType.DMA(()),
                        pltpu.SemaphoreType.DMA((2,)),
                        pltpu.SemaphoreType.DMA((2,))],
    )
    return pl.pallas_call(
        _moe_kernel,
        grid_spec=grid_spec,
        out_shape=jax.ShapeDtypeStruct((N_TOK * TOP_K + MOE_SB, D_MODEL), F32),
        compiler_params=_cparams(("arbitrary",)),
        name="moe",
    )(sb_expert, sb_off, sb_n, n_active, src_tok, dst_row, h2, w1, b1, w2, b2)


def _final_kernel(x1_ref, y0_ref, y1_ref, y2_ref, y3_ref, gate_ref, g2_ref, lg_ref, lb_ref, o_ref):
    y = gate_ref[:, 0:1] * y0_ref[...]
    for k, y_ref in enumerate((y1_ref, y2_ref, y3_ref), start=1):
        y = y + gate_ref[:, k:k + 1] * y_ref[...]
    z = DEEPNORM_ALPHA * x1_ref[...] + g2_ref[...] * y
    o_ref[...] = _layer_norm(z) * lg_ref[...] + lb_ref[...]


def _final(x1, y4, gates, g2, ln_g, ln_b):
    tiles_per_batch = SEQ // FIN_TM
    tiles = N_TOK // FIN_TM
    rows = lambda w: pl.BlockSpec((FIN_TM, w), lambda i: (i, 0))
    plane = lambda k: pl.BlockSpec((FIN_TM, D_MODEL), lambda i: (k * tiles + i, 0))
    vec = pl.BlockSpec((1, D_MODEL), lambda i: (0, 0))
    return pl.pallas_call(
        _final_kernel,
        grid=(tiles,),
        in_specs=[rows(D_MODEL), plane(0), plane(1), plane(2), plane(3), rows(LANES),
                  pl.BlockSpec((None, 1, D_MODEL), lambda i: (i // tiles_per_batch, 0, 0)), vec, vec],
        out_specs=rows(D_MODEL),
        out_shape=jax.ShapeDtypeStruct((N_TOK, D_MODEL), F32),
        compiler_params=_cparams(("arbitrary",)),
        name="final",
    )(x1, y4, y4, y4, y4, gates, g2, ln_g, ln_b)


def _rope_tables():
    rows = SEQ // GRID_W
    t = np.arange(SEQ)
    row = (t // GRID_W - rows // 2).astype(np.float32)
    col = (t % GRID_W - GRID_W // 2).astype(np.float32)
    inv_freq = jnp.asarray(ROPE_THETA, F32) ** (-jnp.arange(0, ROPE_AXIS_DIM, 2, dtype=F32) / ROPE_AXIS_DIM)
    ang_row = jnp.asarray(row)[:, None] * inv_freq[None, :]
    ang_col = jnp.asarray(col)[:, None] * inv_freq[None, :]
    zeros = jnp.zeros_like(ang_row)
    cos = jnp.concatenate([jnp.cos(ang_row)] * 2 + [jnp.cos(ang_col)] * 2, axis=-1)
    sin_lo = jnp.concatenate([-jnp.sin(ang_row), zeros, -jnp.sin(ang_col), zeros], axis=-1)
    sin_hi = jnp.concatenate([zeros, jnp.sin(ang_row), zeros, jnp.sin(ang_col)], axis=-1)
    return cos, sin_lo, sin_hi


def _routing(top_i, rank, counts):
    counts = counts.astype(jnp.int32)
    nsb = (counts + MOE_SB - 1) // MOE_SB
    sb_end = jnp.cumsum(nsb)
    sb_start = sb_end - nsb
    g = jnp.arange(MOE_G, dtype=jnp.int32)
    active = g < sb_end[-1]
    e_of_g = jnp.minimum(jnp.sum(g[:, None] >= sb_end[None, :], axis=1), N_EXPERTS - 1).astype(jnp.int32)
    first_row = (g - sb_start[e_of_g]) * MOE_SB
    n_of_g = jnp.where(active, jnp.clip(counts[e_of_g] - first_row, 0, MOE_SB), 0).astype(jnp.int32)
    order = jnp.argsort(-n_of_g, stable=True).astype(jnp.int32)
    place = jnp.zeros((MOE_G,), jnp.int32).at[order].set(g)
    sb_n = n_of_g[order]
    sb_off = (jnp.cumsum(sb_n) - sb_n).astype(jnp.int32)
    last_e = e_of_g[order[jnp.maximum(sb_end[-1] - 1, 0)]]
    sb_expert = jnp.where(sb_n > 0, e_of_g[order], last_e).astype(jnp.int32)
    assign = jnp.arange(N_TOK * TOP_K, dtype=jnp.int32)
    max_chunks = N_TOK // MOE_SB
    chunk_ids = jnp.arange(max_chunks, dtype=jnp.int32)
    base = sb_off[place[jnp.minimum(sb_start[:, None] + chunk_ids[None, :], MOE_G - 1)]]
    is_e = top_i[:, :, None] == jnp.arange(N_EXPERTS, dtype=jnp.int32)
    is_c = (rank // MOE_SB)[:, :, None] == chunk_ids
    base_e = jnp.sum(jnp.where(is_e[:, :, :, None], base[None, None], 0), axis=2)
    dest = (jnp.sum(jnp.where(is_c, base_e, 0), axis=-1) + rank % MOE_SB).reshape(-1)
    sorted_assign = lax.sort_key_val(dest, assign)[1]
    src_tok = sorted_assign // TOP_K
    dst_row = (sorted_assign % TOP_K) * N_TOK + src_tok
    src_tok = jnp.concatenate([src_tok, jnp.zeros((MOE_SB,), jnp.int32)])
    dst_row = jnp.concatenate([dst_row, N_TOK * TOP_K + jnp.arange(MOE_SB, dtype=jnp.int32)])
    n_active = sb_end[-1:].astype(jnp.int32)
    return sb_expert, sb_off, sb_n, n_active, src_tok, dst_row


def kernel(x, c, w_ada, b_ada, w_in, q_norm_w, k_norm_w, attn_norm_w, hgrn_lb, hgrn_norm_w, w_out, ln1_g, ln1_b, w_router, b_router, w_exp_in, b_exp_in, w_exp_out, b_exp_out, ln2_g, ln2_b):
    c_pad = jnp.zeros((8, D_MODEL), F32).at[:BATCH].set(c)
    cos, sin_lo, sin_hi = _rope_tables()
    x2 = x.reshape(N_TOK, D_MODEL)
    for l in range(DEPTH):
        mod = _ada(c_pad, w_ada[l], b_ada[l][None, :])[:BATCH]
        sh1, sc1, g1, sh2, sc2, g2 = [m.reshape(BATCH, 1, D_MODEL) for m in jnp.split(mod, 6, axis=-1)]

        proj = _proj(x2, sc1, sh1, w_in[l].astype(BF16))
        o_attn = _attention(proj, cos, sin_lo, sin_hi, q_norm_w[l][None, :], k_norm_w[l][None, :],
                            attn_norm_w[l][None, :])
        lb = jnp.cumsum(jax.nn.softmax(hgrn_lb.astype(F32), axis=1), axis=1)[:, l]
        o_r = _hgrn(proj, lb.reshape(2, 1, HGRN_WIDTH), hgrn_norm_w[l][None, :])

        w_o = w_out[l].astype(BF16)
        w_rt = jnp.zeros((D_MODEL, LANES), BF16).at[:, :N_EXPERTS].set(w_router[l].astype(BF16))
        b_rt = jnp.full((1, LANES), -1e30, F32).at[0, :N_EXPERTS].set(b_router[l])
        x1, h2, idx, gates, rank, counts = _mix(
            o_attn, o_r, w_o[:ATTN_WIDTH], w_o[ATTN_WIDTH:], x2, g1, sc2, sh2,
            ln1_g[l][None, :], ln1_b[l][None, :], w_rt, b_rt)
        sb_expert, sb_off, sb_n, n_active, src_tok, dst_row = _routing(idx[:, :TOP_K], rank[:, :TOP_K],
                                                                       counts[0, :N_EXPERTS])
        y4 = _moe(sb_expert, sb_off, sb_n, n_active, src_tok, dst_row, h2, w_exp_in[l],
                  b_exp_in[l].reshape(N_EXPERTS, 2 * MOE_NJ, MOE_TH), w_exp_out[l], b_exp_out[l][:, None, :])
        x2 = _final(x1, y4, gates, g2, ln2_g[l][None, :], ln2_b[l][None, :])
    return x2.reshape(BATCH, SEQ, D_MODEL)
```

```python
import math

import numpy as np
import jax
import jax.numpy as jnp
from jax import lax
from jax.experimental import pallas as pl
from jax.experimental.pallas import tpu as pltpu

F32 = jnp.float32
BF16 = jnp.bfloat16

D_MODEL = 2048
BATCH = 4
SEQ = 2048
DEPTH = 1
N_TOK = BATCH * SEQ
HEAD_DIM = 128
ATTN_WIDTH = 1024
N_Q_HEADS = 8
N_KV_HEADS = 2
KV_GROUP = 4
HGRN_WIDTH = 1024
N_HGRN_HEADS = 8
HGRN_CHUNK = 64
GRID_W = 64
ROPE_THETA = 10000.0
ROPE_AXIS_DIM = 64
N_EXPERTS = 32
TOP_K = 4
D_EXPERT = 2048
SWIGLU_LIMIT = 7.0
SWIGLU_ALPHA = 1.702
NORM_EPS = 1e-6
DEEPNORM_ALPHA = (2 * DEPTH) ** 0.25
PROJ_WIDTH = 6656
LANES = 128

COL_Q = 0
COL_K = 8
COL_V = 10
COL_QR = 12
COL_FF = 20
COL_FB = 28
COL_IN = 36
COL_GO = 44

V7X_VMEM_BYTES = 64 * 1024 * 1024
VMEM_LIMIT = V7X_VMEM_BYTES * 7 // 8

ADA_TN = 2048
PROJ_TM = 512
PROJ_TN = 3328
ATTN_TQ = 256
HGRN_HB = 4
HGRN_UN = 2
MIX_TRIPS = 2
MIX_TM = 512
MIX_SUB = 128
MOE_ISSUE_UNROLL = 8
MOE_SB = 1024
MOE_TMI = 256
MOE_TH = 256
MOE_NJ = D_EXPERT // MOE_TH
MOE_G = N_TOK * TOP_K // MOE_SB + N_EXPERTS + 1
MOE_Q_TILE = MOE_SB // (MOE_NJ // 2) // (MOE_SB // MOE_TMI)
FIN_TM = 512


def _cparams(sem):
    return pltpu.CompilerParams(dimension_semantics=sem, vmem_limit_bytes=VMEM_LIMIT)


def _sigmoid(x):
    return 1.0 / (1.0 + jnp.exp(-x))


def _layer_norm(x):
    mu = jnp.mean(x, axis=-1, keepdims=True)
    xc = x - mu
    var = jnp.mean(xc * xc, axis=-1, keepdims=True)
    return xc * lax.rsqrt(var + NORM_EPS)


def _rms(x):
    return x * lax.rsqrt(jnp.mean(x * x, axis=-1, keepdims=True) + NORM_EPS)


def _ada_kernel(c_ref, w_ref, b_ref, o_ref):
    c = c_ref[...]
    ca = c * _sigmoid(c)
    o_ref[...] = jnp.dot(ca.astype(BF16), w_ref[...].astype(BF16),
                         preferred_element_type=F32) + b_ref[...]


def _ada(c_pad, w, b):
    n = w.shape[1]
    return pl.pallas_call(
        _ada_kernel,
        grid=(n // ADA_TN,),
        in_specs=[pl.BlockSpec((8, D_MODEL), lambda j: (0, 0)),
                  pl.BlockSpec((D_MODEL, ADA_TN), lambda j: (0, j)),
                  pl.BlockSpec((1, ADA_TN), lambda j: (0, j))],
        out_specs=pl.BlockSpec((8, ADA_TN), lambda j: (0, j)),
        out_shape=jax.ShapeDtypeStruct((8, n), F32),
        compiler_params=_cparams(("arbitrary",)),
        name="ada",
    )(c_pad, w, b)


def _proj_kernel(x_ref, sc_ref, sh_ref, w_ref, o_ref, h_ref):
    @pl.when(pl.program_id(1) == 0)
    def _():
        h = _layer_norm(x_ref[...]) * (1.0 + sc_ref[...]) + sh_ref[...]
        h_ref[...] = h.astype(BF16)

    o_ref[...] = jnp.dot(h_ref[...], w_ref[...], preferred_element_type=F32).astype(BF16)


def _proj(x2, sc, sh, w_bf):
    tiles_per_batch = SEQ // PROJ_TM
    return pl.pallas_call(
        _proj_kernel,
        grid=(N_TOK // PROJ_TM, PROJ_WIDTH // PROJ_TN),
        in_specs=[pl.BlockSpec((PROJ_TM, D_MODEL), lambda i, j: (i, 0)),
                  pl.BlockSpec((None, 1, D_MODEL), lambda i, j: (i // tiles_per_batch, 0, 0)),
                  pl.BlockSpec((None, 1, D_MODEL), lambda i, j: (i // tiles_per_batch, 0, 0)),
                  pl.BlockSpec((D_MODEL, PROJ_TN), lambda i, j: (0, j))],
        out_specs=pl.BlockSpec((PROJ_TM, PROJ_TN), lambda i, j: (i, j)),
        out_shape=jax.ShapeDtypeStruct((N_TOK, PROJ_WIDTH), BF16),
        scratch_shapes=[pltpu.VMEM((PROJ_TM, D_MODEL), BF16)],
        compiler_params=_cparams(("arbitrary", "arbitrary")),
        name="proj",
    )(x2, sc, sh, w_bf)


def _rope(x, cos, sin_lo, sin_hi):
    return (x * cos + pltpu.roll(x, 96, axis=1) * sin_lo + pltpu.roll(x, 32, axis=1) * sin_hi)


def _mixers_kernel(q_ref, k_ref, v_ref, cq_ref, slq_ref, shq_ref, ck_ref, slk_ref, shk_ref, qw_ref, kw_ref, aw_ref,
                   qr_ref, ff_ref, fb_ref, iv_ref, go_ref, lb_ref, nw_ref,
                   o_ref, or_ref, kr_ref, v1_ref, acc_ref, st_ref):
    i = pl.program_id(2)
    trip = _hgrn_trip_fn(qr_ref, ff_ref, fb_ref, iv_ref, go_ref, lb_ref, nw_ref, or_ref, acc_ref, st_ref)

    @pl.when(i == 0)
    def _():
        k = _rms(k_ref[...].astype(F32)) * kw_ref[...]
        kr_ref[...] = _rope(k, ck_ref[...], slk_ref[...], shk_ref[...]).astype(BF16)
        v1_ref[:, :HEAD_DIM] = v_ref[...]
        v1_ref[:, HEAD_DIM:] = jnp.ones((SEQ, HEAD_DIM), BF16)
        st_ref[...] = jnp.zeros_like(st_ref)

    def body(finish):
        scale = math.log2(math.e) / math.sqrt(HEAD_DIM)
        cq = cq_ref[...]
        slq = slq_ref[...]
        shq = shq_ref[...]
        heads = [slice(h * HEAD_DIM, (h + 1) * HEAD_DIM) for h in range(KV_GROUP)]
        qs = []
        for cols in heads:
            q = _rms(q_ref[:, cols].astype(F32)) * qw_ref[...]
            qs.append((_rope(q, cq, slq, shq) * scale).astype(BF16))
        scores = [lax.dot_general(q, kr_ref[...], (((1,), (1,)), ((), ())), preferred_element_type=F32)
                  for q in qs]
        trip(MIX_TRIPS * i, finish)
        outs = []
        for s in scores:
            p = jnp.exp2((s - jnp.max(s, axis=-1, keepdims=True)).astype(BF16))
            ov = jnp.dot(p, v1_ref[...], preferred_element_type=F32)
            outs.append(ov[:, :HEAD_DIM] / ov[:, HEAD_DIM:HEAD_DIM + 1])
        for u in range(1, MIX_TRIPS):
            trip(MIX_TRIPS * i + u, finish)
        for cols, o in zip(heads, outs):
            o_ref[:, cols] = (_rms(o) * aw_ref[:, cols]).astype(BF16)

    half = pl.num_programs(2) // 2

    @pl.when(i < half)
    def _():
        body(False)

    @pl.when(i >= half)
    def _():
        body(True)


def _mixers(proj, cos, sin_lo, sin_hi, qw, kw, aw, lb, nw):
    nq = SEQ // ATTN_TQ
    gw = KV_GROUP * HEAD_DIM
    assert HGRN_HB * HEAD_DIM == gw and N_HGRN_HEADS // HGRN_HB == N_KV_HEADS
    assert MIX_TRIPS * nq == SEQ // HGRN_CHUNK // HGRN_UN and nq % 2 == 0
    tab_q = pl.BlockSpec((ATTN_TQ, HEAD_DIM), lambda b, g, i: (i, 0))
    tab_k = pl.BlockSpec((SEQ, HEAD_DIM), lambda b, g, i: (0, 0))

    def col(c0):
        return pl.BlockSpec((SEQ, gw), lambda b, g, i: (b, c0 // HGRN_HB + g))

    return pl.pallas_call(
        _mixers_kernel,
        grid=(BATCH, N_KV_HEADS, nq),
        in_specs=[pl.BlockSpec((ATTN_TQ, gw), lambda b, g, i: (b * nq + i, g)),
                  pl.BlockSpec((SEQ, HEAD_DIM), lambda b, g, i: (b, COL_K + g)),
                  pl.BlockSpec((SEQ, HEAD_DIM), lambda b, g, i: (b, COL_V + g)),
                  tab_q, tab_q, tab_q, tab_k, tab_k, tab_k,
                  pl.BlockSpec((1, HEAD_DIM), lambda b, g, i: (0, 0)),
                  pl.BlockSpec((1, HEAD_DIM), lambda b, g, i: (0, 0)),
                  pl.BlockSpec((1, gw), lambda b, g, i: (0, g)),
                  col(COL_QR), col(COL_FF), col(COL_FB), col(COL_IN), col(COL_GO),
                  pl.BlockSpec((2, 1, gw), lambda b, g, i: (0, 0, g)),
                  pl.BlockSpec((1, gw), lambda b, g, i: (0, g))],
        out_specs=[pl.BlockSpec((ATTN_TQ, gw), lambda b, g, i: (b * nq + i, g)),
                   pl.BlockSpec((SEQ, gw), lambda b, g, i: (b, g))],
        out_shape=[jax.ShapeDtypeStruct((N_TOK, ATTN_WIDTH), BF16),
                   jax.ShapeDtypeStruct((N_TOK, HGRN_WIDTH), BF16)],
        scratch_shapes=[pltpu.VMEM((SEQ, HEAD_DIM), BF16), pltpu.VMEM((SEQ, 2 * HEAD_DIM), BF16),
                        pltpu.VMEM((SEQ, gw), F32),
                        pltpu.VMEM((2 * HGRN_HB, HEAD_DIM, HEAD_DIM), F32)],
        compiler_params=_cparams(("arbitrary", "arbitrary", "arbitrary")),
        name="mixers",
    )(proj, proj, proj, cos, sin_lo, sin_hi, cos, sin_lo, sin_hi, qw, kw, aw,
      proj, proj, proj, proj, proj, lb, nw)


def _hgrn_trip_fn(qr_ref, ff_ref, fb_ref, iv_ref, go_ref, lb_ref, nw_ref, o_ref, acc_ref, st_ref):
    C = HGRN_CHUNK
    nc = SEQ // C
    row = lax.broadcasted_iota(jnp.int32, (C, C), 0)
    col = lax.broadcasted_iota(jnp.int32, (C, C), 1)
    keeps = (row >= col, row <= col)
    lasts = (C - 1, 0)
    f_refs = (ff_ref, fb_ref)

    nt_dims = (((1,), (1,)), ((), ()))
    tn_dims = (((0,), (0,)), ((), ()))

    def trip(it, finish):
        chains = []
        for h in range(HGRN_HB):
            cols = slice(h * HEAD_DIM, (h + 1) * HEAD_DIM)
            for d in range(2):
                for u in range(HGRN_UN):
                    n = it * HGRN_UN + u
                    cidx = n if d == 0 else nc - 1 - n
                    chains.append(dict(h=h, d=d, cols=cols, rows=pl.ds(pl.multiple_of(cidx * C, C), C)))

        for ch in chains:
            d = ch["d"]
            lb = lb_ref[d, :, ch["cols"]]
            fg = lb + (1.0 - lb) * _sigmoid(f_refs[d][ch["rows"], ch["cols"]].astype(F32))
            ch["kk"] = 1.0 - fg
            lf = jnp.log(fg)
            lf_hi = lf.astype(BF16)
            lf_lo = (lf - lf_hi.astype(F32)).astype(BF16)
            tri = jnp.where(keeps[d], 1.0, 0.0).astype(BF16)
            ch["b"] = (jnp.dot(tri, lf_hi, preferred_element_type=F32)
                       + jnp.dot(tri, lf_lo, preferred_element_type=F32))
        for ch in chains:
            b = ch["b"]
            bl = b[lasts[ch["d"]]:lasts[ch["d"]] + 1, :]
            qx = qr_ref[ch["rows"], ch["cols"]].astype(F32)
            ch["qd"] = (qx * _sigmoid(qx) * jnp.exp(b)).astype(BF16)
            kd = ch["kk"] * jnp.exp(-b)
            ch["decay"] = jnp.exp(bl)
            ku = (kd * ch["decay"]).astype(BF16)
            kd = kd.astype(BF16)
            ch["v"] = iv_ref[ch["rows"], ch["cols"]]
            ch["sc"] = lax.dot_general(ch["qd"], kd, nt_dims, preferred_element_type=F32)
            ch["u_t"] = lax.dot_general(ch["v"], ku, tn_dims, preferred_element_type=F32)
        for h in range(HGRN_HB):
            for d in range(2):
                state = st_ref[2 * h + d]
                for ch in chains:
                    if ch["h"] == h and ch["d"] == d:
                        ch["state"] = state.astype(BF16)
                        state = state * ch["decay"] + ch["u_t"]
                st_ref[2 * h + d] = state
        for ch in chains:
            sc = jnp.where(keeps[ch["d"]], ch["sc"], 0.0).astype(BF16)
            ch["o"] = (jnp.dot(sc, ch["v"], preferred_element_type=F32)
                       + lax.dot_general(ch["qd"], ch["state"], nt_dims, preferred_element_type=F32))
        for ch in chains:
            rows, cols = ch["rows"], ch["cols"]
            if finish:
                o = _rms(acc_ref[rows, cols] + ch["o"]) * nw_ref[:, cols]
                g = go_ref[rows, cols].astype(F32)
                o_ref[rows, cols] = (o * (g * _sigmoid(g))).astype(BF16)
            else:
                acc_ref[rows, cols] = ch["o"]

    return trip


def _mix_kernel(oa_ref, or_ref, wa_ref, wr_ref, x_ref, g1_ref, sc_ref, sh_ref, lg_ref, lbias_ref,
                wrt_ref, brt_ref, x1_ref, h2_ref, idx_ref, gate_ref, rank_ref, cnt_ref, carry_ref):
    i = pl.program_id(0)

    @pl.when(i == 0)
    def _():
        carry_ref[...] = jnp.zeros_like(carry_ref)

    tm = MIX_SUB
    subs = [slice(u * tm, (u + 1) * tm) for u in range(MIX_TM // MIX_SUB)]
    lane = lax.broadcasted_iota(jnp.int32, (tm, LANES), 1)
    neg = jnp.float32(-jnp.inf)

    ys = [jnp.dot(oa_ref[rs, :], wa_ref[...], preferred_element_type=F32)
          + jnp.dot(or_ref[rs, :], wr_ref[...], preferred_element_type=F32) for rs in subs]
    h2s = []
    for rs, y in zip(subs, ys):
        x1 = _layer_norm(DEEPNORM_ALPHA * x_ref[rs, :] + g1_ref[...] * y) * lg_ref[...] + lbias_ref[...]
        x1_ref[rs, :] = x1
        h2 = _layer_norm(x1) * (1.0 + sc_ref[...]) + sh_ref[...]
        h2_ref[rs, :] = h2
        h2s.append(h2.astype(BF16))
    logit_list = [jnp.dot(h2, wrt_ref[...], preferred_element_type=F32) + brt_ref[...] for h2 in h2s]

    picks = []
    for work in logit_list:
        vals, sels = [], []
        for _ in range(TOP_K):
            m = jnp.max(work, axis=-1, keepdims=True)
            sel = jnp.min(jnp.where(work == m, lane, LANES), axis=-1, keepdims=True)
            vals.append(m)
            sels.append(sel)
            work = jnp.where(lane == sel, neg, work)
        es = [jnp.exp(v - vals[0]) for v in vals]
        multi = jnp.zeros((tm, LANES), F32)
        for sel in sels:
            multi = multi + jnp.where(lane == sel, 1.0, 0.0)
        picks.append((sels, es, es[0] + es[1] + es[2] + es[3], multi))

    r = lax.broadcasted_iota(jnp.int32, (tm, tm), 0)
    c = lax.broadcasted_iota(jnp.int32, (tm, tm), 1)
    strict = jnp.where(r > c, 1.0, 0.0).astype(BF16)
    within = [jnp.dot(strict, multi.astype(BF16), preferred_element_type=F32) for _, _, _, multi in picks]
    carry = carry_ref[...]
    for rs, (sels, es, denom, multi), inside in zip(subs, picks, within):
        before = inside + carry
        carry = carry + jnp.sum(multi, axis=0, keepdims=True)
        idx_out = jnp.zeros((tm, LANES), jnp.int32)
        gate_out = jnp.zeros((tm, LANES), F32)
        rank_out = jnp.zeros((tm, LANES), F32)
        for k in range(TOP_K):
            rk = jnp.sum(jnp.where(lane == sels[k], before, 0.0), axis=-1, keepdims=True)
            idx_out = jnp.where(lane == k, sels[k], idx_out)
            gate_out = jnp.where(lane == k, es[k] / denom, gate_out)
            rank_out = jnp.where(lane == k, rk, rank_out)
        idx_ref[rs, :] = idx_out
        gate_ref[rs, :] = gate_out
        rank_ref[rs, :] = rank_out.astype(jnp.int32)
    carry_ref[...] = carry
    cnt_ref[...] = carry


def _mix(o_attn, o_r, wa, wr, x2, g1, sc2, sh2, ln_g, ln_b, w_rt, b_rt):
    tiles_per_batch = SEQ // MIX_TM
    rows = lambda w: pl.BlockSpec((MIX_TM, w), lambda i: (i, 0))
    full = lambda a, b: pl.BlockSpec((a, b), lambda i: (0, 0))
    per_batch = pl.BlockSpec((None, 1, D_MODEL), lambda i: (i // tiles_per_batch, 0, 0))
    return pl.pallas_call(
        _mix_kernel,
        grid=(N_TOK // MIX_TM,),
        in_specs=[rows(ATTN_WIDTH), rows(HGRN_WIDTH), full(ATTN_WIDTH, D_MODEL), full(HGRN_WIDTH, D_MODEL),
                  rows(D_MODEL), per_batch, per_batch, per_batch, full(1, D_MODEL), full(1, D_MODEL),
                  full(D_MODEL, LANES), full(1, LANES)],
        out_specs=[rows(D_MODEL), rows(D_MODEL), rows(LANES), rows(LANES), rows(LANES), full(1, LANES)],
        out_shape=[jax.ShapeDtypeStruct((N_TOK, D_MODEL), F32),
                   jax.ShapeDtypeStruct((N_TOK, D_MODEL), F32),
                   jax.ShapeDtypeStruct((N_TOK, LANES), jnp.int32),
                   jax.ShapeDtypeStruct((N_TOK, LANES), F32),
                   jax.ShapeDtypeStruct((N_TOK, LANES), jnp.int32),
                   jax.ShapeDtypeStruct((1, LANES), F32)],
        scratch_shapes=[pltpu.VMEM((1, LANES), F32)],
        compiler_params=_cparams(("arbitrary",)),
        name="mix",
    )(o_attn, o_r, wa, wr, x2, g1, sc2, sh2, ln_g, ln_b, w_rt, b_rt)


def _wait_rows(n, make_copy):
    for bit in range(MOE_SB.bit_length()):
        @pl.when(((n >> bit) & 1) == 1)
        def _(bit=bit):
            make_copy(1 << bit).wait()


def _for_rows(lo, hi, fn):
    groups = (hi - lo) // MOE_ISSUE_UNROLL

    def group(q, carry):
        base = lo + q * MOE_ISSUE_UNROLL
        for u in range(MOE_ISSUE_UNROLL):
            fn(base + u)
        return carry

    def single(r, carry):
        fn(r)
        return carry

    lax.fori_loop(0, groups, group, 0)
    lax.fori_loop(lo + groups * MOE_ISSUE_UNROLL, hi, single, 0)


def _moe_kernel(se_ref, so_ref, sn_ref, na_ref, src_ref, dst_ref, h2_hbm, w1_hbm, b1_ref, w2_hbm, b2_ref,
                y_hbm, xg_ref, xb_ref, acc_ref, wf1g_ref, wf1l_ref, wf2_ref, wb1g_ref, wb1l_ref, wb2_ref,
                gsem, ssem, wsem):
    g = pl.program_id(0)
    nj = MOE_NJ
    n_assign = N_TOK * TOP_K
    n_slices = na_ref[0] * nj

    def weight_copies(t, s):
        e = se_ref[jnp.minimum(t // nj, MOE_G - 1)]
        col = pl.multiple_of((t % nj) * MOE_TH, MOE_TH)
        return (pltpu.make_async_copy(w1_hbm.at[e, :, pl.ds(col, MOE_TH)], wf1g_ref.at[s], wsem.at[s]),
                pltpu.make_async_copy(w1_hbm.at[e, :, pl.ds(D_EXPERT + col, MOE_TH)], wf1l_ref.at[s], wsem.at[s]),
                pltpu.make_async_copy(w2_hbm.at[e, pl.ds(col, MOE_TH), :], wf2_ref.at[s], wsem.at[s]))

    def cast_weights(s):
        wb1g_ref[s] = wf1g_ref[s].astype(BF16)
        wb1l_ref[s] = wf1l_ref[s].astype(BF16)
        wb2_ref[s] = wf2_ref[s].astype(BF16)

    def tiles_of(rows):
        return (rows + MOE_TMI - 1) // MOE_TMI

    n = sn_ref[g]
    nt = tiles_of(n)
    slot = g % 2
    g_next = jnp.minimum(g + 1, MOE_G - 1)
    n_next = jnp.where(g + 1 < MOE_G, sn_ref[g_next], 0)
    off_next = so_ref[g_next]
    g_prev = jnp.maximum(g - 1, 0)
    n_prev = jnp.where(g >= 1, sn_ref[g_prev], 0)
    off_prev = so_ref[g_prev]
    step_rows = nt * MOE_Q_TILE
    eager = (nj // 2) * step_rows

    def gather_copy(tok, r):
        return pltpu.make_async_copy(h2_hbm.at[pl.ds(tok, 1)], xg_ref.at[pl.ds(r, 1)], gsem)

    def scatter_copy(s, r, a):
        return pltpu.make_async_copy(acc_ref.at[s, pl.ds(r, 1)], y_hbm.at[pl.ds(a, 1)], ssem.at[s])

    def gather_start(r):
        gather_copy(src_ref[off_next + r], r).start()

    def scatter_start(r):
        scatter_copy(1 - slot, r, dst_ref[off_prev + r]).start()

    def eager_issue(first, count):
        for q in range(count):
            gather_start(first + q)
            scatter_start(first + q)

    def tile_rows(i):
        return pl.ds(pl.multiple_of(i * MOE_TMI, MOE_TMI), MOE_TMI)

    def begin():
        @pl.when(g == 0)
        def _():
            xg_ref[...] = jnp.zeros_like(xg_ref)
            acc_ref[...] = jnp.zeros_like(acc_ref)
            spare = pltpu.make_async_copy(acc_ref.at[0], y_hbm.at[pl.ds(n_assign, MOE_SB)], ssem.at[0])
            spare.start()
            spare.wait()
            _for_rows(0, n, lambda r: gather_copy(src_ref[so_ref[0] + r], r).start())
            for t in range(2):
                @pl.when(t < n_slices)
                def _(t=t):
                    for c in weight_copies(t, t):
                        c.start()

            @pl.when(n_slices > 0)
            def _():
                for c in weight_copies(0, 0):
                    c.wait()
                cast_weights(0)

        nt_prev = tiles_of(n_prev)
        eager_prev = (nj // 2) * nt_prev * MOE_Q_TILE
        n_prev2 = jnp.where(g >= 2, sn_ref[jnp.maximum(g - 2, 0)], 0)
        gathered = jnp.maximum(eager_prev, n)
        scattered = jnp.maximum(eager_prev, n_prev2)
        _wait_rows(gathered,
                   lambda k: pltpu.make_async_copy(h2_hbm.at[pl.ds(0, k)], xg_ref.at[pl.ds(0, k)], gsem))
        _wait_rows(scattered,
                   lambda k: pltpu.make_async_copy(acc_ref.at[slot, pl.ds(0, k)], y_hbm.at[pl.ds(0, k)],
                                                   ssem.at[slot]))

        def prep(i, carry):
            rows = tile_rows(i)
            xb_ref[rows, :] = xg_ref[rows, :].astype(BF16)
            acc_ref[slot, rows, :] = jnp.broadcast_to(b2_ref[...], (MOE_TMI, D_MODEL))
            return carry

        lax.fori_loop(0, nt, prep, 0)

    begin()

    def hidden_slice(j, ws):
        t = g * nj + j

        @pl.when(t + 1 < n_slices)
        def _():
            for c in weight_copies(t + 1, 1 - ws):
                c.wait()

        @pl.when(t + 2 < n_slices)
        def _():
            for c in weight_copies(t + 2, ws):
                c.start()

        b1g = b1_ref[pl.ds(j, 1), :]
        b1l = b1_ref[pl.ds(nj + j, 1), :]

        def tiles(first_tile, count, cast_next):
            if ws == 0:
                eager_issue(pl.multiple_of((j // 2) * step_rows + first_tile * MOE_Q_TILE, MOE_Q_TILE),
                            count * MOE_Q_TILE)
            rows = [tile_rows(first_tile + i) for i in range(count)]
            xs = [xb_ref[r, :] for r in rows]
            hid = [(jnp.dot(x, wb1g_ref[ws], preferred_element_type=F32) + b1g,
                    jnp.dot(x, wb1l_ref[ws], preferred_element_type=F32) + b1l) for x in xs]
            if cast_next:
                cast_weights(1 - ws)
            parts = []
            for hg, hl in hid:
                hg = jnp.minimum(hg, SWIGLU_LIMIT)
                hl = jnp.clip(hl, -SWIGLU_LIMIT, SWIGLU_LIMIT)
                act = hg * _sigmoid(SWIGLU_ALPHA * hg) * (hl + 1.0)
                parts.append(jnp.dot(act.astype(BF16), wb2_ref[ws], preferred_element_type=F32))
            for r, part in zip(rows, parts):
                acc_ref[slot, r, :] += part

        @pl.when(nt == 4)
        def _():
            tiles(0, 4, True)

        @pl.when((nt == 2) | (nt == 3))
        def _():
            tiles(0, 2, True)

        @pl.when(nt == 1)
        def _():
            tiles(0, 1, True)

        @pl.when(nt == 3)
        def _():
            tiles(2, 1, False)

    def slice_pair(jj, carry):
        for ws in range(2):
            hidden_slice(2 * jj + ws, ws)
        return carry

    @pl.when(nt > 0)
    def _():
        lax.fori_loop(0, nj // 2, slice_pair, 0)

    _for_rows(jnp.minimum(eager, n_next), n_next, gather_start)
    _for_rows(jnp.minimum(eager, n_prev), n_prev, scatter_start)

    @pl.when(g == MOE_G - 1)
    def _():
        _wait_rows(n_prev, lambda k: pltpu.make_async_copy(acc_ref.at[1 - slot, pl.ds(0, k)],
                                                            y_hbm.at[pl.ds(0, k)], ssem.at[1 - slot]))


def _moe(sb_expert, sb_off, sb_n, n_active, src_tok, dst_row, h2, w1, b1, w2, b2):
    grid_spec = pltpu.PrefetchScalarGridSpec(
        num_scalar_prefetch=6,
        grid=(MOE_G,),
        in_specs=[
            pl.BlockSpec(memory_space=pl.ANY),
            pl.BlockSpec(memory_space=pl.ANY),
            pl.BlockSpec((None, 2 * MOE_NJ, MOE_TH), lambda g, se, *_: (se[g], 0, 0)),
            pl.BlockSpec(memory_space=pl.ANY),
            pl.BlockSpec((None, 1, D_MODEL), lambda g, se, *_: (se[g], 0, 0)),
        ],
        out_specs=pl.BlockSpec(memory_space=pl.ANY),
        scratch_shapes=[pltpu.VMEM((MOE_SB, D_MODEL), F32),
                        pltpu.VMEM((MOE_SB, D_MODEL), BF16),
                        pltpu.VMEM((2, MOE_SB, D_MODEL), F32),
                        pltpu.VMEM((2, D_MODEL, MOE_TH), F32),
                        pltpu.VMEM((2, D_MODEL, MOE_TH), F32),
                        pltpu.VMEM((2, MOE_TH, D_MODEL), F32),
                        pltpu.VMEM((2, D_MODEL, MOE_TH), BF16),
                        pltpu.VMEM((2, D_MODEL, MOE_TH), BF16),
                        pltpu.VMEM((2, MOE_TH, D_MODEL), BF16),
                        pltpu.SemaphoreType.DMA(()),
                        pltpu.SemaphoreType.DMA((2,)),
                        pltpu.SemaphoreType.DMA((2,))],
    )
    return pl.pallas_call(
        _moe_kernel,
        grid_spec=grid_spec,
        out_shape=jax.ShapeDtypeStruct((N_TOK * TOP_K + MOE_SB, D_MODEL), F32),
        compiler_params=_cparams(("arbitrary",)),
        name="moe",
    )(sb_expert, sb_off, sb_n, n_active, src_tok, dst_row, h2, w1, b1, w2, b2)


def _final_kernel(x1_ref, y0_ref, y1_ref, y2_ref, y3_ref, gate_ref, g2_ref, lg_ref, lb_ref, o_ref):
    y = gate_ref[:, 0:1] * y0_ref[...]
    for k, y_ref in enumerate((y1_ref, y2_ref, y3_ref), start=1):
        y = y + gate_ref[:, k:k + 1] * y_ref[...]
    z = DEEPNORM_ALPHA * x1_ref[...] + g2_ref[...] * y
    o_ref[...] = _layer_norm(z) * lg_ref[...] + lb_ref[...]


def _final(x1, y4, gates, g2, ln_g, ln_b):
    tiles_per_batch = SEQ // FIN_TM
    tiles = N_TOK // FIN_TM
    rows = lambda w: pl.BlockSpec((FIN_TM, w), lambda i: (i, 0))
    plane = lambda k: pl.BlockSpec((FIN_TM, D_MODEL), lambda i: (k * tiles + i, 0))
    vec = pl.BlockSpec((1, D_MODEL), lambda i: (0, 0))
    return pl.pallas_call(
        _final_kernel,
        grid=(tiles,),
        in_specs=[rows(D_MODEL), plane(0), plane(1), plane(2), plane(3), rows(LANES),
                  pl.BlockSpec((None, 1, D_MODEL), lambda i: (i // tiles_per_batch, 0, 0)), vec, vec],
        out_specs=rows(D_MODEL),
        out_shape=jax.ShapeDtypeStruct((N_TOK, D_MODEL), F32),
        compiler_params=_cparams(("arbitrary",)),
        name="final",
    )(x1, y4, y4, y4, y4, gates, g2, ln_g, ln_b)


def _rope_tables():
    rows = SEQ // GRID_W
    t = np.arange(SEQ)
    row = (t // GRID_W - rows // 2).astype(np.float64)
    col = (t % GRID_W - GRID_W // 2).astype(np.float64)
    inv_freq = ROPE_THETA ** (-np.arange(0, ROPE_AXIS_DIM, 2, dtype=np.float64) / ROPE_AXIS_DIM)
    ang_row = row[:, None] * inv_freq[None, :]
    ang_col = col[:, None] * inv_freq[None, :]
    zeros = np.zeros_like(ang_row)
    cos = np.concatenate([np.cos(ang_row)] * 2 + [np.cos(ang_col)] * 2, axis=-1)
    sin_lo = np.concatenate([-np.sin(ang_row), zeros, -np.sin(ang_col), zeros], axis=-1)
    sin_hi = np.concatenate([zeros, np.sin(ang_row), zeros, np.sin(ang_col)], axis=-1)
    return tuple(jnp.asarray(a.astype(np.float32)) for a in (cos, sin_lo, sin_hi))


def _routing(top_i, rank, counts):
    counts = counts.astype(jnp.int32)
    nsb = (counts + MOE_SB - 1) // MOE_SB
    sb_end = jnp.cumsum(nsb)
    sb_start = sb_end - nsb
    g = jnp.arange(MOE_G, dtype=jnp.int32)
    active = g < sb_end[-1]
    e_of_g = jnp.minimum(jnp.sum(g[:, None] >= sb_end[None, :], axis=1), N_EXPERTS - 1).astype(jnp.int32)
    first_row = (g - sb_start[e_of_g]) * MOE_SB
    n_of_g = jnp.where(active, jnp.clip(counts[e_of_g] - first_row, 0, MOE_SB), 0).astype(jnp.int32)
    order = jnp.argsort(-n_of_g, stable=True).astype(jnp.int32)
    place = jnp.zeros((MOE_G,), jnp.int32).at[order].set(g)
    sb_n = n_of_g[order]
    sb_off = (jnp.cumsum(sb_n) - sb_n).astype(jnp.int32)
    last_e = e_of_g[order[jnp.maximum(sb_end[-1] - 1, 0)]]
    sb_expert = jnp.where(sb_n > 0, e_of_g[order], last_e).astype(jnp.int32)
    assign = jnp.arange(N_TOK * TOP_K, dtype=jnp.int32)
    max_chunks = N_TOK // MOE_SB
    chunk_ids = jnp.arange(max_chunks, dtype=jnp.int32)
    base = sb_off[place[jnp.minimum(sb_start[:, None] + chunk_ids[None, :], MOE_G - 1)]]
    is_e = top_i[:, :, None] == jnp.arange(N_EXPERTS, dtype=jnp.int32)
    is_c = (rank // MOE_SB)[:, :, None] == chunk_ids
    base_e = jnp.sum(jnp.where(is_e[:, :, :, None], base[None, None], 0), axis=2)
    dest = (jnp.sum(jnp.where(is_c, base_e, 0), axis=-1) + rank % MOE_SB).reshape(-1)
    sorted_assign = lax.sort_key_val(dest, assign)[1]
    src_tok = sorted_assign // TOP_K
    dst_row = (sorted_assign % TOP_K) * N_TOK + src_tok
    src_tok = jnp.concatenate([src_tok, jnp.zeros((MOE_SB,), jnp.int32)])
    dst_row = jnp.concatenate([dst_row, N_TOK * TOP_K + jnp.arange(MOE_SB, dtype=jnp.int32)])
    n_active = sb_end[-1:].astype(jnp.int32)
    return sb_expert, sb_off, sb_n, n_active, src_tok, dst_row


def kernel(x, c, w_ada, b_ada, w_in, q_norm_w, k_norm_w, attn_norm_w, hgrn_lb, hgrn_norm_w, w_out, ln1_g, ln1_b, w_router, b_router, w_exp_in, b_exp_in, w_exp_out, b_exp_out, ln2_g, ln2_b):
    c_pad = jnp.zeros((8, D_MODEL), F32).at[:BATCH].set(c)
    cos, sin_lo, sin_hi = _rope_tables()
    x2 = x.reshape(N_TOK, D_MODEL)
    for l in range(DEPTH):
        mod = _ada(c_pad, w_ada[l], b_ada[l][None, :])[:BATCH]
        sh1, sc1, g1, sh2, sc2, g2 = [m.reshape(BATCH, 1, D_MODEL) for m in jnp.split(mod, 6, axis=-1)]

        proj = _proj(x2, sc1, sh1, w_in[l].astype(BF16))
        lb = jnp.cumsum(jax.nn.softmax(hgrn_lb.astype(F32), axis=1), axis=1)[:, l]
        o_attn, o_r = _mixers(proj, cos, sin_lo, sin_hi, q_norm_w[l][None, :], k_norm_w[l][None, :],
                              attn_norm_w[l][None, :], lb.reshape(2, 1, HGRN_WIDTH), hgrn_norm_w[l][None, :])

        w_o = w_out[l].astype(BF16)
        w_rt = jnp.zeros((D_MODEL, LANES), BF16).at[:, :N_EXPERTS].set(w_router[l].astype(BF16))
        b_rt = jnp.full((1, LANES), -1e30, F32).at[0, :N_EXPERTS].set(b_router[l])
        x1, h2, idx, gates, rank, counts = _mix(
            o_attn, o_r, w_o[:ATTN_WIDTH], w_o[ATTN_WIDTH:], x2, g1, sc2, sh2,
            ln1_g[l][None, :], ln1_b[l][None, :], w_rt, b_rt)
        sb_expert, sb_off, sb_n, n_active, src_tok, dst_row = _routing(idx[:, :TOP_K], rank[:, :TOP_K],
                                                                       counts[0, :N_EXPERTS])
        y4 = _moe(sb_expert, sb_off, sb_n, n_active, src_tok, dst_row, h2, w_exp_in[l],
                  b_exp_in[l].reshape(N_EXPERTS, 2 * MOE_NJ, MOE_TH), w_exp_out[l], b_exp_out[l][:, None, :])
        x2 = _final(x1, y4, gates, g2, ln2_g[l][None, :], ln2_b[l][None, :])
    return x2.reshape(BATCH, SEQ, D_MODEL)
```

```python
import math

import numpy as np
import jax
import jax.numpy as jnp
from jax import lax
from jax.experimental import pallas as pl
from jax.experimental.pallas import tpu as pltpu

F32 = jnp.float32
BF16 = jnp.bfloat16

D_MODEL = 2048
BATCH = 4
SEQ = 2048
DEPTH = 1
N_TOK = BATCH * SEQ
HEAD_DIM = 128
ATTN_WIDTH = 1024
N_Q_HEADS = 8
N_KV_HEADS = 2
KV_GROUP = 4
HGRN_WIDTH = 1024
N_HGRN_HEADS = 8
HGRN_CHUNK = 64
GRID_W = 64
ROPE_THETA = 10000.0
ROPE_AXIS_DIM = 64
N_EXPERTS = 32
TOP_K = 4
D_EXPERT = 2048
SWIGLU_LIMIT = 7.0
SWIGLU_ALPHA = 1.702
NORM_EPS = 1e-6
DEEPNORM_ALPHA = (2 * DEPTH) ** 0.25
PROJ_WIDTH = 6656
LANES = 128

COL_Q = 0
COL_K = 8
COL_V = 10
COL_QR = 12
COL_FF = 20
COL_FB = 28
COL_IN = 36
COL_GO = 44

V7X_VMEM_BYTES = 64 * 1024 * 1024
VMEM_LIMIT = V7X_VMEM_BYTES * 7 // 8

ADA_TN = 2048
PROJ_TM = 512
PROJ_TN = 3328
ATTN_TQ = 256
HGRN_HB = 4
HGRN_UN = 4
MIX_TM = 512
MIX_SUB = 128
MOE_ISSUE_UNROLL = 8
MOE_SB = 1024
MOE_TMI = 256
MOE_TH = 256
MOE_NJ = D_EXPERT // MOE_TH
MOE_G = N_TOK * TOP_K // MOE_SB + N_EXPERTS + 1
MOE_Q_TILE = MOE_SB // (MOE_NJ // 2) // (MOE_SB // MOE_TMI)
FIN_TM = 512


def _cparams(sem):
    return pltpu.CompilerParams(dimension_semantics=sem, vmem_limit_bytes=VMEM_LIMIT)


def _sigmoid(x):
    return 1.0 / (1.0 + jnp.exp(-x))


def _layer_norm(x):
    mu = jnp.mean(x, axis=-1, keepdims=True)
    xc = x - mu
    var = jnp.mean(xc * xc, axis=-1, keepdims=True)
    return xc * lax.rsqrt(var + NORM_EPS)


def _rms(x):
    return x * lax.rsqrt(jnp.mean(x * x, axis=-1, keepdims=True) + NORM_EPS)


def _ada_kernel(c_ref, w_ref, b_ref, o_ref):
    c = c_ref[...]
    ca = c * _sigmoid(c)
    o_ref[...] = jnp.dot(ca.astype(BF16), w_ref[...].astype(BF16),
                         preferred_element_type=F32) + b_ref[...]


def _ada(c_pad, w, b):
    n = w.shape[1]
    return pl.pallas_call(
        _ada_kernel,
        grid=(n // ADA_TN,),
        in_specs=[pl.BlockSpec((8, D_MODEL), lambda j: (0, 0)),
                  pl.BlockSpec((D_MODEL, ADA_TN), lambda j: (0, j)),
                  pl.BlockSpec((1, ADA_TN), lambda j: (0, j))],
        out_specs=pl.BlockSpec((8, ADA_TN), lambda j: (0, j)),
        out_shape=jax.ShapeDtypeStruct((8, n), F32),
        compiler_params=_cparams(("arbitrary",)),
        name="ada",
    )(c_pad, w, b)


def _proj_kernel(x_ref, sc_ref, sh_ref, w_ref, o_ref, h_ref):
    @pl.when(pl.program_id(1) == 0)
    def _():
        h = _layer_norm(x_ref[...]) * (1.0 + sc_ref[...]) + sh_ref[...]
        h_ref[...] = h.astype(BF16)

    o_ref[...] = jnp.dot(h_ref[...], w_ref[...], preferred_element_type=F32).astype(BF16)


def _proj(x2, sc, sh, w_bf):
    tiles_per_batch = SEQ // PROJ_TM
    return pl.pallas_call(
        _proj_kernel,
        grid=(N_TOK // PROJ_TM, PROJ_WIDTH // PROJ_TN),
        in_specs=[pl.BlockSpec((PROJ_TM, D_MODEL), lambda i, j: (i, 0)),
                  pl.BlockSpec((None, 1, D_MODEL), lambda i, j: (i // tiles_per_batch, 0, 0)),
                  pl.BlockSpec((None, 1, D_MODEL), lambda i, j: (i // tiles_per_batch, 0, 0)),
                  pl.BlockSpec((D_MODEL, PROJ_TN), lambda i, j: (0, j))],
        out_specs=pl.BlockSpec((PROJ_TM, PROJ_TN), lambda i, j: (i, j)),
        out_shape=jax.ShapeDtypeStruct((N_TOK, PROJ_WIDTH), BF16),
        scratch_shapes=[pltpu.VMEM((PROJ_TM, D_MODEL), BF16)],
        compiler_params=_cparams(("arbitrary", "arbitrary")),
        name="proj",
    )(x2, sc, sh, w_bf)


def _rope(x, cos, sin_lo, sin_hi):
    return (x * cos + pltpu.roll(x, 96, axis=1) * sin_lo + pltpu.roll(x, 32, axis=1) * sin_hi)


def _attn_kernel(q_ref, k_ref, v_ref, cq_ref, slq_ref, shq_ref, ck_ref, slk_ref, shk_ref,
                 qw_ref, kw_ref, aw_ref, o_ref, kr_ref, v1_ref):
    @pl.when(pl.program_id(2) == 0)
    def _():
        k = _rms(k_ref[...].astype(F32)) * kw_ref[...]
        kr_ref[...] = _rope(k, ck_ref[...], slk_ref[...], shk_ref[...]).astype(BF16)
        v1_ref[:, :HEAD_DIM] = v_ref[...]
        v1_ref[:, HEAD_DIM:] = jnp.ones((SEQ, HEAD_DIM), BF16)

    scale = math.log2(math.e) / math.sqrt(HEAD_DIM)
    cq = cq_ref[...]
    slq = slq_ref[...]
    shq = shq_ref[...]
    heads = [slice(h * HEAD_DIM, (h + 1) * HEAD_DIM) for h in range(KV_GROUP)]
    qs = []
    for cols in heads:
        q = _rms(q_ref[:, cols].astype(F32)) * qw_ref[...]
        qs.append((_rope(q, cq, slq, shq) * scale).astype(BF16))
    scores = [lax.dot_general(q, kr_ref[...], (((1,), (1,)), ((), ())), preferred_element_type=F32)
              for q in qs]
    outs = []
    for s in scores:
        p = jnp.exp2((s - jnp.max(s, axis=-1, keepdims=True)).astype(BF16))
        ov = jnp.dot(p, v1_ref[...], preferred_element_type=F32)
        outs.append(ov[:, :HEAD_DIM] / ov[:, HEAD_DIM:HEAD_DIM + 1])
    for cols, o in zip(heads, outs):
        o_ref[:, cols] = (_rms(o) * aw_ref[:, cols]).astype(BF16)


def _attention(proj, cos, sin_lo, sin_hi, qw, kw, aw):
    nq = SEQ // ATTN_TQ
    gw = KV_GROUP * HEAD_DIM
    tab_q = pl.BlockSpec((ATTN_TQ, HEAD_DIM), lambda b, g, i: (i, 0))
    tab_k = pl.BlockSpec((SEQ, HEAD_DIM), lambda b, g, i: (0, 0))
    return pl.pallas_call(
        _attn_kernel,
        grid=(BATCH, N_KV_HEADS, nq),
        in_specs=[pl.BlockSpec((ATTN_TQ, gw), lambda b, g, i: (b * nq + i, g)),
                  pl.BlockSpec((SEQ, HEAD_DIM), lambda b, g, i: (b, COL_K + g)),
                  pl.BlockSpec((SEQ, HEAD_DIM), lambda b, g, i: (b, COL_V + g)),
                  tab_q, tab_q, tab_q, tab_k, tab_k, tab_k,
                  pl.BlockSpec((1, HEAD_DIM), lambda b, g, i: (0, 0)),
                  pl.BlockSpec((1, HEAD_DIM), lambda b, g, i: (0, 0)),
                  pl.BlockSpec((1, gw), lambda b, g, i: (0, g))],
        out_specs=pl.BlockSpec((ATTN_TQ, gw), lambda b, g, i: (b * nq + i, g)),
        out_shape=jax.ShapeDtypeStruct((N_TOK, ATTN_WIDTH), BF16),
        scratch_shapes=[pltpu.VMEM((SEQ, HEAD_DIM), BF16), pltpu.VMEM((SEQ, 2 * HEAD_DIM), BF16)],
        compiler_params=_cparams(("arbitrary", "arbitrary", "arbitrary")),
        name="attn",
    )(proj, proj, proj, cos, sin_lo, sin_hi, cos, sin_lo, sin_hi, qw, kw, aw)


def _hgrn_kernel(qr_ref, ff_ref, fb_ref, iv_ref, go_ref, lb_ref, nw_ref, o_ref, acc_ref, st_ref):
    C = HGRN_CHUNK
    nc = SEQ // C
    trips = nc // HGRN_UN
    row = lax.broadcasted_iota(jnp.int32, (C, C), 0)
    col = lax.broadcasted_iota(jnp.int32, (C, C), 1)
    keeps = (row >= col, row <= col)
    lasts = (C - 1, 0)
    f_refs = (ff_ref, fb_ref)

    nt_dims = (((1,), (1,)), ((), ()))
    tn_dims = (((0,), (0,)), ((), ()))
    st_ref[...] = jnp.zeros_like(st_ref)

    def trip(it, finish):
        chains = []
        for h in range(HGRN_HB):
            cols = slice(h * HEAD_DIM, (h + 1) * HEAD_DIM)
            for d in range(2):
                for u in range(HGRN_UN):
                    n = it * HGRN_UN + u
                    cidx = n if d == 0 else nc - 1 - n
                    chains.append(dict(h=h, d=d, cols=cols, rows=pl.ds(pl.multiple_of(cidx * C, C), C)))

        for ch in chains:
            d = ch["d"]
            lb = lb_ref[d, :, ch["cols"]]
            fg = lb + (1.0 - lb) * _sigmoid(f_refs[d][ch["rows"], ch["cols"]].astype(F32))
            ch["kk"] = 1.0 - fg
            lf = jnp.log(fg)
            lf_hi = lf.astype(BF16)
            lf_lo = (lf - lf_hi.astype(F32)).astype(BF16)
            tri = jnp.where(keeps[d], 1.0, 0.0).astype(BF16)
            ch["b"] = (jnp.dot(tri, lf_hi, preferred_element_type=F32)
                       + jnp.dot(tri, lf_lo, preferred_element_type=F32))
        for ch in chains:
            b = ch["b"]
            bl = b[lasts[ch["d"]]:lasts[ch["d"]] + 1, :]
            qx = qr_ref[ch["rows"], ch["cols"]].astype(F32)
            ch["qd"] = (qx * _sigmoid(qx) * jnp.exp(b)).astype(BF16)
            kd = ch["kk"] * jnp.exp(-b)
            ch["decay"] = jnp.exp(bl)
            ku = (kd * ch["decay"]).astype(BF16)
            kd = kd.astype(BF16)
            ch["v"] = iv_ref[ch["rows"], ch["cols"]]
            ch["sc"] = lax.dot_general(ch["qd"], kd, nt_dims, preferred_element_type=F32)
            ch["u_t"] = lax.dot_general(ch["v"], ku, tn_dims, preferred_element_type=F32)
        for h in range(HGRN_HB):
            for d in range(2):
                state = st_ref[2 * h + d]
                for ch in chains:
                    if ch["h"] == h and ch["d"] == d:
                        ch["state"] = state.astype(BF16)
                        state = state * ch["decay"] + ch["u_t"]
                st_ref[2 * h + d] = state
        for ch in chains:
            sc = jnp.where(keeps[ch["d"]], ch["sc"], 0.0).astype(BF16)
            ch["o"] = (jnp.dot(sc, ch["v"], preferred_element_type=F32)
                       + lax.dot_general(ch["qd"], ch["state"], nt_dims, preferred_element_type=F32))
        for ch in chains:
            rows, cols = ch["rows"], ch["cols"]
            if finish:
                o = _rms(acc_ref[rows, cols] + ch["o"]) * nw_ref[:, cols]
                g = go_ref[rows, cols].astype(F32)
                o_ref[rows, cols] = (o * (g * _sigmoid(g))).astype(BF16)
            else:
                acc_ref[rows, cols] = ch["o"]

    def first_half(it, carry):
        trip(it, False)
        return carry

    def second_half(it, carry):
        trip(it, True)
        return carry

    lax.fori_loop(0, trips // 2, first_half, 0)
    lax.fori_loop(trips // 2, trips, second_half, 0)


def _hgrn(proj, lb, nw):
    width = HGRN_HB * HEAD_DIM

    def col(c0):
        return pl.BlockSpec((SEQ, width), lambda b, h: (b, c0 // HGRN_HB + h))

    return pl.pallas_call(
        _hgrn_kernel,
        grid=(BATCH, N_HGRN_HEADS // HGRN_HB),
        in_specs=[col(COL_QR), col(COL_FF), col(COL_FB), col(COL_IN), col(COL_GO),
                  pl.BlockSpec((2, 1, width), lambda b, h: (0, 0, h)),
                  pl.BlockSpec((1, width), lambda b, h: (0, h))],
        out_specs=pl.BlockSpec((SEQ, width), lambda b, h: (b, h)),
        out_shape=jax.ShapeDtypeStruct((N_TOK, HGRN_WIDTH), BF16),
        scratch_shapes=[pltpu.VMEM((SEQ, width), F32),
                        pltpu.VMEM((2 * HGRN_HB, HEAD_DIM, HEAD_DIM), F32)],
        compiler_params=_cparams(("arbitrary", "arbitrary")),
        name="hgrn",
    )(proj, proj, proj, proj, proj, lb, nw)


def _mix_kernel(oa_ref, or_ref, wa_ref, wr_ref, x_ref, g1_ref, sc_ref, sh_ref, lg_ref, lbias_ref,
                wrt_ref, brt_ref, x1_ref, h2_ref, idx_ref, gate_ref, rank_ref, cnt_ref, carry_ref):
    i = pl.program_id(0)

    @pl.when(i == 0)
    def _():
        carry_ref[...] = jnp.zeros_like(carry_ref)

    tm = MIX_SUB
    subs = [slice(u * tm, (u + 1) * tm) for u in range(MIX_TM // MIX_SUB)]
    lane = lax.broadcasted_iota(jnp.int32, (tm, LANES), 1)
    neg = jnp.float32(-jnp.inf)

    ys = [jnp.dot(oa_ref[rs, :], wa_ref[...], preferred_element_type=F32)
          + jnp.dot(or_ref[rs, :], wr_ref[...], preferred_element_type=F32) for rs in subs]
    h2s = []
    for rs, y in zip(subs, ys):
        x1 = _layer_norm(DEEPNORM_ALPHA * x_ref[rs, :] + g1_ref[...] * y) * lg_ref[...] + lbias_ref[...]
        x1_ref[rs, :] = x1
        h2 = _layer_norm(x1) * (1.0 + sc_ref[...]) + sh_ref[...]
        h2_ref[rs, :] = h2
        h2s.append(h2.astype(BF16))
    logit_list = [jnp.dot(h2, wrt_ref[...], preferred_element_type=F32) + brt_ref[...] for h2 in h2s]

    picks = []
    for work in logit_list:
        vals, sels = [], []
        for _ in range(TOP_K):
            m = jnp.max(work, axis=-1, keepdims=True)
            sel = jnp.min(jnp.where(work == m, lane, LANES), axis=-1, keepdims=True)
            vals.append(m)
            sels.append(sel)
            work = jnp.where(lane == sel, neg, work)
        es = [jnp.exp(v - vals[0]) for v in vals]
        multi = jnp.zeros((tm, LANES), F32)
        for sel in sels:
            multi = multi + jnp.where(lane == sel, 1.0, 0.0)
        picks.append((sels, es, es[0] + es[1] + es[2] + es[3], multi))

    r = lax.broadcasted_iota(jnp.int32, (tm, tm), 0)
    c = lax.broadcasted_iota(jnp.int32, (tm, tm), 1)
    strict = jnp.where(r > c, 1.0, 0.0).astype(BF16)
    within = [jnp.dot(strict, multi.astype(BF16), preferred_element_type=F32) for _, _, _, multi in picks]
    carry = carry_ref[...]
    for rs, (sels, es, denom, multi), inside in zip(subs, picks, within):
        before = inside + carry
        carry = carry + jnp.sum(multi, axis=0, keepdims=True)
        idx_out = jnp.zeros((tm, LANES), jnp.int32)
        gate_out = jnp.zeros((tm, LANES), F32)
        rank_out = jnp.zeros((tm, LANES), F32)
        for k in range(TOP_K):
            rk = jnp.sum(jnp.where(lane == sels[k], before, 0.0), axis=-1, keepdims=True)
            idx_out = jnp.where(lane == k, sels[k], idx_out)
            gate_out = jnp.where(lane == k, es[k] / denom, gate_out)
            rank_out = jnp.where(lane == k, rk, rank_out)
        idx_ref[rs, :] = idx_out
        gate_ref[rs, :] = gate_out
        rank_ref[rs, :] = rank_out.astype(jnp.int32)
    carry_ref[...] = carry
    cnt_ref[...] = carry


def _mix(o_attn, o_r, wa, wr, x2, g1, sc2, sh2, ln_g, ln_b, w_rt, b_rt):
    tiles_per_batch = SEQ // MIX_TM
    rows = lambda w: pl.BlockSpec((MIX_TM, w), lambda i: (i, 0))
    full = lambda a, b: pl.BlockSpec((a, b), lambda i: (0, 0))
    per_batch = pl.BlockSpec((None, 1, D_MODEL), lambda i: (i // tiles_per_batch, 0, 0))
    return pl.pallas_call(
        _mix_kernel,
        grid=(N_TOK // MIX_TM,),
        in_specs=[rows(ATTN_WIDTH), rows(HGRN_WIDTH), full(ATTN_WIDTH, D_MODEL), full(HGRN_WIDTH, D_MODEL),
                  rows(D_MODEL), per_batch, per_batch, per_batch, full(1, D_MODEL), full(1, D_MODEL),
                  full(D_MODEL, LANES), full(1, LANES)],
        out_specs=[rows(D_MODEL), rows(D_MODEL), rows(LANES), rows(LANES), rows(LANES), full(1, LANES)],
        out_shape=[jax.ShapeDtypeStruct((N_TOK, D_MODEL), F32),
                   jax.ShapeDtypeStruct((N_TOK, D_MODEL), F32),
                   jax.ShapeDtypeStruct((N_TOK, LANES), jnp.int32),
                   jax.ShapeDtypeStruct((N_TOK, LANES), F32),
                   jax.ShapeDtypeStruct((N_TOK, LANES), jnp.int32),
                   jax.ShapeDtypeStruct((1, LANES), F32)],
        scratch_shapes=[pltpu.VMEM((1, LANES), F32)],
        compiler_params=_cparams(("arbitrary",)),
        name="mix",
    )(o_attn, o_r, wa, wr, x2, g1, sc2, sh2, ln_g, ln_b, w_rt, b_rt)


def _wait_rows(n, make_copy):
    for bit in range(MOE_SB.bit_length()):
        @pl.when(((n >> bit) & 1) == 1)
        def _(bit=bit):
            make_copy(1 << bit).wait()


def _for_rows(lo, hi, fn):
    groups = (hi - lo) // MOE_ISSUE_UNROLL

    def group(q, carry):
        base = lo + q * MOE_ISSUE_UNROLL
        for u in range(MOE_ISSUE_UNROLL):
            fn(base + u)
        return carry

    def single(r, carry):
        fn(r)
        return carry

    lax.fori_loop(0, groups, group, 0)
    lax.fori_loop(lo + groups * MOE_ISSUE_UNROLL, hi, single, 0)


def _moe_kernel(se_ref, so_ref, sn_ref, na_ref, src_ref, dst_ref, h2_hbm, w1_hbm, b1_ref, w2_hbm, b2_ref,
                y_hbm, xg_ref, xb_ref, acc_ref, wf1g_ref, wf1l_ref, wf2_ref, wb1g_ref, wb1l_ref, wb2_ref,
                gsem, ssem, wsem):
    g = pl.program_id(0)
    nj = MOE_NJ
    n_assign = N_TOK * TOP_K
    n_slices = na_ref[0] * nj

    def weight_copies(t, s):
        e = se_ref[jnp.minimum(t // nj, MOE_G - 1)]
        col = pl.multiple_of((t % nj) * MOE_TH, MOE_TH)
        return (pltpu.make_async_copy(w1_hbm.at[e, :, pl.ds(col, MOE_TH)], wf1g_ref.at[s], wsem.at[s]),
                pltpu.make_async_copy(w1_hbm.at[e, :, pl.ds(D_EXPERT + col, MOE_TH)], wf1l_ref.at[s], wsem.at[s]),
                pltpu.make_async_copy(w2_hbm.at[e, pl.ds(col, MOE_TH), :], wf2_ref.at[s], wsem.at[s]))

    def cast_weights(s):
        wb1g_ref[s] = wf1g_ref[s].astype(BF16)
        wb1l_ref[s] = wf1l_ref[s].astype(BF16)
        wb2_ref[s] = wf2_ref[s].astype(BF16)

    def tiles_of(rows):
        return (rows + MOE_TMI - 1) // MOE_TMI

    n = sn_ref[g]
    nt = tiles_of(n)
    slot = g % 2
    g_next = jnp.minimum(g + 1, MOE_G - 1)
    n_next = jnp.where(g + 1 < MOE_G, sn_ref[g_next], 0)
    off_next = so_ref[g_next]
    g_prev = jnp.maximum(g - 1, 0)
    n_prev = jnp.where(g >= 1, sn_ref[g_prev], 0)
    off_prev = so_ref[g_prev]
    step_rows = nt * MOE_Q_TILE
    eager = (nj // 2) * step_rows

    def gather_copy(tok, r):
        return pltpu.make_async_copy(h2_hbm.at[pl.ds(tok, 1)], xg_ref.at[pl.ds(r, 1)], gsem)

    def scatter_copy(s, r, a):
        return pltpu.make_async_copy(acc_ref.at[s, pl.ds(r, 1)], y_hbm.at[pl.ds(a, 1)], ssem.at[s])

    def gather_start(r):
        gather_copy(src_ref[off_next + r], r).start()

    def scatter_start(r):
        scatter_copy(1 - slot, r, dst_ref[off_prev + r]).start()

    def eager_issue(first, count):
        for q in range(count):
            gather_start(first + q)
            scatter_start(first + q)

    def tile_rows(i):
        return pl.ds(pl.multiple_of(i * MOE_TMI, MOE_TMI), MOE_TMI)

    def begin():
        @pl.when(g == 0)
        def _():
            xg_ref[...] = jnp.zeros_like(xg_ref)
            acc_ref[...] = jnp.zeros_like(acc_ref)
            spare = pltpu.make_async_copy(acc_ref.at[0], y_hbm.at[pl.ds(n_assign, MOE_SB)], ssem.at[0])
            spare.start()
            spare.wait()
            _for_rows(0, n, lambda r: gather_copy(src_ref[so_ref[0] + r], r).start())
            for t in range(2):
                @pl.when(t < n_slices)
                def _(t=t):
                    for c in weight_copies(t, t):
                        c.start()

            @pl.when(n_slices > 0)
            def _():
                for c in weight_copies(0, 0):
                    c.wait()
                cast_weights(0)

        nt_prev = tiles_of(n_prev)
        eager_prev = (nj // 2) * nt_prev * MOE_Q_TILE
        n_prev2 = jnp.where(g >= 2, sn_ref[jnp.maximum(g - 2, 0)], 0)
        gathered = jnp.maximum(eager_prev, n)
        scattered = jnp.maximum(eager_prev, n_prev2)
        _wait_rows(gathered,
                   lambda k: pltpu.make_async_copy(h2_hbm.at[pl.ds(0, k)], xg_ref.at[pl.ds(0, k)], gsem))
        _wait_rows(scattered,
                   lambda k: pltpu.make_async_copy(acc_ref.at[slot, pl.ds(0, k)], y_hbm.at[pl.ds(0, k)],
                                                   ssem.at[slot]))

        def prep(i, carry):
            rows = tile_rows(i)
            xb_ref[rows, :] = xg_ref[rows, :].astype(BF16)
            acc_ref[slot, rows, :] = jnp.broadcast_to(b2_ref[...], (MOE_TMI, D_MODEL))
            return carry

        lax.fori_loop(0, nt, prep, 0)

    begin()

    def hidden_slice(j, ws):
        t = g * nj + j

        @pl.when(t + 1 < n_slices)
        def _():
            for c in weight_copies(t + 1, 1 - ws):
                c.wait()

        @pl.when(t + 2 < n_slices)
        def _():
            for c in weight_copies(t + 2, ws):
                c.start()

        b1g = b1_ref[pl.ds(j, 1), :]
        b1l = b1_ref[pl.ds(nj + j, 1), :]

        def tiles(first_tile, count, cast_next):
            if ws == 0:
                eager_issue(pl.multiple_of((j // 2) * step_rows + first_tile * MOE_Q_TILE, MOE_Q_TILE),
                            count * MOE_Q_TILE)
            rows = [tile_rows(first_tile + i) for i in range(count)]
            xs = [xb_ref[r, :] for r in rows]
            hid = [(jnp.dot(x, wb1g_ref[ws], preferred_element_type=F32) + b1g,
                    jnp.dot(x, wb1l_ref[ws], preferred_element_type=F32) + b1l) for x in xs]
            if cast_next:
                cast_weights(1 - ws)
            parts = []
            for hg, hl in hid:
                hg = jnp.minimum(hg, SWIGLU_LIMIT)
                hl = jnp.clip(hl, -SWIGLU_LIMIT, SWIGLU_LIMIT)
                act = hg * _sigmoid(SWIGLU_ALPHA * hg) * (hl + 1.0)
                parts.append(jnp.dot(act.astype(BF16), wb2_ref[ws], preferred_element_type=F32))
            for r, part in zip(rows, parts):
                acc_ref[slot, r, :] += part

        @pl.when(nt == 4)
        def _():
            tiles(0, 4, True)

        @pl.when((nt == 2) | (nt == 3))
        def _():
            tiles(0, 2, True)

        @pl.when(nt == 1)
        def _():
            tiles(0, 1, True)

        @pl.when(nt == 3)
        def _():
            tiles(2, 1, False)

    def slice_pair(jj, carry):
        for ws in range(2):
            hidden_slice(2 * jj + ws, ws)
        return carry

    @pl.when(nt > 0)
    def _():
        lax.fori_loop(0, nj // 2, slice_pair, 0)

    _for_rows(jnp.minimum(eager, n_next), n_next, gather_start)
    _for_rows(jnp.minimum(eager, n_prev), n_prev, scatter_start)

    @pl.when(g == MOE_G - 1)
    def _():
        _wait_rows(n_prev, lambda k: pltpu.make_async_copy(acc_ref.at[1 - slot, pl.ds(0, k)],
                                                            y_hbm.at[pl.ds(0, k)], ssem.at[1 - slot]))


def _moe(sb_expert, sb_off, sb_n, n_active, src_tok, dst_row, h2, w1, b1, w2, b2):
    grid_spec = pltpu.PrefetchScalarGridSpec(
        num_scalar_prefetch=6,
        grid=(MOE_G,),
        in_specs=[
            pl.BlockSpec(memory_space=pl.ANY),
            pl.BlockSpec(memory_space=pl.ANY),
            pl.BlockSpec((None, 2 * MOE_NJ, MOE_TH), lambda g, se, *_: (se[g], 0, 0)),
            pl.BlockSpec(memory_space=pl.ANY),
            pl.BlockSpec((None, 1, D_MODEL), lambda g, se, *_: (se[g], 0, 0)),
        ],
        out_specs=pl.BlockSpec(memory_space=pl.ANY),
        scratch_shapes=[pltpu.VMEM((MOE_SB, D_MODEL), F32),
                        pltpu.VMEM((MOE_SB, D_MODEL), BF16),
                        pltpu.VMEM((2, MOE_SB, D_MODEL), F32),
                        pltpu.VMEM((2, D_MODEL, MOE_TH), F32),
                        pltpu.VMEM((2, D_MODEL, MOE_TH), F32),
                        pltpu.VMEM((2, MOE_TH, D_MODEL), F32),
                        pltpu.VMEM((2, D_MODEL, MOE_TH), BF16),
                        pltpu.VMEM((2, D_MODEL, MOE_TH), BF16),
                        pltpu.VMEM((2, MOE_TH, D_MODEL), BF16),
                        pltpu.SemaphoreType.DMA(()),
                        pltpu.SemaphoreType.DMA((2,)),
                        pltpu.SemaphoreType.DMA((2,))],
    )
    return pl.pallas_call(
        _moe_kernel,
        grid_spec=grid_spec,
        out_shape=jax.ShapeDtypeStruct((N_TOK * TOP_K + MOE_SB, D_MODEL), F32),
        compiler_params=_cparams(("arbitrary",)),
        name="moe",
    )(sb_expert, sb_off, sb_n, n_active, src_tok, dst_row, h2, w1, b1, w2, b2)


def _final_kernel(x1_ref, y0_ref, y1_ref, y2_ref, y3_ref, gate_ref, g2_ref, lg_ref, lb_ref, o_ref):
    y = gate_ref[:, 0:1] * y0_ref[...]
    for k, y_ref in enumerate((y1_ref, y2_ref, y3_ref), start=1):
        y = y + gate_ref[:, k:k + 1] * y_ref[...]
    z = DEEPNORM_ALPHA * x1_ref[...] + g2_ref[...] * y
    o_ref[...] = _layer_norm(z) * lg_ref[...] + lb_ref[...]


def _final(x1, y4, gates, g2, ln_g, ln_b):
    tiles_per_batch = SEQ // FIN_TM
    tiles = N_TOK // FIN_TM
    rows = lambda w: pl.BlockSpec((FIN_TM, w), lambda i: (i, 0))
    plane = lambda k: pl.BlockSpec((FIN_TM, D_MODEL), lambda i: (k * tiles + i, 0))
    vec = pl.BlockSpec((1, D_MODEL), lambda i: (0, 0))
    return pl.pallas_call(
        _final_kernel,
        grid=(tiles,),
        in_specs=[rows(D_MODEL), plane(0), plane(1), plane(2), plane(3), rows(LANES),
                  pl.BlockSpec((None, 1, D_MODEL), lambda i: (i // tiles_per_batch, 0, 0)), vec, vec],
        out_specs=rows(D_MODEL),
        out_shape=jax.ShapeDtypeStruct((N_TOK, D_MODEL), F32),
        compiler_params=_cparams(("arbitrary",)),
        name="final",
    )(x1, y4, y4, y4, y4, gates, g2, ln_g, ln_b)


def _rope_tables():
    rows = SEQ // GRID_W
    t = np.arange(SEQ)
    row = (t // GRID_W - rows // 2).astype(np.float32)
    col = (t % GRID_W - GRID_W // 2).astype(np.float32)
    inv_freq = jnp.asarray(ROPE_THETA, F32) ** (-jnp.arange(0, ROPE_AXIS_DIM, 2, dtype=F32) / ROPE_AXIS_DIM)
    ang_row = jnp.asarray(row)[:, None] * inv_freq[None, :]
    ang_col = jnp.asarray(col)[:, None] * inv_freq[None, :]
    zeros = jnp.zeros_like(ang_row)
    cos = jnp.concatenate([jnp.cos(ang_row)] * 2 + [jnp.cos(ang_col)] * 2, axis=-1)
    sin_lo = jnp.concatenate([-jnp.sin(ang_row), zeros, -jnp.sin(ang_col), zeros], axis=-1)
    sin_hi = jnp.concatenate([zeros, jnp.sin(ang_row), zeros, jnp.sin(ang_col)], axis=-1)
    return cos, sin_lo, sin_hi


def _routing(top_i, rank, counts):
    counts = counts.astype(jnp.int32)
    nsb = (counts + MOE_SB - 1) // MOE_SB
    sb_end = jnp.cumsum(nsb)
    sb_start = sb_end - nsb
    g = jnp.arange(MOE_G, dtype=jnp.int32)
    active = g < sb_end[-1]
    e_of_g = jnp.minimum(jnp.sum(g[:, None] >= sb_end[None, :], axis=1), N_EXPERTS - 1).astype(jnp.int32)
    first_row = (g - sb_start[e_of_g]) * MOE_SB
    n_of_g = jnp.where(active, jnp.clip(counts[e_of_g] - first_row, 0, MOE_SB), 0).astype(jnp.int32)
    order = jnp.argsort(-n_of_g, stable=True).astype(jnp.int32)
    place = jnp.zeros((MOE_G,), jnp.int32).at[order].set(g)
    sb_n = n_of_g[order]
    sb_off = (jnp.cumsum(sb_n) - sb_n).astype(jnp.int32)
    last_e = e_of_g[order[jnp.maximum(sb_end[-1] - 1, 0)]]
    sb_expert = jnp.where(sb_n > 0, e_of_g[order], last_e).astype(jnp.int32)
    assign = jnp.arange(N_TOK * TOP_K, dtype=jnp.int32)
    max_chunks = N_TOK // MOE_SB
    chunk_ids = jnp.arange(max_chunks, dtype=jnp.int32)
    base = sb_off[place[jnp.minimum(sb_start[:, None] + chunk_ids[None, :], MOE_G - 1)]]
    is_e = top_i[:, :, None] == jnp.arange(N_EXPERTS, dtype=jnp.int32)
    is_c = (rank // MOE_SB)[:, :, None] == chunk_ids
    base_e = jnp.sum(jnp.where(is_e[:, :, :, None], base[None, None], 0), axis=2)
    dest = (jnp.sum(jnp.where(is_c, base_e, 0), axis=-1) + rank % MOE_SB).reshape(-1)
    sorted_assign = lax.sort_key_val(dest, assign)[1]
    src_tok = sorted_assign // TOP_K
    dst_row = (sorted_assign % TOP_K) * N_TOK + src_tok
    src_tok = jnp.concatenate([src_tok, jnp.zeros((MOE_SB,), jnp.int32)])
    dst_row = jnp.concatenate([dst_row, N_TOK * TOP_K + jnp.arange(MOE_SB, dtype=jnp.int32)])
    n_active = sb_end[-1:].astype(jnp.int32)
    return sb_expert, sb_off, sb_n, n_active, src_tok, dst_row


def kernel(x, c, w_ada, b_ada, w_in, q_norm_w, k_norm_w, attn_norm_w, hgrn_lb, hgrn_norm_w, w_out, ln1_g, ln1_b, w_router, b_router, w_exp_in, b_exp_in, w_exp_out, b_exp_out, ln2_g, ln2_b):
    c_pad = jnp.zeros((8, D_MODEL), F32).at[:BATCH].set(c)
    cos, sin_lo, sin_hi = _rope_tables()
    x2 = x.reshape(N_TOK, D_MODEL)
    for l in range(DEPTH):
        mod = _ada(c_pad, w_ada[l], b_ada[l][None, :])[:BATCH]
        sh1, sc1, g1, sh2, sc2, g2 = [m.reshape(BATCH, 1, D_MODEL) for m in jnp.split(mod, 6, axis=-1)]

        proj = _proj(x2, sc1, sh1, w_in[l].astype(BF16))
        o_attn = _attention(proj, cos, sin_lo, sin_hi, q_norm_w[l][None, :], k_norm_w[l][None, :],
                            attn_norm_w[l][None, :])
        lb = jnp.cumsum(jax.nn.softmax(hgrn_lb.astype(F32), axis=1), axis=1)[:, l]
        o_r = _hgrn(proj, lb.reshape(2, 1, HGRN_WIDTH), hgrn_norm_w[l][None, :])

        w_o = w_out[l].astype(BF16)
        w_rt = jnp.zeros((D_MODEL, LANES), BF16).at[:, :N_EXPERTS].set(w_router[l].astype(BF16))
        b_rt = jnp.full((1, LANES), -1e30, F32).at[0, :N_EXPERTS].set(b_router[l])
        x1, h2, idx, gates, rank, counts = _mix(
            o_attn, o_r, w_o[:ATTN_WIDTH], w_o[ATTN_WIDTH:], x2, g1, sc2, sh2,
            ln1_g[l][None, :], ln1_b[l][None, :], w_rt, b_rt)
        sb_expert, sb_off, sb_n, n_active, src_tok, dst_row = _routing(idx[:, :TOP_K], rank[:, :TOP_K],
                                                                       counts[0, :N_EXPERTS])
        y4 = _moe(sb_expert, sb_off, sb_n, n_active, src_tok, dst_row, h2, w_exp_in[l],
                  b_exp_in[l].reshape(N_EXPERTS, 2 * MOE_NJ, MOE_TH), w_exp_out[l], b_exp_out[l][:, None, :])
        x2 = _final(x1, y4, gates, g2, ln2_g[l][None, :], ln2_b[l][None, :])
    return x2.reshape(BATCH, SEQ, D_MODEL)
```
